```python
import math
import functools
import jax
import jax.numpy as jnp
from jax import lax
import numpy as np

D_MODEL = 1024
BATCH = 4
SEQ = 4096
DEPTH = 4
DEC_BATCH = 128
DEC_SEQ = 4
PAST_LEN = 8192
PAGE_SIZE = 128

D_MIX = D_MODEL
HEAD_DIM = 64
N_HEADS = (D_MIX // 2) // HEAD_DIM
N_KV_HEADS = 2
KV_GROUP = N_HEADS // N_KV_HEADS
D_ATTN = N_HEADS * HEAD_DIM
D_KV = N_KV_HEADS * HEAD_DIM
WINDOW = 128
ROPE_THETA = 10000.0
D_SSM = D_MIX // 4
SSM_GROUP = 16
N_SSM_GROUPS = D_SSM // SSM_GROUP
SSM_STATE = 64
D_LRU = D_MIX // 4
N_LRU_BLOCKS = 4
LRU_BLOCK = D_LRU // N_LRU_BLOCKS
LRU_CONV = 4
LRU_C = 8.0
D_FF = 11 * D_MODEL // 4
FFN_CONV = 3
D_IN = D_ATTN + 2 * D_KV + D_SSM + 2 * D_LRU
N_MOD = 6
EPS = 1e-6

kernel_name = 'hymba_s5_swa_rglru_convffn_step'


def rms_norm(x, w):
    xf = x.astype(jnp.float32)
    y = xf * lax.rsqrt(jnp.mean(xf * xf, axis=-1, keepdims=True) + EPS)
    return (y * w.astype(jnp.float32)).astype(x.dtype)


def rotary(x, pos):
    half = HEAD_DIM // 2
    inv = ROPE_THETA ** (-jnp.arange(half, dtype=jnp.float32) / half)
    ang = pos[:, None] * inv[None, :]
    cos = jnp.cos(ang)[None, :, None, :]
    sin = jnp.sin(ang)[None, :, None, :]
    xf = x.astype(jnp.float32)
    x1, x2 = xf[..., :half], xf[..., half:]
    return jnp.concatenate([x1 * cos - x2 * sin, x2 * cos + x1 * sin], axis=-1).astype(x.dtype)


def causal_dwconv(x, buf, w, b):
    k = w.shape[0]
    t = x.shape[1]
    xp = jnp.concatenate([buf.astype(x.dtype), x], axis=1)
    y = b + sum(w[j] * xp[:, j:j + t] for j in range(k))
    return y.astype(x.dtype), xp[:, t:]


def linear_recurrence(a, b, h0):
    b = b.at[:, 0].add(a[:, 0] * h0)

    def combine(l, r):
        return l[0] * r[0], r[0] * l[1] + r[1]

    _, h = lax.associative_scan(combine, (a, b), axis=1)
    return h


def sink_softmax_attend(q, k, v, allowed, sinks):
    s = jnp.einsum('...qkgd,...skd->...kgqs', q.astype(jnp.float32), k.astype(jnp.float32)) * (HEAD_DIM ** -0.5)
    s = jnp.where(allowed, s, -jnp.inf)
    sink = sinks.astype(jnp.float32)[..., None, None]
    m = jnp.maximum(jnp.max(s, axis=-1, keepdims=True), sink)
    p = jnp.exp(s - m)
    p = p / (jnp.sum(p, axis=-1, keepdims=True) + jnp.exp(sink - m))
    o = jnp.einsum('...kgqs,...skd->...qkgd', p, v.astype(jnp.float32))
    return o.astype(v.dtype)


def window_attention_prompt(q, k, v, sinks):
    bsz, s = q.shape[:2]
    nb = s // WINDOW
    qb = q.reshape(bsz, nb, WINDOW, N_KV_HEADS, KV_GROUP, HEAD_DIM)

    def band(t):
        tb = t.reshape(bsz, nb, WINDOW, N_KV_HEADS, HEAD_DIM)
        prev = jnp.concatenate([jnp.zeros_like(tb[:, :1]), tb[:, :-1]], axis=1)
        return jnp.concatenate([prev, tb], axis=2)

    qi = jnp.arange(WINDOW)[:, None]
    si = jnp.arange(2 * WINDOW)[None, :]
    blk = jnp.arange(nb)[:, None, None]
    diff = qi + WINDOW - si
    key_pos = blk * WINDOW - WINDOW + si
    allowed = ((diff >= 0) & (diff < WINDOW) & (key_pos >= 0))[:, None, None]
    o = sink_softmax_attend(qb, band(k), band(v), allowed, sinks.reshape(N_KV_HEADS, KV_GROUP))
    keep = min(WINDOW, s)
    return o.reshape(bsz, s, D_ATTN), k[:, s - keep:], v[:, s - keep:]


def window_attention_sample(q, k, v, sinks, k_buf, v_buf):
    n, t = q.shape[:2]
    w_buf = k_buf.shape[1]
    kk = jnp.concatenate([k_buf.astype(k.dtype), k], axis=1)
    vv = jnp.concatenate([v_buf.astype(v.dtype), v], axis=1)
    q_pos = PAST_LEN + jnp.arange(t)
    k_pos = PAST_LEN - w_buf + jnp.arange(w_buf + t)
    diff = q_pos[:, None] - k_pos[None, :]
    allowed = (diff >= 0) & (diff < WINDOW)
    qg = q.reshape(n, t, N_KV_HEADS, KV_GROUP, HEAD_DIM)
    o = sink_softmax_attend(qg, kk, vv, allowed, sinks.reshape(N_KV_HEADS, KV_GROUP))
    return o.reshape(n, t, D_ATTN), kk[:, t:], vv[:, t:]


def s5_mixer(u, h0_re, h0_im, lp):
    f32 = jnp.float32
    n, t = u.shape[:2]
    lam = lax.complex(lp['ssm_a_re'].astype(f32), lp['ssm_a_im'].astype(f32))
    dt = jnp.exp(lp['ssm_log_dt'].astype(f32))[:, None]
    a_bar = jnp.exp(lam * dt)
    b_mat = lax.complex(lp['ssm_b_re'].astype(f32), lp['ssm_b_im'].astype(f32))
    b_bar = ((a_bar - 1.0) / lam)[..., None] * b_mat
    uf = u.astype(f32)
    ug = uf.reshape(n, t, N_SSM_GROUPS, SSM_GROUP).astype(jnp.complex64)
    bu = jnp.einsum('ntgc,gpc->ntgp', ug, b_bar)
    h0 = lax.complex(h0_re.astype(f32), h0_im.astype(f32))
    h = linear_recurrence(jnp.broadcast_to(a_bar, bu.shape), bu, h0)
    c_mat = lax.complex(lp['ssm_c_re'].astype(f32), lp['ssm_c_im'].astype(f32))
    y = jnp.real(jnp.einsum('ntgp,gcp->ntgc', h, c_mat)).reshape(n, t, D_SSM)
    y = y + lp['ssm_d'].astype(f32) * uf
    g = jax.nn.gelu(y)
    out = g * jax.nn.sigmoid(g @ lp['ssm_w_glu'].astype(f32) + lp['ssm_b_glu'].astype(f32))
    h_last = h[:, -1]
    return out.astype(u.dtype), jnp.real(h_last), jnp.imag(h_last)


def rglru_mixer(xr, yg, conv_buf, h0, pos, lp):
    f32 = jnp.float32
    n, t = xr.shape[:2]
    xc, new_buf = causal_dwconv(xr, conv_buf, lp['lru_conv_w'], lp['lru_conv_b'])
    xf = xc.astype(f32)
    xh = xf.reshape(n, t, N_LRU_BLOCKS, LRU_BLOCK)
    r = jax.nn.sigmoid(jnp.einsum('nthi,hij->nthj', xh, lp['lru_w_a'].astype(f32)).reshape(n, t, D_LRU)
                       + lp['lru_b_a'].astype(f32))
    gi = jax.nn.sigmoid(jnp.einsum('nthi,hij->nthj', xh, lp['lru_w_i'].astype(f32)).reshape(n, t, D_LRU)
                        + lp['lru_b_i'].astype(f32))
    log_a = -LRU_C * r * jax.nn.softplus(-lp['lru_lambda'].astype(f32))
    a = jnp.exp(log_a)
    mult = jnp.sqrt(-jnp.expm1(2.0 * log_a))
    mult = jnp.where((pos == 0)[None, :, None], 1.0, mult)
    h = linear_recurrence(a, mult * gi * xf, h0.astype(f32))
    out = h * jax.nn.gelu(yg.astype(f32))
    return out.astype(xr.dtype), new_buf, h[:, -1]


def conv_ffn(h, buf, lp):
    up = h @ lp['ffn_w_up']
    upc, new_buf = causal_dwconv(up, buf, lp['ffn_conv_w'], lp['ffn_conv_b'])
    gate, val = upc[..., :D_FF], upc[..., D_FF:]
    return (jax.nn.gelu(gate) * val) @ lp['ffn_w_down'], new_buf


def trunk_layer(x, c, pos, lp, ssm_re, ssm_im, lru_h, lru_conv, ffn_conv, attend):
    n, t, _ = x.shape
    mod = jax.nn.silu(c) @ lp['w_ada'] + lp['b_ada']
    sh1, sc1, g1, sh2, sc2, g2 = [m[:, None, :] for m in jnp.split(mod, N_MOD, axis=-1)]
    h = rms_norm(x, lp['norm1']) * (1.0 + sc1) + sh1
    proj = h @ lp['w_in']
    c1 = D_ATTN
    c2 = c1 + D_KV
    c3 = c2 + D_KV
    c4 = c3 + D_SSM
    c5 = c4 + D_LRU
    q, k, v, u, xr, yg = jnp.split(proj, (c1, c2, c3, c4, c5), axis=-1)
    q = rotary(rms_norm(q.reshape(n, t, N_HEADS, HEAD_DIM), lp['q_norm']), pos)
    k = rotary(rms_norm(k.reshape(n, t, N_KV_HEADS, HEAD_DIM), lp['k_norm']), pos)
    v = v.reshape(n, t, N_KV_HEADS, HEAD_DIM)
    o_attn, k_new, v_new = attend(q, k, v, lp['sinks'])
    o_ssm, ssm_re_new, ssm_im_new = s5_mixer(u, ssm_re, ssm_im, lp)
    o_lru, lru_conv_new, lru_h_new = rglru_mixer(xr, yg, lru_conv, lru_h, pos, lp)
    on = lp['out_norm']
    o = jnp.concatenate([rms_norm(o_attn, on[:D_ATTN]),
                         rms_norm(o_ssm, on[D_ATTN:D_ATTN + D_SSM]),
                         rms_norm(o_lru, on[D_ATTN + D_SSM:])], axis=-1)
    x = x + g1 * (o @ lp['w_o'])
    h2 = rms_norm(x, lp['norm2']) * (1.0 + sc2) + sh2
    f, ffn_conv_new = conv_ffn(h2, ffn_conv, lp)
    x = x + g2 * f
    return x, (k_new, v_new, ssm_re_new, ssm_im_new, lru_h_new, lru_conv_new, ffn_conv_new)


def setup_inputs(seed: int = 0) -> dict:
    key = jax.random.key(seed)
    keys = iter(jax.random.split(key, 64))
    f32 = jnp.float32

    def nrm(shape, scale):
        return scale * jax.random.normal(next(keys), shape, f32)

    def gain(shape):
        return 1.0 + 0.02 * jax.random.normal(next(keys), shape, f32)

    L = DEPTH
    G, P = N_SSM_GROUPS, SSM_STATE
    w_buf = min(WINDOW, PAST_LEN)
    n_idx = jnp.arange(SSM_STATE, dtype=f32)
    a8 = jax.random.uniform(next(keys), (L, D_LRU), f32, 0.9, 0.999)
    a_base = a8 ** (1.0 / LRU_C)
    log_dt = jax.random.uniform(next(keys), (L, G), f32, math.log(1e-3), math.log(1e-1))
    return {
        'x_prompt': nrm((BATCH, SEQ, D_MODEL), 1.0),
        'x_sample': nrm((DEC_BATCH, DEC_SEQ, D_MODEL), 1.0),
        'cache_k': nrm((L, DEC_BATCH, w_buf, N_KV_HEADS, HEAD_DIM), 1.0),
        'cache_v': nrm((L, DEC_BATCH, w_buf, N_KV_HEADS, HEAD_DIM), 1.0),
        'state_ssm_re': nrm((L, DEC_BATCH, G, P), 0.1),
        'state_ssm_im': nrm((L, DEC_BATCH, G, P), 0.1),
        'state_lru_h': nrm((L, DEC_BATCH, D_LRU), 0.5),
        'state_lru_conv': nrm((L, DEC_BATCH, LRU_CONV - 1, D_LRU), 1.0),
        'state_ffn_conv': nrm((L, DEC_BATCH, FFN_CONV - 1, 2 * D_FF), 1.0),
        'c_prompt': nrm((BATCH, D_MODEL), 1.0),
        'c_sample': nrm((DEC_BATCH, D_MODEL), 1.0),
        'w_ada': nrm((L, D_MODEL, N_MOD * D_MODEL), 0.5 * D_MODEL ** -0.5),
        'b_ada': nrm((L, N_MOD * D_MODEL), 0.02),
        'norm1': gain((L, D_MODEL)),
        'w_in': nrm((L, D_MODEL, D_IN), D_MODEL ** -0.5),
        'q_norm': gain((L, HEAD_DIM)),
        'k_norm': gain((L, HEAD_DIM)),
        'sinks': nrm((L, N_HEADS), 0.5),
        'ssm_a_re': -0.5 + nrm((L, G, P), 0.01),
        'ssm_a_im': math.pi * n_idx + nrm((L, G, P), 0.01),
        'ssm_b_re': nrm((L, G, P, SSM_GROUP), (2 * SSM_GROUP) ** -0.5),
        'ssm_b_im': nrm((L, G, P, SSM_GROUP), (2 * SSM_GROUP) ** -0.5),
        'ssm_c_re': nrm((L, G, SSM_GROUP, P), 0.5),
        'ssm_c_im': nrm((L, G, SSM_GROUP, P), 0.5),
        'ssm_d': nrm((L, D_SSM), 1.0),
        'ssm_log_dt': log_dt,
        'ssm_w_glu': nrm((L, D_SSM, D_SSM), D_SSM ** -0.5),
        'ssm_b_glu': nrm((L, D_SSM), 0.02),
        'lru_conv_w': nrm((L, LRU_CONV, D_LRU), LRU_CONV ** -0.5),
        'lru_conv_b': nrm((L, D_LRU), 0.02),
        'lru_w_a': nrm((L, N_LRU_BLOCKS, LRU_BLOCK, LRU_BLOCK), LRU_BLOCK ** -0.5),
        'lru_b_a': nrm((L, D_LRU), 0.02),
        'lru_w_i': nrm((L, N_LRU_BLOCKS, LRU_BLOCK, LRU_BLOCK), LRU_BLOCK ** -0.5),
        'lru_b_i': nrm((L, D_LRU), 0.02),
        'lru_lambda': jnp.log(a_base) - jnp.log1p(-a_base),
        'out_norm': gain((L, D_MIX)),
        'w_o': nrm((L, D_MIX, D_MODEL), D_MIX ** -0.5),
        'norm2': gain((L, D_MODEL)),
        'ffn_w_up': nrm((L, D_MODEL, 2 * D_FF), D_MODEL ** -0.5),
        'ffn_conv_w': nrm((L, FFN_CONV, 2 * D_FF), FFN_CONV ** -0.5),
        'ffn_conv_b': nrm((L, 2 * D_FF), 0.02),
        'ffn_w_down': nrm((L, D_FF, D_MODEL), D_FF ** -0.5),
    }


def reference(x_prompt, x_sample, cache_k, cache_v, state_ssm_re, state_ssm_im, state_lru_h,
              state_lru_conv, state_ffn_conv, c_prompt, c_sample, w_ada, b_ada, norm1, w_in,
              q_norm, k_norm, sinks, ssm_a_re, ssm_a_im, ssm_b_re, ssm_b_im, ssm_c_re, ssm_c_im,
              ssm_d, ssm_log_dt, ssm_w_glu, ssm_b_glu, lru_conv_w, lru_conv_b, lru_w_a, lru_b_a,
              lru_w_i, lru_b_i, lru_lambda, out_norm, w_o, norm2, ffn_w_up, ffn_conv_w,
              ffn_conv_b, ffn_w_down):
    params = dict(w_ada=w_ada, b_ada=b_ada, norm1=norm1, w_in=w_in, q_norm=q_norm, k_norm=k_norm,
                  sinks=sinks, ssm_a_re=ssm_a_re, ssm_a_im=ssm_a_im, ssm_b_re=ssm_b_re,
                  ssm_b_im=ssm_b_im, ssm_c_re=ssm_c_re, ssm_c_im=ssm_c_im, ssm_d=ssm_d,
                  ssm_log_dt=ssm_log_dt, ssm_w_glu=ssm_w_glu, ssm_b_glu=ssm_b_glu,
                  lru_conv_w=lru_conv_w, lru_conv_b=lru_conv_b, lru_w_a=lru_w_a, lru_b_a=lru_b_a,
                  lru_w_i=lru_w_i, lru_b_i=lru_b_i, lru_lambda=lru_lambda, out_norm=out_norm,
                  w_o=w_o, norm2=norm2, ffn_w_up=ffn_w_up, ffn_conv_w=ffn_conv_w,
                  ffn_conv_b=ffn_conv_b, ffn_w_down=ffn_w_down)
    f32 = jnp.float32
    bsz, seq = x_prompt.shape[:2]
    dseq = x_sample.shape[1]
    pos_prompt = jnp.arange(seq, dtype=f32)
    pos_sample = PAST_LEN + jnp.arange(dseq, dtype=f32)
    xp, xs = x_prompt, x_sample
    new_p, new_s = [], []
    for i in range(DEPTH):
        lp = {name: arr[i] for name, arr in params.items()}
        xp, st_p = trunk_layer(
            xp, c_prompt, pos_prompt, lp,
            jnp.zeros((bsz, N_SSM_GROUPS, SSM_STATE), f32),
            jnp.zeros((bsz, N_SSM_GROUPS, SSM_STATE), f32),
            jnp.zeros((bsz, D_LRU), f32),
            jnp.zeros((bsz, LRU_CONV - 1, D_LRU), xp.dtype),
            jnp.zeros((bsz, FFN_CONV - 1, 2 * D_FF), xp.dtype),
            window_attention_prompt)
        xs, st_s = trunk_layer(
            xs, c_sample, pos_sample, lp,
            state_ssm_re[i], state_ssm_im[i], state_lru_h[i], state_lru_conv[i], state_ffn_conv[i],
            functools.partial(window_attention_sample, k_buf=cache_k[i], v_buf=cache_v[i]))
        new_p.append(st_p)
        new_s.append(st_s)
    pk, pv, p_re, p_im, p_lh, p_lc, p_fc = [jnp.stack(s) for s in zip(*new_p)]
    sk, sv, s_re, s_im, s_lh, s_lc, s_fc = [jnp.stack(s) for s in zip(*new_s)]
    return (xp, xs, pk, pv, p_re, p_im, p_lh, p_lc, p_fc, sk, sv, s_re, s_im, s_lh, s_lc, s_fc)
```

```python
import functools

import jax
import jax.numpy as jnp
from jax import lax
from jax.experimental import pallas as pl
from jax.experimental.pallas import tpu as pltpu

F32 = jnp.float32
BF16 = jnp.bfloat16

D_MODEL = 1024
HEAD_DIM = 64
N_HEADS = 8
N_KV_HEADS = 2
KV_GROUP = N_HEADS // N_KV_HEADS
D_ATTN = N_HEADS * HEAD_DIM
D_KV = N_KV_HEADS * HEAD_DIM
WINDOW = 128
ROPE_THETA = 10000.0
PAST_LEN = 8192
D_SSM = 256
SSM_GROUP = 16
N_SSM_GROUPS = 16
SSM_STATE = 64
D_LRU = 256
N_LRU_BLOCKS = 4
LRU_BLOCK = 64
LRU_CONV = 4
LRU_C = 8.0
D_FF = 2816
FFN_CONV = 3
D_IN = D_ATTN + 2 * D_KV + D_SSM + 2 * D_LRU
N_MOD = 6
EPS = 1e-6

SUBLANES = 8
LANES = 128
VMEM_LIMIT_BYTES = 56 * 1024 * 1024

TOKEN_TILE = 512
ATTN_BLOCK = WINDOW
SSM_CHUNK = 32
LRU_TILE = 512
FF_CHUNK = 256
N_FF_CHUNKS = D_FF // FF_CHUNK
ADA_COLS = 1536
SAMPLE_ATTN_SEQS = 16


def _cparams(*sem):
    return pltpu.CompilerParams(dimension_semantics=sem, vmem_limit_bytes=VMEM_LIMIT_BYTES)


def _dot(a, b):
    return jnp.dot(a, b, preferred_element_type=F32)


def _split_bf16(x):
    hi = x.astype(BF16)
    lo = (x - hi.astype(F32)).astype(BF16)
    return hi, lo


def _ada_kernel(c_ref, w_ref, b_ref, o_ref):
    c = c_ref[...]
    s = jax.nn.silu(c).astype(BF16)
    o_ref[...] = _dot(s, w_ref[...].astype(BF16)) + b_ref[...]


def _ada_call(c_all, w_ada, b_ada):
    depth = w_ada.shape[0]
    rows = c_all.shape[0]
    ncol = N_MOD * D_MODEL // ADA_COLS
    return pl.pallas_call(
        _ada_kernel,
        grid=(depth, ncol),
        in_specs=[
            pl.BlockSpec((rows, D_MODEL), lambda l, j: (0, 0)),
            pl.BlockSpec((None, D_MODEL, ADA_COLS), lambda l, j: (l, 0, j)),
            pl.BlockSpec((None, 1, ADA_COLS), lambda l, j: (l, 0, j)),
        ],
        out_specs=pl.BlockSpec((None, rows, ADA_COLS), lambda l, j: (l, 0, j)),
        out_shape=jax.ShapeDtypeStruct((depth, rows, N_MOD * D_MODEL), F32),
        compiler_params=_cparams("arbitrary", "arbitrary"),
        name="ada",
    )(c_all, w_ada, b_ada.reshape(depth, 1, N_MOD * D_MODEL))


def _rope_kernel(pos_ref, cos_ref, sin_ref):
    pos = pos_ref[...]
    lane = lax.broadcasted_iota(jnp.int32, pos.shape, 1)
    half = HEAD_DIM // 2
    j = (lane & (half - 1)).astype(F32)
    inv = ROPE_THETA ** (-j / half)
    ang = pos * inv
    cos_ref[...] = jnp.cos(ang)
    s = jnp.sin(ang)
    sin_ref[...] = jnp.where((lane & (HEAD_DIM - 1)) < half, -s, s)


def _rope_call(pos_rows):
    t = pos_rows.shape[0]
    pos_b = jnp.broadcast_to(pos_rows[:, None], (t, LANES))
    return pl.pallas_call(
        _rope_kernel,
        out_shape=(jax.ShapeDtypeStruct((t, LANES), F32), jax.ShapeDtypeStruct((t, LANES), F32)),
        name="rope",
    )(pos_b)


def _mod_row(ref, per_token):
    return ref[...] if per_token else ref[0:1, :]


def _rms(x, gain):
    return x * lax.rsqrt(jnp.mean(x * x, axis=-1, keepdims=True) + EPS) * gain


def _head_rms(t, seg, gain):
    hi, lo = _split_bf16(t * t)
    ss = _dot(hi, seg) + _dot(lo, seg)
    return t * lax.rsqrt(ss * (1.0 / HEAD_DIM) + EPS) * gain


def _rope(t, cos, sin):
    width = t.shape[1]
    reps = width // LANES
    if reps > 1:
        cos = jnp.concatenate([cos] * reps, axis=1)
        sin = jnp.concatenate([sin] * reps, axis=1)
    lane = lax.broadcasted_iota(jnp.int32, t.shape, 1)
    half = HEAD_DIM // 2
    up = pltpu.roll(t, width - half, axis=1)
    dn = pltpu.roll(t, half, axis=1)
    rot = jnp.where((lane & (HEAD_DIM - 1)) < half, up, dn)
    return t * cos + rot * sin


def _pre_kernel(x_ref, sh_ref, sc_ref, n1_ref, w_ref, qn_ref, kn_ref, seg_ref, cos_ref, sin_ref,
                q_ref, k_ref, v_ref, u_ref, xr_ref, yg_ref, *, per_token):
    x = x_ref[...]
    h = _rms(x, n1_ref[...]) * (1.0 + _mod_row(sc_ref, per_token)) + _mod_row(sh_ref, per_token)
    proj = _dot(h.astype(BF16), w_ref[...])
    c1 = D_ATTN
    c2 = c1 + D_KV
    c3 = c2 + D_KV
    c4 = c3 + D_SSM
    c5 = c4 + D_LRU
    cos = cos_ref[...]
    sin = sin_ref[...]
    seg = seg_ref[...]
    q = _rope(_head_rms(proj[:, :c1], seg, qn_ref[...]), cos, sin)
    k = _rope(_head_rms(proj[:, c1:c2], seg[:D_KV, :D_KV], kn_ref[...]), cos, sin)
    q_ref[...] = q.astype(BF16)
    k_ref[...] = k
    v_ref[...] = proj[:, c2:c3]
    u_ref[...] = proj[:, c3:c4]
    xr_ref[...] = proj[:, c4:c5]
    yg_ref[...] = proj[:, c5:]


def _mod_spec(piece, per_token, tiles_per_seq, tm):
    if per_token:
        return pl.BlockSpec((None, tm, D_MODEL), lambda i: (piece, i, 0))
    return pl.BlockSpec((None, None, SUBLANES, D_MODEL), lambda i: (piece, i // tiles_per_seq, 0, 0))


def _const_spec(shape, single=False):
    nd = len(shape)
    if single:
        return pl.BlockSpec(shape, lambda *_: (0,) * nd, pipeline_mode=pl.Buffered(1))
    return pl.BlockSpec(shape, lambda *_: (0,) * nd)


def _pre_call(x, mod, lw, cos, sin, *, per_token, seq_len):
    nt = x.shape[0]
    tm = min(TOKEN_TILE, nt)
    tiles_per_seq = seq_len // tm
    if per_token:
        tab_spec = pl.BlockSpec((tm, LANES), lambda i: (i, 0))
    else:
        tab_spec = pl.BlockSpec((tm, LANES), lambda i: (i % tiles_per_seq, 0))
    row = lambda w: pl.BlockSpec((tm, w), lambda i: (i, 0))
    widths = (D_ATTN, D_KV, D_KV, D_SSM, D_LRU, D_LRU)
    dtypes = (BF16, F32, F32, F32, F32, F32)
    return pl.pallas_call(
        functools.partial(_pre_kernel, per_token=per_token),
        grid=(nt // tm,),
        in_specs=[
            row(D_MODEL),
            _mod_spec(0, per_token, tiles_per_seq, tm),
            _mod_spec(1, per_token, tiles_per_seq, tm),
            _const_spec((1, D_MODEL)),
            _const_spec((D_MODEL, D_IN)),
            _const_spec((1, D_ATTN)),
            _const_spec((1, D_KV)),
            _const_spec((D_ATTN, D_ATTN)),
            tab_spec,
            tab_spec,
        ],
        out_specs=[row(w) for w in widths],
        out_shape=[jax.ShapeDtypeStruct((nt, w), d) for w, d in zip(widths, dtypes)],
        compiler_params=_cparams("arbitrary"),
        name="pre",
    )(x, mod, mod, lw["norm1"], lw["w_in"], lw["q_norm"], lw["k_norm"], lw["seg"], cos, sin)


def _sink_column(sink_ref, kv_head, rows_per_head):
    rows = KV_GROUP * rows_per_head
    r = lax.broadcasted_iota(jnp.int32, (rows, 1), 0)
    col = jnp.full((rows, 1), sink_ref[kv_head * KV_GROUP], F32)
    for g in range(1, KV_GROUP):
        col = jnp.where(r >= g * rows_per_head, sink_ref[kv_head * KV_GROUP + g], col)
    return col


def _attn_prompt_kernel(sink_ref, q_ref, kc_ref, kp_ref, vc_ref, vp_ref, o_ref):
    i = pl.program_id(1)
    bq = ATTN_BLOCK
    q = q_ref[...]
    kcat = jnp.concatenate([kp_ref[...], kc_ref[...]], axis=0).astype(BF16)
    vcat = jnp.concatenate([vp_ref[...], vc_ref[...]], axis=0).astype(BF16)
    qi = lax.broadcasted_iota(jnp.int32, (bq, 2 * bq), 0)
    si = lax.broadcasted_iota(jnp.int32, (bq, 2 * bq), 1)
    diff = qi + bq - si
    allowed = (diff >= 0) & (diff < WINDOW) & ((si >= bq) | (i > 0))
    allowed = jnp.concatenate([allowed] * KV_GROUP, axis=0)
    outs = []
    for h in range(N_KV_HEADS):
        kh = kcat[:, h * HEAD_DIM:(h + 1) * HEAD_DIM]
        vh = vcat[:, h * HEAD_DIM:(h + 1) * HEAD_DIM]
        qh = jnp.concatenate(
            [q[:, (h * KV_GROUP + g) * HEAD_DIM:(h * KV_GROUP + g + 1) * HEAD_DIM] for g in range(KV_GROUP)],
            axis=0)
        s = lax.dot_general(qh, kh, (((1,), (1,)), ((), ())), preferred_element_type=F32)
        s = s * (HEAD_DIM ** -0.5)
        s = jnp.where(allowed, s, -jnp.inf)
        sink = _sink_column(sink_ref, h, bq)
        m = jnp.maximum(jnp.max(s, axis=-1, keepdims=True), sink)
        p = jnp.exp(s - m)
        denom = jnp.sum(p, axis=-1, keepdims=True) + jnp.exp(sink - m)
        o = _dot(p.astype(BF16), vh) / denom
        outs += [o[g * bq:(g + 1) * bq] for g in range(KV_GROUP)]
    o_ref[...] = jnp.concatenate(outs, axis=1)


def _attn_prompt_call(sinks, q, k, v, *, bsz, seq_len):
    nb = seq_len // ATTN_BLOCK
    cur = lambda w: pl.BlockSpec((ATTN_BLOCK, w), lambda b, i: (b * nb + i, 0))
    prev = lambda w: pl.BlockSpec((ATTN_BLOCK, w), lambda b, i: (b * nb + jnp.maximum(i - 1, 0), 0))
    return pl.pallas_call(
        _attn_prompt_kernel,
        grid=(bsz, nb),
        in_specs=[pl.BlockSpec(memory_space=pltpu.SMEM), cur(D_ATTN), cur(D_KV), prev(D_KV), cur(D_KV),
                  prev(D_KV)],
        out_specs=cur(D_ATTN),
        out_shape=jax.ShapeDtypeStruct((bsz * seq_len, D_ATTN), F32),
        compiler_params=_cparams("arbitrary", "arbitrary"),
        name="attn_prompt",
    )(sinks, q, k, k, v, v)


def _attn_sample_kernel(sink_ref, q_ref, kn_ref, vn_ref, ck_ref, cv_ref, o_ref, sk_ref, sv_ref, *, dseq):
    kk = jnp.concatenate([ck_ref[...], kn_ref[...]], axis=1)
    vv = jnp.concatenate([cv_ref[...], vn_ref[...]], axis=1)
    wbuf = ck_ref.shape[1]
    sk_ref[...] = kk[:, dseq:, :]
    sv_ref[...] = vv[:, dseq:, :]
    nq = dseq * KV_GROUP
    nk = wbuf + dseq
    qrow = lax.broadcasted_iota(jnp.int32, (nq, nk), 0)
    j = lax.broadcasted_iota(jnp.int32, (nq, nk), 1)
    diff = (qrow >> 2) + wbuf - j
    allowed = ((diff >= 0) & (diff < WINDOW))[None]
    g_of_row = lax.broadcasted_iota(jnp.int32, (nq, 1), 0) & (KV_GROUP - 1)
    kkb = kk.astype(BF16)
    vvb = vv.astype(BF16)
    for h in range(N_KV_HEADS):
        kh = kkb[:, :, h * HEAD_DIM:(h + 1) * HEAD_DIM]
        vh = vvb[:, :, h * HEAD_DIM:(h + 1) * HEAD_DIM]
        qh = q_ref[:, h]
        s = jnp.einsum("sqd,skd->sqk", qh, kh, preferred_element_type=F32) * (HEAD_DIM ** -0.5)
        s = jnp.where(allowed, s, -jnp.inf)
        sink = jnp.full((nq, 1), sink_ref[h * KV_GROUP], F32)
        for g in range(1, KV_GROUP):
            sink = jnp.where(g_of_row == g, sink_ref[h * KV_GROUP + g], sink)
        sink = sink[None]
        m = jnp.maximum(jnp.max(s, axis=-1, keepdims=True), sink)
        p = jnp.exp(s - m)
        denom = jnp.sum(p, axis=-1, keepdims=True) + jnp.exp(sink - m)
        o = jnp.einsum("sqk,skd->sqd", p.astype(BF16), vh, preferred_element_type=F32) / denom
        o_ref[:, h] = o


def _attn_sample_call(sinks, q, kn, vn, ck, cv, *, dseq):
    n, wbuf = ck.shape[0], ck.shape[1]
    sb = min(SAMPLE_ATTN_SEQS, n)
    nq = dseq * KV_GROUP
    qspec = pl.BlockSpec((sb, N_KV_HEADS, nq, HEAD_DIM), lambda i: (i, 0, 0, 0))
    nspec = pl.BlockSpec((sb, dseq, D_KV), lambda i: (i, 0, 0))
    cspec = pl.BlockSpec((sb, wbuf, D_KV), lambda i: (i, 0, 0))
    return pl.pallas_call(
        functools.partial(_attn_sample_kernel, dseq=dseq),
        grid=(n // sb,),
        in_specs=[pl.BlockSpec(memory_space=pltpu.SMEM), qspec, nspec, nspec, cspec, cspec],
        out_specs=[qspec, cspec, cspec],
        out_shape=[jax.ShapeDtypeStruct((n, N_KV_HEADS, nq, HEAD_DIM), F32),
                   jax.ShapeDtypeStruct((n, wbuf, D_KV), F32),
                   jax.ShapeDtypeStruct((n, wbuf, D_KV), F32)],
        compiler_params=_cparams("arbitrary"),
        name="attn_sample",
    )(sinks, q, kn, vn, ck, cv)


def _bdot3(a, b):
    dn = (((2,), (2,)), ((0,), (0,)))
    ah, al = _split_bf16(a)
    bh, bl = _split_bf16(b)
    d = lambda x, y: lax.dot_general(x, y, dn, preferred_element_type=F32)
    return d(ah, bh) + d(ah, bl) + d(al, bh)


def _ssm_prep_kernel(are_ref, aim_ref, ldt_ref, bre_ref, bim_ref, cre_ref, cim_ref,
                     nbr_ref, nbi_ref, car_ref, cai_ref, kl_ref, apw_ref, *, lc, powers):
    a_re = are_ref[...]
    a_im = aim_ref[...]
    dt = jnp.exp(ldt_ref[...])
    zr = a_re * dt
    zi = a_im * dt

    def a_pow(j):
        mag = jnp.exp(zr * j)
        return mag * jnp.cos(zi * j), mag * jnp.sin(zi * j)

    abr, abi = a_pow(1.0)
    xr = abr - 1.0
    den = a_re * a_re + a_im * a_im
    coef_r = (xr * a_re + abi * a_im) / den
    coef_i = (abi * a_re - xr * a_im) / den
    btr = bre_ref[...]
    bti = bim_ref[...]
    bbr = coef_r * btr - coef_i * bti
    bbi = coef_r * bti + coef_i * btr
    cre = cre_ref[...]
    cim = cim_ref[...]
    for j in range(lc + 1):
        er, ei = a_pow(float(j))
        car = cre * er - cim * ei
        cai = cre * ei + cim * er
        car_ref[j] = car
        cai_ref[j] = -cai
        if j < lc:
            nbr_ref[j] = er * bbr - ei * bbi
            nbi_ref[j] = er * bbi + ei * bbr
            kl_ref[j] = _bdot3(car, bbr) - _bdot3(cai, bbi)
    for idx, pw in enumerate(powers):
        er, ei = a_pow(float(pw))
        apw_ref[idx, 0] = jnp.concatenate([er, er], axis=-1)
        apw_ref[idx, 1] = jnp.concatenate([-ei, ei], axis=-1)
        apw_ref[idx, 2] = jnp.concatenate([ei, -ei], axis=-1)


def _ssm_prep_call(a_re, a_im, log_dt, b_re, b_im, c_re, c_im, *, lc, powers):
    depth = a_re.shape[0]
    g, p, c = N_SSM_GROUPS, SSM_STATE, SSM_GROUP
    npw = len(powers)
    a_spec = pl.BlockSpec((None, g, 1, p), lambda l: (l, 0, 0, 0))
    m_spec = pl.BlockSpec((None, g, c, p), lambda l: (l, 0, 0, 0))
    out5 = lambda n, last: pl.BlockSpec((None, n, g, c, last), lambda l: (l, 0, 0, 0, 0))
    return pl.pallas_call(
        functools.partial(_ssm_prep_kernel, lc=lc, powers=powers),
        grid=(depth,),
        in_specs=[a_spec, a_spec, a_spec, m_spec, m_spec, m_spec, m_spec],
        out_specs=[out5(lc, p), out5(lc, p), out5(lc + 1, p), out5(lc + 1, p), out5(lc, c),
                   pl.BlockSpec((None, npw, 3, g, 1, 2 * p), lambda l: (l, 0, 0, 0, 0, 0))],
        out_shape=[jax.ShapeDtypeStruct((depth, lc, g, c, p), F32),
                   jax.ShapeDtypeStruct((depth, lc, g, c, p), F32),
                   jax.ShapeDtypeStruct((depth, lc + 1, g, c, p), F32),
                   jax.ShapeDtypeStruct((depth, lc + 1, g, c, p), F32),
                   jax.ShapeDtypeStruct((depth, lc, g, c, c), F32),
                   jax.ShapeDtypeStruct((depth, npw, 3, g, 1, 2 * p), F32)],
        compiler_params=_cparams("arbitrary"),
        name="ssm_prep",
    )(a_re.reshape(depth, g, 1, p), a_im.reshape(depth, g, 1, p),
      jnp.broadcast_to(log_dt[:, :, None, None], (depth, g, 1, p)),
      jnp.swapaxes(b_re, -1, -2), jnp.swapaxes(b_im, -1, -2), c_re, c_im)


def _ssm_weights(nb_r, nb_i, ca_r, ca_ni, kl, lc):
    g, c, p = N_SSM_GROUPS, SSM_GROUP, SSM_STATE
    kt = jnp.transpose(kl[:lc], (1, 0, 3, 2))
    s_idx = jnp.arange(lc)[:, None]
    t_idx = jnp.arange(lc)[None, :]
    lag = t_idx - s_idx
    toe = kt[:, jnp.clip(lag, 0, lc - 1)]
    toe = jnp.where((lag >= 0)[None, :, :, None, None], toe, 0.0)
    toe = jnp.transpose(toe, (0, 1, 3, 2, 4)).reshape(g, lc * c, lc * c)
    flip = lambda a: jnp.transpose(a[:lc][::-1], (1, 0, 2, 3)).reshape(g, lc * c, p)
    nr, ni = flip(nb_r), flip(nb_i)
    ncat = jnp.concatenate([nr, ni, ni, nr], axis=-1)
    tr = lambda a: jnp.transpose(a[1:lc + 1], (1, 3, 0, 2)).reshape(g, p, lc * c)
    mcat = jnp.concatenate([tr(ca_r), tr(ca_ni)], axis=1)
    return toe.astype(BF16), ncat.astype(BF16), mcat.astype(BF16)


def _shift_rows(x, d):
    rows = lax.broadcasted_iota(jnp.int32, x.shape, 0)
    return jnp.where(rows >= d, pltpu.roll(x, d, axis=0), 0.0)


def _ssm_kernel(*refs, nsteps, rows_per_chunk, has_h0):
    if has_h0:
        u_ref, t_ref, n_ref, m_ref, apw_ref, h0_ref, h0s_ref, y_ref, hend_ref = refs
    else:
        u_ref, t_ref, n_ref, m_ref, apw_ref, y_ref, hend_ref = refs
    uf = u_ref[...]
    y = _dot(uf, t_ref[...])
    s = _dot(uf, n_ref[...])
    w = 2 * SSM_STATE
    h = s[:, :w]
    hs = s[:, w:]
    for i in range(nsteps):
        d = (1 << i) * rows_per_chunk
        a1, a2, a3 = apw_ref[i, 0], apw_ref[i, 1], apw_ref[i, 2]
        hd = _shift_rows(h, d)
        hsd = _shift_rows(hs, d)
        h, hs = h + a1 * hd + a2 * hsd, hs + a1 * hsd + a3 * hd
    if has_h0:
        h0 = h0_ref[...]
        hin = h0
        h = h + apw_ref[0, 0] * h0 + apw_ref[0, 1] * h0s_ref[...]
    else:
        hin = _shift_rows(h, rows_per_chunk)
    y_ref[...] = y + _dot(hin.astype(BF16), m_ref[...])
    rows = h.shape[0]
    hend_ref[...] = h[rows - rows_per_chunk:, :]


def _ssm_call(uf, toe, ncat, mcat, apw, h0=None, h0s=None, *, rows_per_chunk):
    g, rows, w = uf.shape
    nsteps = apw.shape[0] if h0 is None else 0
    gspec = lambda a, b: pl.BlockSpec((None, a, b), lambda i: (i, 0, 0))
    in_specs = [gspec(rows, w), gspec(w, w), gspec(w, 4 * SSM_STATE), gspec(2 * SSM_STATE, w),
                pl.BlockSpec((apw.shape[0], 3, None, 1, 2 * SSM_STATE), lambda i: (0, 0, i, 0, 0))]
    args = [uf, toe, ncat, mcat, apw]
    if h0 is not None:
        in_specs += [gspec(rows, 2 * SSM_STATE)] * 2
        args += [h0, h0s]
    return pl.pallas_call(
        functools.partial(_ssm_kernel, nsteps=nsteps, rows_per_chunk=rows_per_chunk, has_h0=h0 is not None),
        grid=(g,),
        in_specs=in_specs,
        out_specs=[gspec(rows, w), gspec(rows_per_chunk, 2 * SSM_STATE)],
        out_shape=[jax.ShapeDtypeStruct((g, rows, w), F32),
                   jax.ShapeDtypeStruct((g, rows_per_chunk, 2 * SSM_STATE), F32)],
        compiler_params=_cparams("arbitrary"),
        name="ssm",
    )(*args)


def _softplus(z):
    return jnp.maximum(z, 0.0) + jnp.log1p(jnp.exp(-jnp.abs(z)))


def _lru_gates(xc, wg_ref, bg_ref, lam_ref):
    gl = _dot(xc.astype(BF16), wg_ref[...]) + bg_ref[...]
    r = jax.nn.sigmoid(gl[:, :D_LRU])
    gi = jax.nn.sigmoid(gl[:, D_LRU:])
    log_a = -LRU_C * r * _softplus(-lam_ref[...])
    a = jnp.exp(log_a)
    mult = jnp.sqrt(1.0 - a * a)
    return a, mult, gi


def _lru_prompt_kernel(xr_ref, yg_ref, cw_ref, cb_ref, wg_ref, bg_ref, lam_ref, o_ref, hl_ref,
                       xp_scr, hc_scr):
    t = pl.program_id(1)
    tl = xr_ref.shape[0]
    halo = SUBLANES

    @pl.when(t == 0)
    def _():
        xp_scr[0:halo, :] = jnp.zeros((halo, D_LRU), F32)
        hc_scr[...] = jnp.zeros((1, D_LRU), F32)

    xp_scr[halo:halo + tl, :] = xr_ref[...]
    xc = cb_ref[...]
    for j in range(LRU_CONV):
        off = halo - (LRU_CONV - 1) + j
        xc = xc + cw_ref[j:j + 1, :] * xp_scr[off:off + tl, :]
    a, mult, gi = _lru_gates(xc, wg_ref, bg_ref, lam_ref)
    row = lax.broadcasted_iota(jnp.int32, (tl, D_LRU), 0)
    mult = jnp.where((row == 0) & (t == 0), 1.0, mult)
    b = mult * gi * xc
    d = 1
    while d < tl:
        keep = row >= d
        a_sh = jnp.where(keep, pltpu.roll(a, d, axis=0), 1.0)
        b_sh = jnp.where(keep, pltpu.roll(b, d, axis=0), 0.0)
        b = a * b_sh + b
        a = a * a_sh
        d *= 2
    h = a * hc_scr[...] + b
    o_ref[...] = h * jax.nn.gelu(yg_ref[...])
    last = h[tl - 1:tl, :]
    hc_scr[...] = last
    hl_ref[...] = jnp.broadcast_to(last, (SUBLANES, D_LRU))
    xp_scr[0:halo, :] = xp_scr[tl:tl + halo, :]


def _lru_prompt_call(xr, yg, lw, *, bsz, seq_len):
    tl = min(LRU_TILE, seq_len)
    nt = seq_len // tl
    row = pl.BlockSpec((tl, D_LRU), lambda b, t: (b * nt + t, 0))
    return pl.pallas_call(
        _lru_prompt_kernel,
        grid=(bsz, nt),
        in_specs=[row, row, _const_spec((LRU_CONV, D_LRU)), _const_spec((1, D_LRU)),
                  _const_spec((D_LRU, 2 * D_LRU)), _const_spec((1, 2 * D_LRU)), _const_spec((1, D_LRU))],
        out_specs=[row, pl.BlockSpec((None, SUBLANES, D_LRU), lambda b, t: (b, 0, 0))],
        out_shape=[jax.ShapeDtypeStruct((bsz * seq_len, D_LRU), F32),
                   jax.ShapeDtypeStruct((bsz, SUBLANES, D_LRU), F32)],
        scratch_shapes=[pltpu.VMEM((tl + 2 * SUBLANES, D_LRU), F32), pltpu.VMEM((1, D_LRU), F32)],
        compiler_params=_cparams("arbitrary", "arbitrary"),
        name="lru_prompt",
    )(xr, yg, lw["lru_conv_w"], lw["lru_conv_b"], lw["lru_wg"], lw["lru_bg"], lw["lru_lambda"])


def _lru_sample_kernel(xr_ref, yg_ref, buf_ref, h0_ref, cw_ref, cb_ref, wg_ref, bg_ref, lam_ref,
                       o_ref, hl_ref, *, dseq):
    n = h0_ref.shape[0]
    xp = [buf_ref[j] for j in range(LRU_CONV - 1)] + [xr_ref[pl.ds(t * n, n), :] for t in range(dseq)]
    xcs = []
    for t in range(dseq):
        xc = cb_ref[...]
        for j in range(LRU_CONV):
            xc = xc + cw_ref[j:j + 1, :] * xp[t + j]
        xcs.append(xc)
    xc = jnp.concatenate(xcs, axis=0)
    a, mult, gi = _lru_gates(xc, wg_ref, bg_ref, lam_ref)
    b = mult * gi * xc
    h = h0_ref[...]
    for t in range(dseq):
        h = a[t * n:(t + 1) * n] * h + b[t * n:(t + 1) * n]
        o_ref[pl.ds(t * n, n), :] = h * jax.nn.gelu(yg_ref[pl.ds(t * n, n), :])
    hl_ref[...] = h


def _lru_sample_call(xr, yg, buf_tm, h0, lw, *, dseq):
    n = h0.shape[0]
    return pl.pallas_call(
        functools.partial(_lru_sample_kernel, dseq=dseq),
        out_shape=[jax.ShapeDtypeStruct((dseq * n, D_LRU), F32), jax.ShapeDtypeStruct((n, D_LRU), F32)],
        compiler_params=pltpu.CompilerParams(vmem_limit_bytes=VMEM_LIMIT_BYTES),
        name="lru_sample",
    )(xr, yg, buf_tm, h0, lw["lru_conv_w"], lw["lru_conv_b"], lw["lru_wg"], lw["lru_bg"], lw["lru_lambda"])


def _post_kernel(x_ref, oa_ref, ys_ref, u_ref, ol_ref, g1_ref, sh2_ref, sc2_ref, g2_ref,
                 d_ref, wglu_ref, bglu_ref, on_ref, wo_ref, n2_ref, wup_ref, cw_ref, cb_ref, wdn_ref,
                 halo_in_ref, xo_ref, halo_out_ref, h2_scr, acc_scr, xp_scr, halo_scr,
                 *, per_token, row_shift):
    t = pl.program_id(1)
    tm = x_ref.shape[0]
    hrows = halo_in_ref.shape[2]

    @pl.when(t == 0)
    def _():
        halo_scr[...] = halo_in_ref[...]

    ys = ys_ref[...] + d_ref[...] * u_ref[...]
    gs = jax.nn.gelu(ys)
    o_ssm = gs * jax.nn.sigmoid(_dot(gs.astype(BF16), wglu_ref[...]) + bglu_ref[...])
    on = on_ref[...]
    c1 = D_ATTN
    c2 = c1 + D_SSM
    o = jnp.concatenate([_rms(oa_ref[...], on[:, :c1]), _rms(o_ssm, on[:, c1:c2]),
                         _rms(ol_ref[...], on[:, c2:])], axis=-1)
    x1 = x_ref[...] + _mod_row(g1_ref, per_token) * _dot(o.astype(BF16), wo_ref[...])
    h2 = _rms(x1, n2_ref[...]) * (1.0 + _mod_row(sc2_ref, per_token)) + _mod_row(sh2_ref, per_token)
    h2_scr[...] = h2.astype(BF16)
    acc_scr[...] = jnp.zeros_like(acc_scr)
    for c in range(N_FF_CHUNKS):
        halves = []
        for gv in range(2):
            up = _dot(h2_scr[...], wup_ref[gv, c])
            xp_scr[gv, 0:hrows, :] = halo_scr[gv, c]
            xp_scr[gv, hrows:hrows + tm, :] = up
            y = cb_ref[gv, c]
            for j in range(FFN_CONV):
                off = hrows - (FFN_CONV - 1 - j) * row_shift
                y = y + cw_ref[gv, c, j:j + 1, :] * xp_scr[gv, off:off + tm, :]
            halo_scr[gv, c] = xp_scr[gv, tm:tm + hrows, :]
            halves.append(y)
        act = jax.nn.gelu(halves[0]) * halves[1]
        acc_scr[...] += _dot(act.astype(BF16), wdn_ref[c])
    xo_ref[...] = x1 + _mod_row(g2_ref, per_token) * acc_scr[...]
    halo_out_ref[...] = halo_scr[...]


def _post_call(x, oa, ys, u, ol, mod, lw, halo_in, *, per_token, seq_len, row_shift):
    nt = x.shape[0]
    tm = min(TOKEN_TILE, nt)
    tiles_per_seq = seq_len // tm
    nseq = nt // seq_len
    hrows = halo_in.shape[3]
    row = lambda w: pl.BlockSpec((tm, w), lambda s, t: (s * tiles_per_seq + t, 0))

    def mspec(piece):
        if per_token:
            return pl.BlockSpec((None, tm, D_MODEL), lambda s, t: (piece, s * tiles_per_seq + t, 0))
        return pl.BlockSpec((None, None, SUBLANES, D_MODEL), lambda s, t: (piece, s, 0, 0))

    halo_spec = pl.BlockSpec((None, 2, N_FF_CHUNKS, hrows, FF_CHUNK), lambda s, t: (s, 0, 0, 0, 0))
    return pl.pallas_call(
        functools.partial(_post_kernel, per_token=per_token, row_shift=row_shift),
        grid=(nseq, tiles_per_seq),
        in_specs=[
            row(D_MODEL), row(D_ATTN), row(D_SSM), row(D_SSM), row(D_LRU),
            mspec(2), mspec(3), mspec(4), mspec(5),
            _const_spec((1, D_SSM)), _const_spec((D_SSM, D_SSM)), _const_spec((1, D_SSM)),
            _const_spec((1, D_MODEL)), _const_spec((D_MODEL, D_MODEL), True), _const_spec((1, D_MODEL)),
            _const_spec((2, N_FF_CHUNKS, D_MODEL, FF_CHUNK), True),
            _const_spec((2, N_FF_CHUNKS, FFN_CONV, FF_CHUNK)),
            _const_spec((2, N_FF_CHUNKS, 1, FF_CHUNK)),
            _const_spec((N_FF_CHUNKS, FF_CHUNK, D_MODEL), True),
            halo_spec,
        ],
        out_specs=[row(D_MODEL), halo_spec],
        out_shape=[jax.ShapeDtypeStruct((nt, D_MODEL), F32),
                   jax.ShapeDtypeStruct((nseq, 2, N_FF_CHUNKS, hrows, FF_CHUNK), F32)],
        scratch_shapes=[pltpu.VMEM((tm, D_MODEL), BF16), pltpu.VMEM((tm, D_MODEL), F32),
                        pltpu.VMEM((2, tm + hrows, FF_CHUNK), F32),
                        pltpu.VMEM((2, N_FF_CHUNKS, hrows, FF_CHUNK), F32)],
        compiler_params=_cparams("arbitrary", "arbitrary"),
        name="post",
    )(x, oa, ys, u, ol, mod, mod, mod, mod, lw["ssm_d"], lw["ssm_w_glu"], lw["ssm_b_glu"], lw["out_norm"],
      lw["w_o"], lw["norm2"], lw["w_up"], lw["ffn_conv_w"], lw["ffn_conv_b"], lw["w_down"], halo_in)


def _block_diag(w):
    depth, nb, bs, _ = w.shape
    eye = jnp.eye(nb, dtype=w.dtype)
    return jnp.einsum("lhij,hk->lhikj", w, eye).reshape(depth, nb * bs, nb * bs)


def _prepare_weights(p):
    depth = p["w_in"].shape[0]
    row = lambda a: a.reshape(depth, 1, -1)
    seg = (jnp.arange(D_ATTN)[:, None] // HEAD_DIM == jnp.arange(D_ATTN)[None, :] // HEAD_DIM).astype(BF16)
    w_up = p["ffn_w_up"].astype(BF16).reshape(depth, D_MODEL, 2, N_FF_CHUNKS, FF_CHUNK)
    cw = p["ffn_conv_w"].reshape(depth, FFN_CONV, 2, N_FF_CHUNKS, FF_CHUNK)
    return dict(
        norm1=row(p["norm1"]), norm2=row(p["norm2"]), out_norm=row(p["out_norm"]),
        w_in=p["w_in"].astype(BF16), w_o=p["w_o"].astype(BF16),
        q_norm=row(jnp.tile(p["q_norm"], (1, N_HEADS))), k_norm=row(jnp.tile(p["k_norm"], (1, N_KV_HEADS))),
        seg=jnp.broadcast_to(seg, (depth, D_ATTN, D_ATTN)),
        sinks=p["sinks"],
        ssm_d=row(p["ssm_d"]), ssm_w_glu=p["ssm_w_glu"].astype(BF16), ssm_b_glu=row(p["ssm_b_glu"]),
        lru_conv_w=p["lru_conv_w"], lru_conv_b=row(p["lru_conv_b"]),
        lru_wg=jnp.concatenate([_block_diag(p["lru_w_a"]), _block_diag(p["lru_w_i"])], axis=-1).astype(BF16),
        lru_bg=row(jnp.concatenate([p["lru_b_a"], p["lru_b_i"]], axis=-1)),
        lru_lambda=row(p["lru_lambda"]),
        w_up=jnp.transpose(w_up, (0, 2, 3, 1, 4)),
        ffn_conv_w=jnp.transpose(cw, (0, 2, 3, 1, 4)),
        ffn_conv_b=p["ffn_conv_b"].reshape(depth, 2, N_FF_CHUNKS, 1, FF_CHUNK),
        w_down=p["ffn_w_down"].astype(BF16).reshape(depth, N_FF_CHUNKS, FF_CHUNK, D_MODEL),
    )


def _ff_state_from_halo(halo, nrows):
    ns = halo.shape[0]
    last = halo[:, :, :, halo.shape[3] - nrows:, :]
    return jnp.transpose(last, (0, 3, 1, 2, 4)).reshape(ns, nrows, 2 * D_FF)


def _prompt_layer(x, mod, lw, ssm_w, cos, sin, *, bsz, seq_len):
    g, lc = N_SSM_GROUPS, SSM_CHUNK
    nchunk = seq_len // lc
    q, k, v, u, xr, yg = _pre_call(x, mod, lw, cos, sin, per_token=False, seq_len=seq_len)
    oa = _attn_prompt_call(lw["sinks"], q, k, v, bsz=bsz, seq_len=seq_len)
    toe, ncat, mcat, apw = ssm_w
    uf = jnp.transpose(u.reshape(bsz, nchunk, lc, g, SSM_GROUP), (3, 1, 0, 2, 4))
    uf = uf.reshape(g, nchunk * bsz, lc * SSM_GROUP).astype(BF16)
    yf, hend = _ssm_call(uf, toe, ncat, mcat, apw, rows_per_chunk=bsz)
    ys = jnp.transpose(yf.reshape(g, nchunk, bsz, lc, SSM_GROUP), (2, 1, 3, 0, 4)).reshape(bsz * seq_len, D_SSM)
    ol, hl = _lru_prompt_call(xr, yg, lw, bsz=bsz, seq_len=seq_len)
    halo0 = jnp.zeros((bsz, 2, N_FF_CHUNKS, SUBLANES, FF_CHUNK), F32)
    x_new, halo = _post_call(x, oa, ys, u, ol, mod, lw, halo0, per_token=False, seq_len=seq_len, row_shift=1)
    keep = min(WINDOW, seq_len)
    k3 = k.reshape(bsz, seq_len, N_KV_HEADS, HEAD_DIM)
    v3 = v.reshape(bsz, seq_len, N_KV_HEADS, HEAD_DIM)
    hend_b = jnp.transpose(hend, (1, 0, 2))
    states = (k3[:, seq_len - keep:], v3[:, seq_len - keep:],
              hend_b[..., :SSM_STATE], hend_b[..., SSM_STATE:],
              hl[:, 0, :],
              xr.reshape(bsz, seq_len, D_LRU)[:, seq_len - (LRU_CONV - 1):],
              _ff_state_from_halo(halo, FFN_CONV - 1))
    return x_new, states


def _sample_layer(x, mod, lw, ssm_w, cos, sin, st, *, n, dseq):
    g = N_SSM_GROUPS
    ck, cv, s_re, s_im, lru_h, lru_conv, ffn_conv = st
    nt = n * dseq
    q, k, v, u, xr, yg = _pre_call(x, mod, lw, cos, sin, per_token=True, seq_len=nt)
    q_sm = jnp.transpose(q.reshape(dseq, n, N_KV_HEADS, KV_GROUP, HEAD_DIM), (1, 2, 0, 3, 4))
    q_sm = q_sm.reshape(n, N_KV_HEADS, dseq * KV_GROUP, HEAD_DIM)
    kn = jnp.transpose(k.reshape(dseq, n, D_KV), (1, 0, 2))
    vn = jnp.transpose(v.reshape(dseq, n, D_KV), (1, 0, 2))
    wbuf = ck.shape[1]
    o_sm, sk, sv = _attn_sample_call(lw["sinks"], q_sm, kn, vn, ck.reshape(n, wbuf, D_KV),
                                     cv.reshape(n, wbuf, D_KV), dseq=dseq)
    oa = jnp.transpose(o_sm.reshape(n, N_KV_HEADS, dseq, KV_GROUP, HEAD_DIM), (2, 0, 1, 3, 4)).reshape(nt, D_ATTN)
    toe, ncat, mcat, apw = ssm_w
    uf = jnp.transpose(u.reshape(dseq, n, g, SSM_GROUP), (2, 1, 0, 3)).reshape(g, n, dseq * SSM_GROUP).astype(BF16)
    h_re = jnp.transpose(s_re, (1, 0, 2))
    h_im = jnp.transpose(s_im, (1, 0, 2))
    h0 = jnp.concatenate([h_re, h_im], axis=-1)
    h0s = jnp.concatenate([h_im, h_re], axis=-1)
    yf, hend = _ssm_call(uf, toe, ncat, mcat, apw, h0, h0s, rows_per_chunk=n)
    ys = jnp.transpose(yf.reshape(g, n, dseq, SSM_GROUP), (2, 1, 0, 3)).reshape(nt, D_SSM)
    ol, hl = _lru_sample_call(xr, yg, jnp.transpose(lru_conv, (1, 0, 2)), lru_h, lw, dseq=dseq)
    nconv = FFN_CONV - 1
    halo0 = jnp.transpose(ffn_conv.reshape(n, nconv, 2, N_FF_CHUNKS, FF_CHUNK), (2, 3, 1, 0, 4))
    halo0 = halo0.reshape(1, 2, N_FF_CHUNKS, nconv * n, FF_CHUNK)
    x_new, halo = _post_call(x, oa, ys, u, ol, mod, lw, halo0, per_token=True, seq_len=nt, row_shift=n)
    hend_b = jnp.transpose(hend, (1, 0, 2))
    new_fc = jnp.transpose(halo.reshape(2, N_FF_CHUNKS, nconv, n, FF_CHUNK), (3, 2, 0, 1, 4)).reshape(n, nconv, 2 * D_FF)
    xr_tm = xr.reshape(dseq, n, D_LRU)
    lru_conv_all = jnp.concatenate([jnp.transpose(lru_conv, (1, 0, 2)), xr_tm], axis=0)
    states = (sk.reshape(n, wbuf, N_KV_HEADS, HEAD_DIM), sv.reshape(n, wbuf, N_KV_HEADS, HEAD_DIM),
              hend_b[..., :SSM_STATE], hend_b[..., SSM_STATE:],
              hl,
              jnp.transpose(lru_conv_all[dseq:], (1, 0, 2)),
              new_fc)
    return x_new, states


def kernel(x_prompt, x_sample, cache_k, cache_v, state_ssm_re, state_ssm_im, state_lru_h, state_lru_conv,
           state_ffn_conv, c_prompt, c_sample, w_ada, b_ada, norm1, w_in, q_norm, k_norm, sinks, ssm_a_re,
           ssm_a_im, ssm_b_re, ssm_b_im, ssm_c_re, ssm_c_im, ssm_d, ssm_log_dt, ssm_w_glu, ssm_b_glu,
           lru_conv_w, lru_conv_b, lru_w_a, lru_b_a, lru_w_i, lru_b_i, lru_lambda, out_norm, w_o, norm2,
           ffn_w_up, ffn_conv_w, ffn_conv_b, ffn_w_down):
    bsz, seq_len = x_prompt.shape[:2]
    n, dseq = x_sample.shape[:2]
    depth = w_in.shape[0]
    params = dict(norm1=norm1, w_in=w_in, q_norm=q_norm, k_norm=k_norm, sinks=sinks, ssm_d=ssm_d,
                  ssm_w_glu=ssm_w_glu, ssm_b_glu=ssm_b_glu, lru_conv_w=lru_conv_w, lru_conv_b=lru_conv_b,
                  lru_w_a=lru_w_a, lru_b_a=lru_b_a, lru_w_i=lru_w_i, lru_b_i=lru_b_i, lru_lambda=lru_lambda,
                  out_norm=out_norm, w_o=w_o, norm2=norm2, ffn_w_up=ffn_w_up, ffn_conv_w=ffn_conv_w,
                  ffn_conv_b=ffn_conv_b, ffn_w_down=ffn_w_down)
    weights = _prepare_weights(params)

    rows = bsz + n
    pad = (-rows) % SUBLANES
    c_all = jnp.concatenate([c_prompt, c_sample, jnp.zeros((pad, D_MODEL), F32)], axis=0)
    mod_all = _ada_call(c_all, w_ada, b_ada).reshape(depth, rows + pad, N_MOD, D_MODEL)
    mod_p = jnp.transpose(mod_all[:, :bsz], (0, 2, 1, 3))
    mod_p = jnp.broadcast_to(mod_p[:, :, :, None, :], (depth, N_MOD, bsz, SUBLANES, D_MODEL))
    mod_s = jnp.transpose(mod_all[:, bsz:rows], (0, 2, 1, 3))
    mod_s = jnp.tile(mod_s, (1, 1, dseq, 1))

    cos_p, sin_p = _rope_call(jnp.arange(seq_len, dtype=F32))
    cos_s, sin_s = _rope_call(jnp.repeat(PAST_LEN + jnp.arange(dseq, dtype=F32), n))

    nchunk = seq_len // SSM_CHUNK
    nsteps = max(nchunk - 1, 0).bit_length()
    powers = (dseq,) + tuple(SSM_CHUNK * (1 << i) for i in range(nsteps))
    nb_r, nb_i, ca_r, ca_ni, kl, apw = _ssm_prep_call(ssm_a_re, ssm_a_im, ssm_log_dt, ssm_b_re, ssm_b_im,
                                                      ssm_c_re, ssm_c_im, lc=SSM_CHUNK, powers=powers)

    xp = x_prompt.reshape(bsz * seq_len, D_MODEL)
    xs = jnp.transpose(x_sample, (1, 0, 2)).reshape(dseq * n, D_MODEL)
    new_p, new_s = [], []
    for i in range(depth):
        lw = {name: arr[i] for name, arr in weights.items()}
        tabs = (nb_r[i], nb_i[i], ca_r[i], ca_ni[i], kl[i])
        ssm_p = _ssm_weights(*tabs, SSM_CHUNK) + (apw[i, 1:],)
        ssm_s = _ssm_weights(*tabs, dseq) + (apw[i, 0:1],)
        xp, st_p = _prompt_layer(xp, mod_p[i], lw, ssm_p, cos_p, sin_p, bsz=bsz, seq_len=seq_len)
        st_in = (cache_k[i], cache_v[i], state_ssm_re[i], state_ssm_im[i], state_lru_h[i], state_lru_conv[i],
                 state_ffn_conv[i])
        xs, st_s = _sample_layer(xs, mod_s[i], lw, ssm_s, cos_s, sin_s, st_in, n=n, dseq=dseq)
        new_p.append(st_p)
        new_s.append(st_s)
    pk, pv, p_re, p_im, p_lh, p_lc, p_fc = [jnp.stack(s) for s in zip(*new_p)]
    sk, sv, s_re, s_im, s_lh, s_lc, s_fc = [jnp.stack(s) for s in zip(*new_s)]
    y_p = xp.reshape(bsz, seq_len, D_MODEL)
    y_s = jnp.transpose(xs.reshape(dseq, n, D_MODEL), (1, 0, 2))
    return (y_p, y_s, pk, pv, p_re, p_im, p_lh, p_lc, p_fc, sk, sv, s_re, s_im, s_lh, s_lc, s_fc)
```

```python
import functools

import jax
import jax.numpy as jnp
from jax import lax
from jax.experimental import pallas as pl
from jax.experimental.pallas import tpu as pltpu

F32 = jnp.float32
BF16 = jnp.bfloat16

D_MODEL = 1024
HEAD_DIM = 64
N_HEADS = 8
N_KV_HEADS = 2
KV_GROUP = N_HEADS // N_KV_HEADS
D_ATTN = N_HEADS * HEAD_DIM
D_KV = N_KV_HEADS * HEAD_DIM
WINDOW = 128
ROPE_THETA = 10000.0
PAST_LEN = 8192
D_SSM = 256
SSM_GROUP = 16
N_SSM_GROUPS = 16
SSM_STATE = 64
D_LRU = 256
N_LRU_BLOCKS = 4
LRU_BLOCK = 64
LRU_CONV = 4
LRU_C = 8.0
D_FF = 2816
FFN_CONV = 3
D_IN = D_ATTN + 2 * D_KV + D_SSM + 2 * D_LRU
N_MOD = 6
EPS = 1e-6

SUBLANES = 8
LANES = 128
VMEM_LIMIT_BYTES = 56 * 1024 * 1024

TOKEN_TILE = 512
ATTN_BLOCK = WINDOW
SSM_CHUNK = 32
LRU_TILE = 512
FF_CHUNK = 256
N_FF_CHUNKS = D_FF // FF_CHUNK
ADA_COLS = 1536
SAMPLE_ATTN_SEQS = 16


def _cparams(*sem):
    return pltpu.CompilerParams(dimension_semantics=sem, vmem_limit_bytes=VMEM_LIMIT_BYTES)


def _dot(a, b):
    return jnp.dot(a, b, preferred_element_type=F32)


def _split_bf16(x):
    hi = x.astype(BF16)
    lo = (x - hi.astype(F32)).astype(BF16)
    return hi, lo


def _ada_kernel(c_ref, w_ref, b_ref, o_ref):
    c = c_ref[...]
    s = jax.nn.silu(c).astype(BF16)
    o_ref[...] = _dot(s, w_ref[...].astype(BF16)) + b_ref[...]


def _ada_call(c_all, w_ada, b_ada):
    depth = w_ada.shape[0]
    rows = c_all.shape[0]
    ncol = N_MOD * D_MODEL // ADA_COLS
    return pl.pallas_call(
        _ada_kernel,
        grid=(depth, ncol),
        in_specs=[
            pl.BlockSpec((rows, D_MODEL), lambda l, j: (0, 0)),
            pl.BlockSpec((None, D_MODEL, ADA_COLS), lambda l, j: (l, 0, j)),
            pl.BlockSpec((None, 1, ADA_COLS), lambda l, j: (l, 0, j)),
        ],
        out_specs=pl.BlockSpec((None, rows, ADA_COLS), lambda l, j: (l, 0, j)),
        out_shape=jax.ShapeDtypeStruct((depth, rows, N_MOD * D_MODEL), F32),
        compiler_params=_cparams("arbitrary", "arbitrary"),
        name="ada",
    )(c_all, w_ada, b_ada.reshape(depth, 1, N_MOD * D_MODEL))


def _rope_kernel(pos_ref, cos_ref, sin_ref):
    pos = pos_ref[...]
    lane = lax.broadcasted_iota(jnp.int32, pos.shape, 1)
    half = HEAD_DIM // 2
    j = (lane & (half - 1)).astype(F32)
    inv = ROPE_THETA ** (-j / half)
    ang = pos * inv
    cos_ref[...] = jnp.cos(ang)
    s = jnp.sin(ang)
    sin_ref[...] = jnp.where((lane & (HEAD_DIM - 1)) < half, -s, s)


def _rope_call(pos_rows):
    t = pos_rows.shape[0]
    pos_b = jnp.broadcast_to(pos_rows[:, None], (t, LANES))
    return pl.pallas_call(
        _rope_kernel,
        out_shape=(jax.ShapeDtypeStruct((t, LANES), F32), jax.ShapeDtypeStruct((t, LANES), F32)),
        name="rope",
    )(pos_b)


def _mod_row(ref, per_token):
    return ref[...] if per_token else ref[0:1, :]


def _rms(x, gain):
    return x * lax.rsqrt(jnp.mean(x * x, axis=-1, keepdims=True) + EPS) * gain


def _head_rms(t, seg, gain):
    hi, lo = _split_bf16(t * t)
    ss = _dot(hi, seg) + _dot(lo, seg)
    return t * lax.rsqrt(ss * (1.0 / HEAD_DIM) + EPS) * gain


def _rope(t, cos, sin):
    width = t.shape[1]
    reps = width // LANES
    if reps > 1:
        cos = jnp.concatenate([cos] * reps, axis=1)
        sin = jnp.concatenate([sin] * reps, axis=1)
    lane = lax.broadcasted_iota(jnp.int32, t.shape, 1)
    half = HEAD_DIM // 2
    up = pltpu.roll(t, width - half, axis=1)
    dn = pltpu.roll(t, half, axis=1)
    rot = jnp.where((lane & (HEAD_DIM - 1)) < half, up, dn)
    return t * cos + rot * sin


def _unit_transpose8(vs):
    lane = lax.broadcasted_iota(jnp.int32, vs[0].shape, 1)
    unit = lane >> 4
    for b in range(3):
        d = 1 << b
        bit = (unit >> b) & 1
        new = list(vs)
        for i in range(8):
            if (i >> b) & 1 == 0:
                lo, hi = vs[i], vs[i + d]
                new[i] = jnp.where(bit == 0, lo, pltpu.roll(hi, d * SSM_GROUP, axis=1))
                new[i + d] = jnp.where(bit == 1, hi, pltpu.roll(lo, LANES - d * SSM_GROUP, axis=1))
        vs = new
    return vs


SSM_HALVES = D_SSM // LANES
GROUPS_PER_HALF = LANES // SSM_GROUP


def _to_group_major(u_ref, nk, lc):
    outs = [[None] * (lc // 8) for _ in range(N_SSM_GROUPS)]
    for h in range(SSM_HALVES):
        for tb in range(lc // 8):
            vs = [u_ref[h, pl.ds(tb * 8 + tp, nk, stride=lc), :] for tp in range(8)]
            ws = _unit_transpose8(vs)
            for gp in range(GROUPS_PER_HALF):
                outs[h * GROUPS_PER_HALF + gp][tb] = ws[gp]
    return [jnp.concatenate(o, axis=1) for o in outs]


def _from_group_major(yf_ref, ys_ref, nk, lc):
    for h in range(SSM_HALVES):
        for tb in range(lc // 8):
            ws = [yf_ref[h * GROUPS_PER_HALF + gp, :, tb * LANES:(tb + 1) * LANES] for gp in range(GROUPS_PER_HALF)]
            vs = _unit_transpose8(ws)
            for tp in range(8):
                ys_ref[h, pl.ds(tb * 8 + tp, nk, stride=lc), :] = vs[tp]


def _pre_kernel(x_ref, sh_ref, sc_ref, n1_ref, w_ref, qn_ref, kn_ref, seg_ref, cos_ref, sin_ref,
                q_ref, k_ref, v_ref, u_ref, xr_ref, yg_ref, *maybe_uf_ref, per_token):
    x = x_ref[...]
    h = _rms(x, n1_ref[...]) * (1.0 + _mod_row(sc_ref, per_token)) + _mod_row(sh_ref, per_token)
    proj = _dot(h.astype(BF16), w_ref[...])
    c1 = D_ATTN
    c2 = c1 + D_KV
    c3 = c2 + D_KV
    c4 = c3 + D_SSM
    c5 = c4 + D_LRU
    cos = cos_ref[...]
    sin = sin_ref[...]
    seg = seg_ref[...]
    q = _rope(_head_rms(proj[:, :c1], seg, qn_ref[...]), cos, sin)
    k = _rope(_head_rms(proj[:, c1:c2], seg[:D_KV, :D_KV], kn_ref[...]), cos, sin)
    q_ref[...] = q.astype(BF16)
    k_ref[...] = k
    v_ref[...] = proj[:, c2:c3]
    u_ref[...] = proj[:, c3:c4]
    xr_ref[...] = proj[:, c4:c5]
    yg_ref[...] = proj[:, c5:]
    if maybe_uf_ref:
        uf_ref, u_scr = maybe_uf_ref
        for h in range(SSM_HALVES):
            u_scr[h] = proj[:, c3 + h * LANES:c3 + (h + 1) * LANES]
        groups = _to_group_major(u_scr, uf_ref.shape[1], SSM_CHUNK)
        for g in range(N_SSM_GROUPS):
            uf_ref[g] = groups[g].astype(BF16)


def _mod_spec(piece, per_token, tiles_per_seq, tm):
    if per_token:
        return pl.BlockSpec((None, tm, D_MODEL), lambda i: (piece, i, 0))
    return pl.BlockSpec((None, None, SUBLANES, D_MODEL), lambda i: (piece, i // tiles_per_seq, 0, 0))


def _const_spec(shape, single=False):
    nd = len(shape)
    if single:
        return pl.BlockSpec(shape, lambda *_: (0,) * nd, pipeline_mode=pl.Buffered(1))
    return pl.BlockSpec(shape, lambda *_: (0,) * nd)


def _pre_call(x, mod, lw, cos, sin, *, per_token, seq_len):
    nt = x.shape[0]
    tm = min(TOKEN_TILE, nt)
    tiles_per_seq = seq_len // tm
    if per_token:
        tab_spec = pl.BlockSpec((tm, LANES), lambda i: (i, 0))
    else:
        tab_spec = pl.BlockSpec((tm, LANES), lambda i: (i % tiles_per_seq, 0))
    row = lambda w: pl.BlockSpec((tm, w), lambda i: (i, 0))
    widths = (D_ATTN, D_KV, D_KV, D_SSM, D_LRU, D_LRU)
    dtypes = (BF16, F32, F32, F32, F32, F32)
    out_specs = [row(w) for w in widths]
    out_shape = [jax.ShapeDtypeStruct((nt, w), d) for w, d in zip(widths, dtypes)]
    scratch = []
    if not per_token:
        nk = tm // SSM_CHUNK
        wf = SSM_CHUNK * SSM_GROUP
        out_specs.append(pl.BlockSpec((N_SSM_GROUPS, nk, wf), lambda i: (0, i, 0)))
        out_shape.append(jax.ShapeDtypeStruct((N_SSM_GROUPS, nt // SSM_CHUNK, wf), BF16))
        scratch.append(pltpu.VMEM((SSM_HALVES, tm, LANES), F32))
    return pl.pallas_call(
        functools.partial(_pre_kernel, per_token=per_token),
        grid=(nt // tm,),
        in_specs=[
            row(D_MODEL),
            _mod_spec(0, per_token, tiles_per_seq, tm),
            _mod_spec(1, per_token, tiles_per_seq, tm),
            _const_spec((1, D_MODEL)),
            _const_spec((D_MODEL, D_IN)),
            _const_spec((1, D_ATTN)),
            _const_spec((1, D_KV)),
            _const_spec((D_ATTN, D_ATTN)),
            tab_spec,
            tab_spec,
        ],
        out_specs=out_specs,
        out_shape=out_shape,
        scratch_shapes=scratch,
        compiler_params=_cparams("arbitrary"),
        name="pre",
    )(x, mod, mod, lw["norm1"], lw["w_in"], lw["q_norm"], lw["k_norm"], lw["seg"], cos, sin)


def _sink_column(sink_ref, kv_head, rows_per_head):
    rows = KV_GROUP * rows_per_head
    r = lax.broadcasted_iota(jnp.int32, (rows, 1), 0)
    col = jnp.full((rows, 1), sink_ref[kv_head * KV_GROUP], F32)
    for g in range(1, KV_GROUP):
        col = jnp.where(r >= g * rows_per_head, sink_ref[kv_head * KV_GROUP + g], col)
    return col


def _attn_prompt_kernel(sink_ref, q_ref, kc_ref, kp_ref, vc_ref, vp_ref, o_ref):
    i = pl.program_id(1)
    bq = ATTN_BLOCK
    q = q_ref[...]
    kcat = jnp.concatenate([kp_ref[...], kc_ref[...]], axis=0).astype(BF16)
    vcat = jnp.concatenate([vp_ref[...], vc_ref[...]], axis=0).astype(BF16)
    qi = lax.broadcasted_iota(jnp.int32, (bq, 2 * bq), 0)
    si = lax.broadcasted_iota(jnp.int32, (bq, 2 * bq), 1)
    diff = qi + bq - si
    allowed = (diff >= 0) & (diff < WINDOW) & ((si >= bq) | (i > 0))
    allowed = jnp.concatenate([allowed] * KV_GROUP, axis=0)
    outs = []
    for h in range(N_KV_HEADS):
        kh = kcat[:, h * HEAD_DIM:(h + 1) * HEAD_DIM]
        vh = vcat[:, h * HEAD_DIM:(h + 1) * HEAD_DIM]
        qh = jnp.concatenate(
            [q[:, (h * KV_GROUP + g) * HEAD_DIM:(h * KV_GROUP + g + 1) * HEAD_DIM] for g in range(KV_GROUP)],
            axis=0)
        s = lax.dot_general(qh, kh, (((1,), (1,)), ((), ())), preferred_element_type=F32)
        s = s * (HEAD_DIM ** -0.5)
        s = jnp.where(allowed, s, -jnp.inf)
        sink = _sink_column(sink_ref, h, bq)
        m = jnp.maximum(jnp.max(s, axis=-1, keepdims=True), sink)
        p = jnp.exp(s - m)
        denom = jnp.sum(p, axis=-1, keepdims=True) + jnp.exp(sink - m)
        o = _dot(p.astype(BF16), vh) / denom
        outs += [o[g * bq:(g + 1) * bq] for g in range(KV_GROUP)]
    o_ref[...] = jnp.concatenate(outs, axis=1)


def _attn_prompt_call(sinks, q, k, v, *, bsz, seq_len):
    nb = seq_len // ATTN_BLOCK
    cur = lambda w: pl.BlockSpec((ATTN_BLOCK, w), lambda b, i: (b * nb + i, 0))
    prev = lambda w: pl.BlockSpec((ATTN_BLOCK, w), lambda b, i: (b * nb + jnp.maximum(i - 1, 0), 0))
    return pl.pallas_call(
        _attn_prompt_kernel,
        grid=(bsz, nb),
        in_specs=[pl.BlockSpec(memory_space=pltpu.SMEM), cur(D_ATTN), cur(D_KV), prev(D_KV), cur(D_KV),
                  prev(D_KV)],
        out_specs=cur(D_ATTN),
        out_shape=jax.ShapeDtypeStruct((bsz * seq_len, D_ATTN), F32),
        compiler_params=_cparams("arbitrary", "arbitrary"),
        name="attn_prompt",
    )(sinks, q, k, k, v, v)


def _attn_sample_kernel(sink_ref, q_ref, kn_ref, vn_ref, ck_ref, cv_ref, o_ref, sk_ref, sv_ref, *, dseq):
    kk = jnp.concatenate([ck_ref[...], kn_ref[...]], axis=1)
    vv = jnp.concatenate([cv_ref[...], vn_ref[...]], axis=1)
    wbuf = ck_ref.shape[1]
    sk_ref[...] = kk[:, dseq:, :]
    sv_ref[...] = vv[:, dseq:, :]
    nq = dseq * KV_GROUP
    nk = wbuf + dseq
    qrow = lax.broadcasted_iota(jnp.int32, (nq, nk), 0)
    j = lax.broadcasted_iota(jnp.int32, (nq, nk), 1)
    diff = (qrow >> 2) + wbuf - j
    allowed = ((diff >= 0) & (diff < WINDOW))[None]
    g_of_row = lax.broadcasted_iota(jnp.int32, (nq, 1), 0) & (KV_GROUP - 1)
    kkb = kk.astype(BF16)
    vvb = vv.astype(BF16)
    for h in range(N_KV_HEADS):
        kh = kkb[:, :, h * HEAD_DIM:(h + 1) * HEAD_DIM]
        vh = vvb[:, :, h * HEAD_DIM:(h + 1) * HEAD_DIM]
        qh = q_ref[:, h]
        s = jnp.einsum("sqd,skd->sqk", qh, kh, preferred_element_type=F32) * (HEAD_DIM ** -0.5)
        s = jnp.where(allowed, s, -jnp.inf)
        sink = jnp.full((nq, 1), sink_ref[h * KV_GROUP], F32)
        for g in range(1, KV_GROUP):
            sink = jnp.where(g_of_row == g, sink_ref[h * KV_GROUP + g], sink)
        sink = sink[None]
        m = jnp.maximum(jnp.max(s, axis=-1, keepdims=True), sink)
        p = jnp.exp(s - m)
        denom = jnp.sum(p, axis=-1, keepdims=True) + jnp.exp(sink - m)
        o = jnp.einsum("sqk,skd->sqd", p.astype(BF16), vh, preferred_element_type=F32) / denom
        o_ref[:, h] = o


def _attn_sample_call(sinks, q, kn, vn, ck, cv, *, dseq):
    n, wbuf = ck.shape[0], ck.shape[1]
    sb = min(SAMPLE_ATTN_SEQS, n)
    nq = dseq * KV_GROUP
    qspec = pl.BlockSpec((sb, N_KV_HEADS, nq, HEAD_DIM), lambda i: (i, 0, 0, 0))
    nspec = pl.BlockSpec((sb, dseq, D_KV), lambda i: (i, 0, 0))
    cspec = pl.BlockSpec((sb, wbuf, D_KV), lambda i: (i, 0, 0))
    return pl.pallas_call(
        functools.partial(_attn_sample_kernel, dseq=dseq),
        grid=(n // sb,),
        in_specs=[pl.BlockSpec(memory_space=pltpu.SMEM), qspec, nspec, nspec, cspec, cspec],
        out_specs=[qspec, cspec, cspec],
        out_shape=[jax.ShapeDtypeStruct((n, N_KV_HEADS, nq, HEAD_DIM), F32),
                   jax.ShapeDtypeStruct((n, wbuf, D_KV), F32),
                   jax.ShapeDtypeStruct((n, wbuf, D_KV), F32)],
        compiler_params=_cparams("arbitrary"),
        name="attn_sample",
    )(sinks, q, kn, vn, ck, cv)


def _bdot3(a, b):
    dn = (((2,), (2,)), ((0,), (0,)))
    ah, al = _split_bf16(a)
    bh, bl = _split_bf16(b)
    d = lambda x, y: lax.dot_general(x, y, dn, preferred_element_type=F32)
    return d(ah, bh) + d(ah, bl) + d(al, bh)


def _ssm_prep_kernel(are_ref, aim_ref, ldt_ref, bre_ref, bim_ref, cre_ref, cim_ref,
                     r_ref, np_ref, mtp_ref, ns_ref, mts_ref, apw_ref, *, lc, dseq, powers):
    c = SSM_GROUP
    a_re = are_ref[...]
    a_im = aim_ref[...]
    dt = jnp.exp(ldt_ref[...])
    zr = a_re * dt
    zi = a_im * dt

    def a_pow(j):
        mag = jnp.exp(zr * j)
        return mag * jnp.cos(zi * j), mag * jnp.sin(zi * j)

    abr, abi = a_pow(1.0)
    xr = abr - 1.0
    den = a_re * a_re + a_im * a_im
    coef_r = (xr * a_re + abi * a_im) / den
    coef_i = (abi * a_re - xr * a_im) / den
    btr = bre_ref[...]
    bti = bim_ref[...]
    bbr = coef_r * btr - coef_i * bti
    bbi = coef_r * bti + coef_i * btr
    cre = cre_ref[...]
    cim = cim_ref[...]
    ns_ref[...] = jnp.zeros_like(ns_ref)
    mts_ref[...] = jnp.zeros_like(mts_ref)
    cars, cais = [], []
    for j in range(lc + 1):
        er, ei = a_pow(float(j))
        car = cre * er - cim * ei
        cai = cre * ei + cim * er
        if j < lc:
            cars.append(car)
            cais.append(cai)
            nbr = er * bbr - ei * bbi
            nbi = er * bbi + ei * bbr
            ncat = jnp.concatenate([nbr, nbi, nbi, nbr], axis=-1).astype(BF16)
            s = lc - 1 - j
            np_ref[:, s * c:(s + 1) * c, :] = ncat
            if j < dseq:
                s = dseq - 1 - j
                ns_ref[:, s * c:(s + 1) * c, :] = ncat
        if j >= 1:
            mcat = jnp.concatenate([car, -cai], axis=-1).astype(BF16)
            mtp_ref[:, (j - 1) * c:j * c, :] = mcat
            if j <= dseq:
                mts_ref[:, (j - 1) * c:j * c, :] = mcat
    ca_r = jnp.concatenate(cars, axis=1)
    ca_i = jnp.concatenate(cais, axis=1)
    r_ref[...] = _bdot3(bbr, ca_r) - _bdot3(bbi, ca_i)
    for idx, pw in enumerate(powers):
        er, ei = a_pow(float(pw))
        apw_ref[idx, 0] = jnp.concatenate([er, er], axis=-1)
        apw_ref[idx, 1] = jnp.concatenate([-ei, ei], axis=-1)
        apw_ref[idx, 2] = jnp.concatenate([ei, -ei], axis=-1)


def _ssm_prep_call(a_re, a_im, log_dt, b_re, b_im, c_re, c_im, *, lc, dseq, powers):
    depth = a_re.shape[0]
    g, p, c = N_SSM_GROUPS, SSM_STATE, SSM_GROUP
    npw = len(powers)
    a_spec = pl.BlockSpec((None, g, 1, p), lambda l: (l, 0, 0, 0))
    m_spec = pl.BlockSpec((None, g, c, p), lambda l: (l, 0, 0, 0))
    out4 = lambda a, b: pl.BlockSpec((None, g, a, b), lambda l: (l, 0, 0, 0))
    shape4 = lambda a, b, d: jax.ShapeDtypeStruct((depth, g, a, b), d)
    ws = LANES
    return pl.pallas_call(
        functools.partial(_ssm_prep_kernel, lc=lc, dseq=dseq, powers=powers),
        grid=(depth,),
        in_specs=[a_spec, a_spec, a_spec, m_spec, m_spec, m_spec, m_spec],
        out_specs=[out4(c, lc * c), out4(lc * c, 4 * p), out4(lc * c, 2 * p), out4(ws, 4 * p), out4(ws, 2 * p),
                   pl.BlockSpec((None, npw, 3, g, 1, 2 * p), lambda l: (l, 0, 0, 0, 0, 0))],
        out_shape=[shape4(c, lc * c, F32), shape4(lc * c, 4 * p, BF16), shape4(lc * c, 2 * p, BF16),
                   shape4(ws, 4 * p, BF16), shape4(ws, 2 * p, BF16),
                   jax.ShapeDtypeStruct((depth, npw, 3, g, 1, 2 * p), F32)],
        compiler_params=_cparams("arbitrary"),
        name="ssm_prep",
    )(a_re.reshape(depth, g, 1, p), a_im.reshape(depth, g, 1, p),
      jnp.broadcast_to(log_dt[:, :, None, None], (depth, g, 1, p)),
      jnp.swapaxes(b_re, -1, -2), jnp.swapaxes(b_im, -1, -2), c_re, c_im)


def _ssm_kernel(*refs, nsteps, nchunk, nseq, has_h0):
    if has_h0:
        u_ref, r_ref, n_ref, mt_ref, apw_ref, h0_ref, h0s_ref, y_ref, hend_ref, toe_scr = refs
    else:
        u_ref, r_ref, n_ref, mt_ref, apw_ref, y_ref, hend_ref, toe_scr = refs
    w = u_ref.shape[1]
    c = SSM_GROUP
    r = r_ref[...]
    lane = lax.broadcasted_iota(jnp.int32, r.shape, 1)
    toe_scr[0:c, :] = r.astype(BF16)
    for s in range(1, w // c):
        toe_scr[s * c:(s + 1) * c, :] = jnp.where(lane >= s * c, pltpu.roll(r, s * c, axis=1), 0.0).astype(BF16)
    uf = u_ref[...]
    y = _dot(uf, toe_scr[...])
    st = _dot(uf, n_ref[...])
    ws = 2 * SSM_STATE
    h = st[:, :ws]
    hs = st[:, ws:]
    kidx = lax.broadcasted_iota(jnp.int32, h.shape, 0) & (nchunk - 1)
    shift = lambda x, d: jnp.where(kidx >= d, pltpu.roll(x, d, axis=0), 0.0)
    for i in range(nsteps):
        d = 1 << i
        a1, a2, a3 = apw_ref[i, 0], apw_ref[i, 1], apw_ref[i, 2]
        hd = shift(h, d)
        hsd = shift(hs, d)
        h, hs = h + a1 * hd + a2 * hsd, hs + a1 * hsd + a3 * hd
    if has_h0:
        h0 = h0_ref[...]
        hin = h0
        h = h + apw_ref[0, 0] * h0 + apw_ref[0, 1] * h0s_ref[...]
    else:
        hin = shift(h, 1)
    y_ref[...] = y + lax.dot_general(hin.astype(BF16), mt_ref[...], (((1,), (1,)), ((), ())),
                                     preferred_element_type=F32)
    if nchunk == 1:
        hend_ref[...] = h
    else:
        hend_ref[...] = jnp.concatenate([h[(b + 1) * nchunk - 1:(b + 1) * nchunk, :] for b in range(nseq)], axis=0)


def _ssm_call(uf, r, nmat, mt, apw, h0=None, h0s=None, *, nchunk):
    g, rows, w = uf.shape
    assert nchunk & (nchunk - 1) == 0, "chunks per sequence must be a power of two"
    nseq = rows // nchunk
    nsteps = apw.shape[0] if h0 is None else 0
    gspec = lambda a, b: pl.BlockSpec((None, a, b), lambda i: (i, 0, 0))
    in_specs = [gspec(rows, w), gspec(SSM_GROUP, w), gspec(w, 4 * SSM_STATE), gspec(w, 2 * SSM_STATE),
                pl.BlockSpec((apw.shape[0], 3, None, 1, 2 * SSM_STATE), lambda i: (0, 0, i, 0, 0))]
    args = [uf, r, nmat, mt, apw]
    if h0 is not None:
        in_specs += [gspec(rows, 2 * SSM_STATE)] * 2
        args += [h0, h0s]
    return pl.pallas_call(
        functools.partial(_ssm_kernel, nsteps=nsteps, nchunk=nchunk, nseq=nseq, has_h0=h0 is not None),
        grid=(g,),
        in_specs=in_specs,
        out_specs=[gspec(rows, w), gspec(nseq, 2 * SSM_STATE)],
        out_shape=[jax.ShapeDtypeStruct((g, rows, w), F32),
                   jax.ShapeDtypeStruct((g, nseq, 2 * SSM_STATE), F32)],
        scratch_shapes=[pltpu.VMEM((w, w), BF16)],
        compiler_params=_cparams("arbitrary"),
        name="ssm",
    )(*args)


def _softplus(z):
    return jnp.maximum(z, 0.0) + jnp.log1p(jnp.exp(-jnp.abs(z)))


def _lru_gates(xc, wg_ref, bg_ref, lam_ref):
    gl = _dot(xc.astype(BF16), wg_ref[...]) + bg_ref[...]
    r = jax.nn.sigmoid(gl[:, :D_LRU])
    gi = jax.nn.sigmoid(gl[:, D_LRU:])
    log_a = -LRU_C * r * _softplus(-lam_ref[...])
    a = jnp.exp(log_a)
    mult = jnp.sqrt(1.0 - a * a)
    return a, mult, gi


def _lru_prompt_kernel(xr_ref, yg_ref, cw_ref, cb_ref, wg_ref, bg_ref, lam_ref, o_ref, hl_ref,
                       xp_scr, hc_scr):
    t = pl.program_id(1)
    tl = xr_ref.shape[0]
    halo = SUBLANES

    @pl.when(t == 0)
    def _():
        xp_scr[0:halo, :] = jnp.zeros((halo, D_LRU), F32)
        hc_scr[...] = jnp.zeros((1, D_LRU), F32)

    xp_scr[halo:halo + tl, :] = xr_ref[...]
    xc = cb_ref[...]
    for j in range(LRU_CONV):
        off = halo - (LRU_CONV - 1) + j
        xc = xc + cw_ref[j:j + 1, :] * xp_scr[off:off + tl, :]
    a, mult, gi = _lru_gates(xc, wg_ref, bg_ref, lam_ref)
    row = lax.broadcasted_iota(jnp.int32, (tl, D_LRU), 0)
    mult = jnp.where((row == 0) & (t == 0), 1.0, mult)
    b = mult * gi * xc
    d = 1
    while d < tl:
        keep = row >= d
        a_sh = jnp.where(keep, pltpu.roll(a, d, axis=0), 1.0)
        b_sh = jnp.where(keep, pltpu.roll(b, d, axis=0), 0.0)
        b = a * b_sh + b
        a = a * a_sh
        d *= 2
    h = a * hc_scr[...] + b
    o_ref[...] = h * jax.nn.gelu(yg_ref[...])
    last = h[tl - 1:tl, :]
    hc_scr[...] = last
    hl_ref[...] = jnp.broadcast_to(last, (SUBLANES, D_LRU))
    xp_scr[0:halo, :] = xp_scr[tl:tl + halo, :]


def _lru_prompt_call(xr, yg, lw, *, bsz, seq_len):
    tl = min(LRU_TILE, seq_len)
    nt = seq_len // tl
    row = pl.BlockSpec((tl, D_LRU), lambda b, t: (b * nt + t, 0))
    return pl.pallas_call(
        _lru_prompt_kernel,
        grid=(bsz, nt),
        in_specs=[row, row, _const_spec((LRU_CONV, D_LRU)), _const_spec((1, D_LRU)),
                  _const_spec((D_LRU, 2 * D_LRU)), _const_spec((1, 2 * D_LRU)), _const_spec((1, D_LRU))],
        out_specs=[row, pl.BlockSpec((None, SUBLANES, D_LRU), lambda b, t: (b, 0, 0))],
        out_shape=[jax.ShapeDtypeStruct((bsz * seq_len, D_LRU), F32),
                   jax.ShapeDtypeStruct((bsz, SUBLANES, D_LRU), F32)],
        scratch_shapes=[pltpu.VMEM((tl + 2 * SUBLANES, D_LRU), F32), pltpu.VMEM((1, D_LRU), F32)],
        compiler_params=_cparams("arbitrary", "arbitrary"),
        name="lru_prompt",
    )(xr, yg, lw["lru_conv_w"], lw["lru_conv_b"], lw["lru_wg"], lw["lru_bg"], lw["lru_lambda"])


def _lru_sample_kernel(xr_ref, yg_ref, buf_ref, h0_ref, cw_ref, cb_ref, wg_ref, bg_ref, lam_ref,
                       o_ref, hl_ref, *, dseq):
    n = h0_ref.shape[0]
    xp = [buf_ref[j] for j in range(LRU_CONV - 1)] + [xr_ref[pl.ds(t * n, n), :] for t in range(dseq)]
    xcs = []
    for t in range(dseq):
        xc = cb_ref[...]
        for j in range(LRU_CONV):
            xc = xc + cw_ref[j:j + 1, :] * xp[t + j]
        xcs.append(xc)
    xc = jnp.concatenate(xcs, axis=0)
    a, mult, gi = _lru_gates(xc, wg_ref, bg_ref, lam_ref)
    b = mult * gi * xc
    h = h0_ref[...]
    for t in range(dseq):
        h = a[t * n:(t + 1) * n] * h + b[t * n:(t + 1) * n]
        o_ref[pl.ds(t * n, n), :] = h * jax.nn.gelu(yg_ref[pl.ds(t * n, n), :])
    hl_ref[...] = h


def _lru_sample_call(xr, yg, buf_tm, h0, lw, *, dseq):
    n = h0.shape[0]
    return pl.pallas_call(
        functools.partial(_lru_sample_kernel, dseq=dseq),
        out_shape=[jax.ShapeDtypeStruct((dseq * n, D_LRU), F32), jax.ShapeDtypeStruct((n, D_LRU), F32)],
        compiler_params=pltpu.CompilerParams(vmem_limit_bytes=VMEM_LIMIT_BYTES),
        name="lru_sample",
    )(xr, yg, buf_tm, h0, lw["lru_conv_w"], lw["lru_conv_b"], lw["lru_wg"], lw["lru_bg"], lw["lru_lambda"])


def _post_kernel(x_ref, oa_ref, ys_ref, u_ref, ol_ref, g1_ref, sh2_ref, sc2_ref, g2_ref,
                 d_ref, wglu_ref, bglu_ref, on_ref, wo_ref, n2_ref, wup_ref, cw_ref, cb_ref, wdn_ref,
                 halo_in_ref, xo_ref, halo_out_ref, h2_scr, acc_scr, xp_scr, halo_scr, ys_scr,
                 *, per_token, row_shift, group_major):
    t = pl.program_id(1)
    tm = x_ref.shape[0]
    hrows = halo_in_ref.shape[2]

    @pl.when(t == 0)
    def _():
        halo_scr[...] = halo_in_ref[...]

    if group_major:
        _from_group_major(ys_ref, ys_scr, ys_ref.shape[1], SSM_CHUNK)
        ys = jnp.concatenate([ys_scr[h] for h in range(SSM_HALVES)], axis=1)
    else:
        ys = ys_ref[...]
    ys = ys + d_ref[...] * u_ref[...]
    gs = jax.nn.gelu(ys)
    o_ssm = gs * jax.nn.sigmoid(_dot(gs.astype(BF16), wglu_ref[...]) + bglu_ref[...])
    on = on_ref[...]
    c1 = D_ATTN
    c2 = c1 + D_SSM
    o = jnp.concatenate([_rms(oa_ref[...], on[:, :c1]), _rms(o_ssm, on[:, c1:c2]),
                         _rms(ol_ref[...], on[:, c2:])], axis=-1)
    x1 = x_ref[...] + _mod_row(g1_ref, per_token) * _dot(o.astype(BF16), wo_ref[...])
    h2 = _rms(x1, n2_ref[...]) * (1.0 + _mod_row(sc2_ref, per_token)) + _mod_row(sh2_ref, per_token)
    h2_scr[...] = h2.astype(BF16)
    acc_scr[...] = jnp.zeros_like(acc_scr)
    for c in range(N_FF_CHUNKS):
        halves = []
        for gv in range(2):
            up = _dot(h2_scr[...], wup_ref[gv, c])
            xp_scr[gv, 0:hrows, :] = halo_scr[gv, c]
            xp_scr[gv, hrows:hrows + tm, :] = up
            y = cb_ref[gv, c]
            for j in range(FFN_CONV):
                off = hrows - (FFN_CONV - 1 - j) * row_shift
                y = y + cw_ref[gv, c, j:j + 1, :] * xp_scr[gv, off:off + tm, :]
            halo_scr[gv, c] = xp_scr[gv, tm:tm + hrows, :]
            halves.append(y)
        act = jax.nn.gelu(halves[0]) * halves[1]
        acc_scr[...] += _dot(act.astype(BF16), wdn_ref[c])
    xo_ref[...] = x1 + _mod_row(g2_ref, per_token) * acc_scr[...]
    halo_out_ref[...] = halo_scr[...]


def _post_call(x, oa, ys, u, ol, mod, lw, halo_in, *, per_token, seq_len, row_shift):
    nt = x.shape[0]
    tm = min(TOKEN_TILE, nt)
    tiles_per_seq = seq_len // tm
    nseq = nt // seq_len
    hrows = halo_in.shape[3]
    row = lambda w: pl.BlockSpec((tm, w), lambda s, t: (s * tiles_per_seq + t, 0))

    def mspec(piece):
        if per_token:
            return pl.BlockSpec((None, tm, D_MODEL), lambda s, t: (piece, s * tiles_per_seq + t, 0))
        return pl.BlockSpec((None, None, SUBLANES, D_MODEL), lambda s, t: (piece, s, 0, 0))

    halo_spec = pl.BlockSpec((None, 2, N_FF_CHUNKS, hrows, FF_CHUNK), lambda s, t: (s, 0, 0, 0, 0))
    group_major = ys.ndim == 3
    if group_major:
        ys_spec = pl.BlockSpec((N_SSM_GROUPS, tm // SSM_CHUNK, SSM_CHUNK * SSM_GROUP),
                               lambda s, t: (0, s * tiles_per_seq + t, 0))
    else:
        ys_spec = row(D_SSM)
    return pl.pallas_call(
        functools.partial(_post_kernel, per_token=per_token, row_shift=row_shift, group_major=group_major),
        grid=(nseq, tiles_per_seq),
        in_specs=[
            row(D_MODEL), row(D_ATTN), ys_spec, row(D_SSM), row(D_LRU),
            mspec(2), mspec(3), mspec(4), mspec(5),
            _const_spec((1, D_SSM)), _const_spec((D_SSM, D_SSM)), _const_spec((1, D_SSM)),
            _const_spec((1, D_MODEL)), _const_spec((D_MODEL, D_MODEL), True), _const_spec((1, D_MODEL)),
            _const_spec((2, N_FF_CHUNKS, D_MODEL, FF_CHUNK), True),
            _const_spec((2, N_FF_CHUNKS, FFN_CONV, FF_CHUNK)),
            _const_spec((2, N_FF_CHUNKS, 1, FF_CHUNK)),
            _const_spec((N_FF_CHUNKS, FF_CHUNK, D_MODEL), True),
            halo_spec,
        ],
        out_specs=[row(D_MODEL), halo_spec],
        out_shape=[jax.ShapeDtypeStruct((nt, D_MODEL), F32),
                   jax.ShapeDtypeStruct((nseq, 2, N_FF_CHUNKS, hrows, FF_CHUNK), F32)],
        scratch_shapes=[pltpu.VMEM((tm, D_MODEL), BF16), pltpu.VMEM((tm, D_MODEL), F32),
                        pltpu.VMEM((2, tm + hrows, FF_CHUNK), F32),
                        pltpu.VMEM((2, N_FF_CHUNKS, hrows, FF_CHUNK), F32),
                        pltpu.VMEM((SSM_HALVES, tm, LANES), F32)],
        compiler_params=_cparams("arbitrary", "arbitrary"),
        name="post",
    )(x, oa, ys, u, ol, mod, mod, mod, mod, lw["ssm_d"], lw["ssm_w_glu"], lw["ssm_b_glu"], lw["out_norm"],
      lw["w_o"], lw["norm2"], lw["w_up"], lw["ffn_conv_w"], lw["ffn_conv_b"], lw["w_down"], halo_in)


def _block_diag(w):
    depth, nb, bs, _ = w.shape
    eye = jnp.eye(nb, dtype=w.dtype)
    return jnp.einsum("lhij,hk->lhikj", w, eye).reshape(depth, nb * bs, nb * bs)


def _prepare_weights(p):
    depth = p["w_in"].shape[0]
    row = lambda a: a.reshape(depth, 1, -1)
    seg = (jnp.arange(D_ATTN)[:, None] // HEAD_DIM == jnp.arange(D_ATTN)[None, :] // HEAD_DIM).astype(BF16)
    w_up = p["ffn_w_up"].astype(BF16).reshape(depth, D_MODEL, 2, N_FF_CHUNKS, FF_CHUNK)
    cw = p["ffn_conv_w"].reshape(depth, FFN_CONV, 2, N_FF_CHUNKS, FF_CHUNK)
    return dict(
        norm1=row(p["norm1"]), norm2=row(p["norm2"]), out_norm=row(p["out_norm"]),
        w_in=p["w_in"].astype(BF16), w_o=p["w_o"].astype(BF16),
        q_norm=row(jnp.tile(p["q_norm"], (1, N_HEADS))), k_norm=row(jnp.tile(p["k_norm"], (1, N_KV_HEADS))),
        seg=jnp.broadcast_to(seg, (depth, D_ATTN, D_ATTN)),
        sinks=p["sinks"],
        ssm_d=row(p["ssm_d"]), ssm_w_glu=p["ssm_w_glu"].astype(BF16), ssm_b_glu=row(p["ssm_b_glu"]),
        lru_conv_w=p["lru_conv_w"], lru_conv_b=row(p["lru_conv_b"]),
        lru_wg=jnp.concatenate([_block_diag(p["lru_w_a"]), _block_diag(p["lru_w_i"])], axis=-1).astype(BF16),
        lru_bg=row(jnp.concatenate([p["lru_b_a"], p["lru_b_i"]], axis=-1)),
        lru_lambda=row(p["lru_lambda"]),
        w_up=jnp.transpose(w_up, (0, 2, 3, 1, 4)),
        ffn_conv_w=jnp.transpose(cw, (0, 2, 3, 1, 4)),
        ffn_conv_b=p["ffn_conv_b"].reshape(depth, 2, N_FF_CHUNKS, 1, FF_CHUNK),
        w_down=p["ffn_w_down"].astype(BF16).reshape(depth, N_FF_CHUNKS, FF_CHUNK, D_MODEL),
    )


def _ff_state_from_halo(halo, nrows):
    ns = halo.shape[0]
    last = halo[:, :, :, halo.shape[3] - nrows:, :]
    return jnp.transpose(last, (0, 3, 1, 2, 4)).reshape(ns, nrows, 2 * D_FF)


def _prompt_layer(x, mod, lw, ssm_w, cos, sin, *, bsz, seq_len):
    g, lc = N_SSM_GROUPS, SSM_CHUNK
    nchunk = seq_len // lc
    q, k, v, u, xr, yg, uf = _pre_call(x, mod, lw, cos, sin, per_token=False, seq_len=seq_len)
    oa = _attn_prompt_call(lw["sinks"], q, k, v, bsz=bsz, seq_len=seq_len)
    r, nmat, mt, apw = ssm_w
    yf, hend = _ssm_call(uf, r, nmat, mt, apw, nchunk=nchunk)
    ol, hl = _lru_prompt_call(xr, yg, lw, bsz=bsz, seq_len=seq_len)
    halo0 = jnp.zeros((bsz, 2, N_FF_CHUNKS, SUBLANES, FF_CHUNK), F32)
    x_new, halo = _post_call(x, oa, yf, u, ol, mod, lw, halo0, per_token=False, seq_len=seq_len, row_shift=1)
    keep = min(WINDOW, seq_len)
    k3 = k.reshape(bsz, seq_len, N_KV_HEADS, HEAD_DIM)
    v3 = v.reshape(bsz, seq_len, N_KV_HEADS, HEAD_DIM)
    hend_b = jnp.transpose(hend, (1, 0, 2))
    states = (k3[:, seq_len - keep:], v3[:, seq_len - keep:],
              hend_b[..., :SSM_STATE], hend_b[..., SSM_STATE:],
              hl[:, 0, :],
              xr.reshape(bsz, seq_len, D_LRU)[:, seq_len - (LRU_CONV - 1):],
              _ff_state_from_halo(halo, FFN_CONV - 1))
    return x_new, states


def _sample_layer(x, mod, lw, ssm_w, cos, sin, st, *, n, dseq):
    g = N_SSM_GROUPS
    ck, cv, s_re, s_im, lru_h, lru_conv, ffn_conv = st
    nt = n * dseq
    q, k, v, u, xr, yg = _pre_call(x, mod, lw, cos, sin, per_token=True, seq_len=nt)
    q_sm = jnp.transpose(q.reshape(dseq, n, N_KV_HEADS, KV_GROUP, HEAD_DIM), (1, 2, 0, 3, 4))
    q_sm = q_sm.reshape(n, N_KV_HEADS, dseq * KV_GROUP, HEAD_DIM)
    kn = jnp.transpose(k.reshape(dseq, n, D_KV), (1, 0, 2))
    vn = jnp.transpose(v.reshape(dseq, n, D_KV), (1, 0, 2))
    wbuf = ck.shape[1]
    o_sm, sk, sv = _attn_sample_call(lw["sinks"], q_sm, kn, vn, ck.reshape(n, wbuf, D_KV),
                                     cv.reshape(n, wbuf, D_KV), dseq=dseq)
    oa = jnp.transpose(o_sm.reshape(n, N_KV_HEADS, dseq, KV_GROUP, HEAD_DIM), (2, 0, 1, 3, 4)).reshape(nt, D_ATTN)
    r, nmat, mt, apw = ssm_w
    wpad = LANES - dseq * SSM_GROUP
    uf = jnp.transpose(u.reshape(dseq, n, g, SSM_GROUP), (2, 1, 0, 3)).reshape(g, n, dseq * SSM_GROUP)
    uf = jnp.pad(uf, ((0, 0), (0, 0), (0, wpad))).astype(BF16)
    h_re = jnp.transpose(s_re, (1, 0, 2))
    h_im = jnp.transpose(s_im, (1, 0, 2))
    h0 = jnp.concatenate([h_re, h_im], axis=-1)
    h0s = jnp.concatenate([h_im, h_re], axis=-1)
    yf, hend = _ssm_call(uf, r, nmat, mt, apw, h0, h0s, nchunk=1)
    yf = yf[:, :, :dseq * SSM_GROUP]
    ys = jnp.transpose(yf.reshape(g, n, dseq, SSM_GROUP), (2, 1, 0, 3)).reshape(nt, D_SSM)
    ol, hl = _lru_sample_call(xr, yg, jnp.transpose(lru_conv, (1, 0, 2)), lru_h, lw, dseq=dseq)
    nconv = FFN_CONV - 1
    halo0 = jnp.transpose(ffn_conv.reshape(n, nconv, 2, N_FF_CHUNKS, FF_CHUNK), (2, 3, 1, 0, 4))
    halo0 = halo0.reshape(1, 2, N_FF_CHUNKS, nconv * n, FF_CHUNK)
    x_new, halo = _post_call(x, oa, ys, u, ol, mod, lw, halo0, per_token=True, seq_len=nt, row_shift=n)
    hend_b = jnp.transpose(hend, (1, 0, 2))
    new_fc = jnp.transpose(halo.reshape(2, N_FF_CHUNKS, nconv, n, FF_CHUNK), (3, 2, 0, 1, 4)).reshape(n, nconv, 2 * D_FF)
    xr_tm = xr.reshape(dseq, n, D_LRU)
    lru_conv_all = jnp.concatenate([jnp.transpose(lru_conv, (1, 0, 2)), xr_tm], axis=0)
    states = (sk.reshape(n, wbuf, N_KV_HEADS, HEAD_DIM), sv.reshape(n, wbuf, N_KV_HEADS, HEAD_DIM),
              hend_b[..., :SSM_STATE], hend_b[..., SSM_STATE:],
              hl,
              jnp.transpose(lru_conv_all[dseq:], (1, 0, 2)),
              new_fc)
    return x_new, states


def kernel(x_prompt, x_sample, cache_k, cache_v, state_ssm_re, state_ssm_im, state_lru_h, state_lru_conv,
           state_ffn_conv, c_prompt, c_sample, w_ada, b_ada, norm1, w_in, q_norm, k_norm, sinks, ssm_a_re,
           ssm_a_im, ssm_b_re, ssm_b_im, ssm_c_re, ssm_c_im, ssm_d, ssm_log_dt, ssm_w_glu, ssm_b_glu,
           lru_conv_w, lru_conv_b, lru_w_a, lru_b_a, lru_w_i, lru_b_i, lru_lambda, out_norm, w_o, norm2,
           ffn_w_up, ffn_conv_w, ffn_conv_b, ffn_w_down):
    bsz, seq_len = x_prompt.shape[:2]
    n, dseq = x_sample.shape[:2]
    depth = w_in.shape[0]
    params = dict(norm1=norm1, w_in=w_in, q_norm=q_norm, k_norm=k_norm, sinks=sinks, ssm_d=ssm_d,
                  ssm_w_glu=ssm_w_glu, ssm_b_glu=ssm_b_glu, lru_conv_w=lru_conv_w, lru_conv_b=lru_conv_b,
                  lru_w_a=lru_w_a, lru_b_a=lru_b_a, lru_w_i=lru_w_i, lru_b_i=lru_b_i, lru_lambda=lru_lambda,
                  out_norm=out_norm, w_o=w_o, norm2=norm2, ffn_w_up=ffn_w_up, ffn_conv_w=ffn_conv_w,
                  ffn_conv_b=ffn_conv_b, ffn_w_down=ffn_w_down)
    weights = _prepare_weights(params)

    rows = bsz + n
    pad = (-rows) % SUBLANES
    c_all = jnp.concatenate([c_prompt, c_sample, jnp.zeros((pad, D_MODEL), F32)], axis=0)
    mod_all = _ada_call(c_all, w_ada, b_ada).reshape(depth, rows + pad, N_MOD, D_MODEL)
    mod_p = jnp.transpose(mod_all[:, :bsz], (0, 2, 1, 3))
    mod_p = jnp.broadcast_to(mod_p[:, :, :, None, :], (depth, N_MOD, bsz, SUBLANES, D_MODEL))
    mod_s = jnp.transpose(mod_all[:, bsz:rows], (0, 2, 1, 3))
    mod_s = jnp.tile(mod_s, (1, 1, dseq, 1))

    cos_p, sin_p = _rope_call(jnp.arange(seq_len, dtype=F32))
    cos_s, sin_s = _rope_call(jnp.repeat(PAST_LEN + jnp.arange(dseq, dtype=F32), n))

    nchunk = seq_len // SSM_CHUNK
    nsteps = max(nchunk - 1, 0).bit_length()
    powers = (dseq,) + tuple(SSM_CHUNK * (1 << i) for i in range(nsteps))
    assert dseq * SSM_GROUP <= LANES
    r_all, np_all, mtp_all, ns_all, mts_all, apw = _ssm_prep_call(
        ssm_a_re, ssm_a_im, ssm_log_dt, ssm_b_re, ssm_b_im, ssm_c_re, ssm_c_im,
        lc=SSM_CHUNK, dseq=dseq, powers=powers)

    xp = x_prompt.reshape(bsz * seq_len, D_MODEL)
    xs = jnp.transpose(x_sample, (1, 0, 2)).reshape(dseq * n, D_MODEL)
    new_p, new_s = [], []
    for i in range(depth):
        lw = {name: arr[i] for name, arr in weights.items()}
        ssm_p = (r_all[i], np_all[i], mtp_all[i], apw[i, 1:])
        ssm_s = (r_all[i], ns_all[i], mts_all[i], apw[i, 0:1])
        xp, st_p = _prompt_layer(xp, mod_p[i], lw, ssm_p, cos_p, sin_p, bsz=bsz, seq_len=seq_len)
        st_in = (cache_k[i], cache_v[i], state_ssm_re[i], state_ssm_im[i], state_lru_h[i], state_lru_conv[i],
                 state_ffn_conv[i])
        xs, st_s = _sample_layer(xs, mod_s[i], lw, ssm_s, cos_s, sin_s, st_in, n=n, dseq=dseq)
        new_p.append(st_p)
        new_s.append(st_s)
    pk, pv, p_re, p_im, p_lh, p_lc, p_fc = [jnp.stack(s) for s in zip(*new_p)]
    sk, sv, s_re, s_im, s_lh, s_lc, s_fc = [jnp.stack(s) for s in zip(*new_s)]
    y_p = xp.reshape(bsz, seq_len, D_MODEL)
    y_s = jnp.transpose(xs.reshape(dseq, n, D_MODEL), (1, 0, 2))
    return (y_p, y_s, pk, pv, p_re, p_im, p_lh, p_lc, p_fc, sk, sv, s_re, s_im, s_lh, s_lc, s_fc)
```

```python
import functools

import jax
import jax.numpy as jnp
from jax import lax
from jax.experimental import pallas as pl
from jax.experimental.pallas import tpu as pltpu

F32 = jnp.float32
BF16 = jnp.bfloat16

D_MODEL = 1024
HEAD_DIM = 64
N_HEADS = 8
N_KV_HEADS = 2
KV_GROUP = N_HEADS // N_KV_HEADS
D_ATTN = N_HEADS * HEAD_DIM
D_KV = N_KV_HEADS * HEAD_DIM
WINDOW = 128
ROPE_THETA = 10000.0
PAST_LEN = 8192
D_SSM = 256
SSM_GROUP = 16
N_SSM_GROUPS = 16
SSM_STATE = 64
D_LRU = 256
N_LRU_BLOCKS = 4
LRU_BLOCK = 64
LRU_CONV = 4
LRU_C = 8.0
D_FF = 2816
FFN_CONV = 3
D_IN = D_ATTN + 2 * D_KV + D_SSM + 2 * D_LRU
N_MOD = 6
EPS = 1e-6

SUBLANES = 8
LANES = 128
VMEM_LIMIT_BYTES = 56 * 1024 * 1024

TOKEN_TILE = 512
ATTN_BLOCK = WINDOW
ATTN_SUBBLOCKS = 4
SSM_CHUNK = 32
LRU_TILE = 512
FF_CHUNK = 256
N_FF_CHUNKS = D_FF // FF_CHUNK
FF_ROW_SPLIT = 4
SAMPLE_ATTN_SEQS = 16
SSM_HALVES = D_SSM // LANES
GROUPS_PER_HALF = LANES // SSM_GROUP


def _cparams(*sem):
    return pltpu.CompilerParams(dimension_semantics=sem, vmem_limit_bytes=VMEM_LIMIT_BYTES)


def _dot(a, b):
    return jnp.dot(a, b, preferred_element_type=F32)


def _split_bf16(x):
    hi = x.astype(BF16)
    lo = (x - hi.astype(F32)).astype(BF16)
    return hi, lo


def _layer_spec(shape, layer, single=False):
    nd = len(shape)
    idx = lambda *_: (layer,) + (0,) * nd
    if single:
        return pl.BlockSpec((None,) + tuple(shape), idx, pipeline_mode=pl.Buffered(1))
    return pl.BlockSpec((None,) + tuple(shape), idx)


def _ada_kernel(c_ref, w_ref, b_ref, o_ref):
    c = c_ref[...]
    s = jax.nn.silu(c).astype(BF16)
    o_ref[...] = _dot(s, w_ref[...].astype(BF16)) + b_ref[...]


def _ada_call(c_all, w_ada, b_ada):
    depth = w_ada.shape[0]
    rows = c_all.shape[0]
    return pl.pallas_call(
        _ada_kernel,
        grid=(depth, N_MOD),
        in_specs=[
            pl.BlockSpec((rows, D_MODEL), lambda l, j: (0, 0)),
            pl.BlockSpec((None, D_MODEL, D_MODEL), lambda l, j: (l, 0, j)),
            pl.BlockSpec((None, 1, D_MODEL), lambda l, j: (l, 0, j)),
        ],
        out_specs=pl.BlockSpec((None, None, rows, D_MODEL), lambda l, j: (l, j, 0, 0)),
        out_shape=jax.ShapeDtypeStruct((depth, N_MOD, rows, D_MODEL), F32),
        compiler_params=_cparams("arbitrary", "arbitrary"),
        name="ada",
    )(c_all, w_ada, b_ada.reshape(depth, 1, N_MOD * D_MODEL))


def _rope_kernel(pos_ref, cos_ref, sin_ref):
    pos = pos_ref[...]
    lane = lax.broadcasted_iota(jnp.int32, pos.shape, 1)
    half = HEAD_DIM // 2
    j = (lane & (half - 1)).astype(F32)
    inv = ROPE_THETA ** (-j / half)
    ang = pos * inv
    cos_ref[...] = jnp.cos(ang)
    s = jnp.sin(ang)
    sin_ref[...] = jnp.where((lane & (HEAD_DIM - 1)) < half, -s, s)


def _rope_call(pos_rows):
    t = pos_rows.shape[0]
    pos_b = jnp.broadcast_to(pos_rows[:, None], (t, LANES))
    return pl.pallas_call(
        _rope_kernel,
        out_shape=(jax.ShapeDtypeStruct((t, LANES), F32), jax.ShapeDtypeStruct((t, LANES), F32)),
        name="rope",
    )(pos_b)


def _mod_rows(ref, per_token, tm):
    if not per_token:
        return ref[0:1, :]
    m = ref[...]
    reps = tm // m.shape[0]
    return jnp.concatenate([m] * reps, axis=0) if reps > 1 else m


def _mod_spec(layer, piece, per_token, seq_of_step):
    if per_token:
        return lambda n: pl.BlockSpec((None, None, n, D_MODEL), lambda *g: (layer, piece, 0, 0))
    return lambda n: pl.BlockSpec((None, None, None, SUBLANES, D_MODEL),
                                  lambda *g: (layer, piece, seq_of_step(*g), 0, 0))


def _rms(x, gain):
    return x * lax.rsqrt(jnp.mean(x * x, axis=-1, keepdims=True) + EPS) * gain


def _head_rms(t, seg, gain):
    hi, lo = _split_bf16(t * t)
    ss = _dot(hi, seg) + _dot(lo, seg)
    return t * lax.rsqrt(ss * (1.0 / HEAD_DIM) + EPS) * gain


def _rope(t, cos, sin):
    width = t.shape[1]
    reps = width // LANES
    if reps > 1:
        cos = jnp.concatenate([cos] * reps, axis=1)
        sin = jnp.concatenate([sin] * reps, axis=1)
    lane = lax.broadcasted_iota(jnp.int32, t.shape, 1)
    half = HEAD_DIM // 2
    up = pltpu.roll(t, width - half, axis=1)
    dn = pltpu.roll(t, half, axis=1)
    rot = jnp.where((lane & (HEAD_DIM - 1)) < half, up, dn)
    return t * cos + rot * sin


def _unit_transpose8(vs):
    lane = lax.broadcasted_iota(jnp.int32, vs[0].shape, 1)
    unit = lane >> 4
    for b in range(3):
        d = 1 << b
        bit = (unit >> b) & 1
        new = list(vs)
        for i in range(8):
            if (i >> b) & 1 == 0:
                lo, hi = vs[i], vs[i + d]
                new[i] = jnp.where(bit == 0, lo, pltpu.roll(hi, d * SSM_GROUP, axis=1))
                new[i + d] = jnp.where(bit == 1, hi, pltpu.roll(lo, LANES - d * SSM_GROUP, axis=1))
        vs = new
    return vs


def _to_group_major(u_ref, nk, lc):
    outs = [[None] * (lc // 8) for _ in range(N_SSM_GROUPS)]
    for h in range(SSM_HALVES):
        for tb in range(lc // 8):
            vs = [u_ref[h, pl.ds(tb * 8 + tp, nk, stride=lc), :] for tp in range(8)]
            ws = _unit_transpose8(vs)
            for gp in range(GROUPS_PER_HALF):
                outs[h * GROUPS_PER_HALF + gp][tb] = ws[gp]
    return [jnp.concatenate(o, axis=1) for o in outs]


def _from_group_major(yf_ref, ys_ref, nk, lc):
    for h in range(SSM_HALVES):
        for tb in range(lc // 8):
            ws = [yf_ref[h * GROUPS_PER_HALF + gp, :, tb * LANES:(tb + 1) * LANES] for gp in range(GROUPS_PER_HALF)]
            vs = _unit_transpose8(ws)
            for tp in range(8):
                ys_ref[h, pl.ds(tb * 8 + tp, nk, stride=lc), :] = vs[tp]


def _pre_kernel(x_ref, sh_ref, sc_ref, n1_ref, w_ref, qn_ref, kn_ref, seg_ref, cos_ref, sin_ref,
                q_ref, k_ref, v_ref, u_ref, xr_ref, yg_ref, *maybe_uf_ref, per_token):
    x = x_ref[...]
    tm = x.shape[0]
    h = _rms(x, n1_ref[...]) * (1.0 + _mod_rows(sc_ref, per_token, tm)) + _mod_rows(sh_ref, per_token, tm)
    proj = _dot(h.astype(BF16), w_ref[...])
    c1 = D_ATTN
    c2 = c1 + D_KV
    c3 = c2 + D_KV
    c4 = c3 + D_SSM
    c5 = c4 + D_LRU
    cos = cos_ref[...]
    sin = sin_ref[...]
    seg = seg_ref[...]
    q = _rope(_head_rms(proj[:, :c1], seg, qn_ref[...]), cos, sin)
    k = _rope(_head_rms(proj[:, c1:c2], seg[:D_KV, :D_KV], kn_ref[...]), cos, sin)
    q_ref[...] = q.astype(BF16)
    k_ref[...] = k
    v_ref[...] = proj[:, c2:c3]
    u_ref[...] = proj[:, c3:c4]
    xr_ref[...] = proj[:, c4:c5]
    yg_ref[...] = proj[:, c5:]
    if maybe_uf_ref:
        uf_ref, u_scr = maybe_uf_ref
        for hh in range(SSM_HALVES):
            u_scr[hh] = proj[:, c3 + hh * LANES:c3 + (hh + 1) * LANES]
        groups = _to_group_major(u_scr, uf_ref.shape[1], SSM_CHUNK)
        for g in range(N_SSM_GROUPS):
            uf_ref[g] = groups[g].astype(BF16)


def _pre_call(x, mod, w, seg, cos, sin, *, layer, per_token, seq_len, nseq=None):
    nt = x.shape[0]
    tm = min(TOKEN_TILE, nt)
    tiles_per_seq = seq_len // tm
    if per_token:
        tab_spec = pl.BlockSpec((tm, LANES), lambda i: (i, 0))
    else:
        tab_spec = pl.BlockSpec((tm, LANES), lambda i: (i % tiles_per_seq, 0))
    mspec = lambda piece: _mod_spec(layer, piece, per_token, lambda i: i // tiles_per_seq)(nseq)
    row = lambda wd: pl.BlockSpec((tm, wd), lambda i: (i, 0))
    widths = (D_ATTN, D_KV, D_KV, D_SSM, D_LRU, D_LRU)
    dtypes = (BF16, F32, F32, F32, F32, F32)
    out_specs = [row(wd) for wd in widths]
    out_shape = [jax.ShapeDtypeStruct((nt, wd), d) for wd, d in zip(widths, dtypes)]
    scratch = []
    if not per_token:
        nk = tm // SSM_CHUNK
        wf = SSM_CHUNK * SSM_GROUP
        out_specs.append(pl.BlockSpec((N_SSM_GROUPS, nk, wf), lambda i: (0, i, 0)))
        out_shape.append(jax.ShapeDtypeStruct((N_SSM_GROUPS, nt // SSM_CHUNK, wf), BF16))
        scratch.append(pltpu.VMEM((SSM_HALVES, tm, LANES), F32))
    return pl.pallas_call(
        functools.partial(_pre_kernel, per_token=per_token),
        grid=(nt // tm,),
        in_specs=[
            row(D_MODEL), mspec(0), mspec(1),
            _layer_spec((1, D_MODEL), layer),
            _layer_spec((D_MODEL, D_IN), layer, True),
            _layer_spec((1, D_ATTN), layer),
            _layer_spec((1, D_KV), layer),
            pl.BlockSpec((D_ATTN, D_ATTN), lambda i: (0, 0)),
            tab_spec, tab_spec,
        ],
        out_specs=out_specs,
        out_shape=out_shape,
        scratch_shapes=scratch,
        compiler_params=_cparams("arbitrary"),
        name="pre",
    )(x, mod, mod, w["norm1"], w["w_in"], w["q_norm"], w["k_norm"], seg, cos, sin)


def _attn_prompt_kernel(sink_ref, q_ref, kc_ref, kp_ref, vc_ref, vp_ref, o_ref, *, layer, nsub):
    assert (KV_GROUP, N_KV_HEADS, 2 * HEAD_DIM) == (4, 2, LANES)
    i = pl.program_id(1)
    bq = ATTN_BLOCK
    kcat = jnp.concatenate([kp_ref[...], kc_ref[...]], axis=0).astype(BF16)
    vcat = jnp.concatenate([vp_ref[...], vc_ref[...]], axis=0).astype(BF16)
    low = lax.broadcasted_iota(jnp.int32, kcat.shape, 1) < HEAD_DIM

    def half_placed(x):
        zero = jnp.zeros_like(x)
        h0_lo = jnp.where(low, x, zero)
        h1_hi = jnp.where(low, zero, x)
        return [[h0_lo, pltpu.roll(h0_lo, HEAD_DIM, axis=1)], [pltpu.roll(h1_hi, HEAD_DIM, axis=1), h1_hi]]

    kz = half_placed(kcat)
    vz = half_placed(vcat)
    qi = lax.broadcasted_iota(jnp.int32, (bq, 2 * bq), 0)
    si = lax.broadcasted_iota(jnp.int32, (bq, 2 * bq), 1)
    diff = qi + bq - si
    in_window = (diff >= 0) & (diff < WINDOW)
    first_pair = lax.broadcasted_iota(jnp.int32, (2 * bq, 1), 0) < bq
    for j in range(nsub):
        if j == 0:
            allowed = in_window & ((si >= bq) | (i > 0))
        else:
            allowed = in_window
        allowed = jnp.concatenate([allowed, allowed], axis=0)
        band = slice(j * bq, (j + 2) * bq)
        qrows = slice(j * bq, (j + 1) * bq)
        for h in range(N_KV_HEADS):
            qh = jnp.concatenate([q_ref[qrows, (2 * h + p) * LANES:(2 * h + p + 1) * LANES] for p in range(2)],
                                 axis=0)
            o_pair = None
            for e in range(2):
                s = lax.dot_general(qh, kz[h][e][band], (((1,), (1,)), ((), ())), preferred_element_type=F32)
                s = s * (HEAD_DIM ** -0.5)
                s = jnp.where(allowed, s, -jnp.inf)
                sink = jnp.where(first_pair, sink_ref[layer, KV_GROUP * h + e], sink_ref[layer, KV_GROUP * h + 2 + e])
                m = jnp.maximum(jnp.max(s, axis=-1, keepdims=True), sink)
                p = jnp.exp(s - m)
                denom = jnp.sum(p, axis=-1, keepdims=True) + jnp.exp(sink - m)
                o = _dot(p.astype(BF16), vz[h][e][band]) / denom
                o_pair = o if o_pair is None else o_pair + o
            o_ref[qrows, (2 * h) * LANES:(2 * h + 1) * LANES] = o_pair[:bq]
            o_ref[qrows, (2 * h + 1) * LANES:(2 * h + 2) * LANES] = o_pair[bq:]


def _attn_prompt_call(sinks, q, k, v, *, layer, bsz, seq_len):
    nsub = min(ATTN_SUBBLOCKS, seq_len // ATTN_BLOCK)
    bq = nsub * ATTN_BLOCK
    nb = seq_len // bq
    nb_small = seq_len // ATTN_BLOCK
    cur = lambda w: pl.BlockSpec((bq, w), lambda b, i: (b * nb + i, 0))
    prev = lambda w: pl.BlockSpec((ATTN_BLOCK, w), lambda b, i: (b * nb_small + jnp.maximum(i * nsub - 1, 0), 0))
    return pl.pallas_call(
        functools.partial(_attn_prompt_kernel, layer=layer, nsub=nsub),
        grid=(bsz, nb),
        in_specs=[pl.BlockSpec(memory_space=pltpu.SMEM), cur(D_ATTN), cur(D_KV), prev(D_KV), cur(D_KV),
                  prev(D_KV)],
        out_specs=cur(D_ATTN),
        out_shape=jax.ShapeDtypeStruct((bsz * seq_len, D_ATTN), F32),
        compiler_params=_cparams("arbitrary", "arbitrary"),
        name="attn_prompt",
    )(sinks, q, k, k, v, v)


def _attn_sample_kernel(sink_ref, q_ref, kn_ref, vn_ref, ck_ref, cv_ref, o_ref, sk_ref, sv_ref, *, layer, dseq):
    kk = jnp.concatenate([ck_ref[...], kn_ref[...]], axis=1)
    vv = jnp.concatenate([cv_ref[...], vn_ref[...]], axis=1)
    wbuf = ck_ref.shape[1]
    sk_ref[...] = kk[:, dseq:, :]
    sv_ref[...] = vv[:, dseq:, :]
    nq = dseq * KV_GROUP
    nk = wbuf + dseq
    qrow = lax.broadcasted_iota(jnp.int32, (nq, nk), 0)
    j = lax.broadcasted_iota(jnp.int32, (nq, nk), 1)
    diff = (qrow >> 2) + wbuf - j
    allowed = ((diff >= 0) & (diff < WINDOW))[None]
    g_of_row = lax.broadcasted_iota(jnp.int32, (nq, 1), 0) & (KV_GROUP - 1)
    kkb = kk.astype(BF16)
    vvb = vv.astype(BF16)
    for h in range(N_KV_HEADS):
        kh = kkb[:, :, h * HEAD_DIM:(h + 1) * HEAD_DIM]
        vh = vvb[:, :, h * HEAD_DIM:(h + 1) * HEAD_DIM]
        qh = q_ref[:, h]
        s = jnp.einsum("sqd,skd->sqk", qh, kh, preferred_element_type=F32) * (HEAD_DIM ** -0.5)
        s = jnp.where(allowed, s, -jnp.inf)
        sink = jnp.full((nq, 1), sink_ref[layer, h * KV_GROUP], F32)
        for g in range(1, KV_GROUP):
            sink = jnp.where(g_of_row == g, sink_ref[layer, h * KV_GROUP + g], sink)
        sink = sink[None]
        m = jnp.maximum(jnp.max(s, axis=-1, keepdims=True), sink)
        p = jnp.exp(s - m)
        denom = jnp.sum(p, axis=-1, keepdims=True) + jnp.exp(sink - m)
        o = jnp.einsum("sqk,skd->sqd", p.astype(BF16), vh, preferred_element_type=F32) / denom
        o_ref[:, h] = o


def _attn_sample_call(sinks, q, kn, vn, ck, cv, *, layer, dseq):
    n, wbuf = ck.shape[0], ck.shape[1]
    sb = min(SAMPLE_ATTN_SEQS, n)
    nq = dseq * KV_GROUP
    qspec = pl.BlockSpec((sb, N_KV_HEADS, nq, HEAD_DIM), lambda i: (i, 0, 0, 0))
    nspec = pl.BlockSpec((sb, dseq, D_KV), lambda i: (i, 0, 0))
    cspec = pl.BlockSpec((sb, wbuf, D_KV), lambda i: (i, 0, 0))
    return pl.pallas_call(
        functools.partial(_attn_sample_kernel, layer=layer, dseq=dseq),
        grid=(n // sb,),
        in_specs=[pl.BlockSpec(memory_space=pltpu.SMEM), qspec, nspec, nspec, cspec, cspec],
        out_specs=[qspec, cspec, cspec],
        out_shape=[jax.ShapeDtypeStruct((n, N_KV_HEADS, nq, HEAD_DIM), F32),
                   jax.ShapeDtypeStruct((n, wbuf, D_KV), F32),
                   jax.ShapeDtypeStruct((n, wbuf, D_KV), F32)],
        compiler_params=_cparams("arbitrary"),
        name="attn_sample",
    )(sinks, q, kn, vn, ck, cv)


def _bdot3(a, b):
    dn = (((2,), (2,)), ((0,), (0,)))
    ah, al = _split_bf16(a)
    bh, bl = _split_bf16(b)
    d = lambda x, y: lax.dot_general(x, y, dn, preferred_element_type=F32)
    return d(ah, bh) + d(ah, bl) + d(al, bh)


def _ssm_prep_kernel(are_ref, aim_ref, ldt_ref, bre_ref, bim_ref, cre_ref, cim_ref,
                     r_ref, np_ref, mtp_ref, ns_ref, mts_ref, apw_ref, *, lc, dseq, powers):
    c = SSM_GROUP
    a_re = are_ref[...]
    a_im = aim_ref[...]
    dt = jnp.exp(ldt_ref[...])
    zr = a_re * dt
    zi = a_im * dt

    def a_pow(j):
        mag = jnp.exp(zr * j)
        return mag * jnp.cos(zi * j), mag * jnp.sin(zi * j)

    abr, abi = a_pow(1.0)
    xr = abr - 1.0
    den = a_re * a_re + a_im * a_im
    coef_r = (xr * a_re + abi * a_im) / den
    coef_i = (abi * a_re - xr * a_im) / den
    btr = bre_ref[...]
    bti = bim_ref[...]
    bbr = coef_r * btr - coef_i * bti
    bbi = coef_r * bti + coef_i * btr
    cre = cre_ref[...]
    cim = cim_ref[...]
    ns_ref[...] = jnp.zeros_like(ns_ref)
    mts_ref[...] = jnp.zeros_like(mts_ref)
    cars, cais = [], []
    for j in range(lc + 1):
        er, ei = a_pow(float(j))
        car = cre * er - cim * ei
        cai = cre * ei + cim * er
        if j < lc:
            cars.append(car)
            cais.append(cai)
            nbr = er * bbr - ei * bbi
            nbi = er * bbi + ei * bbr
            ncat = jnp.concatenate([nbr, nbi, nbi, nbr], axis=-1).astype(BF16)
            s = lc - 1 - j
            np_ref[:, s * c:(s + 1) * c, :] = ncat
            if j < dseq:
                s = dseq - 1 - j
                ns_ref[:, s * c:(s + 1) * c, :] = ncat
        if j >= 1:
            mcat = jnp.concatenate([car, -cai], axis=-1).astype(BF16)
            mtp_ref[:, (j - 1) * c:j * c, :] = mcat
            if j <= dseq:
                mts_ref[:, (j - 1) * c:j * c, :] = mcat
    ca_r = jnp.concatenate(cars, axis=1)
    ca_i = jnp.concatenate(cais, axis=1)
    r_ref[...] = _bdot3(bbr, ca_r) - _bdot3(bbi, ca_i)
    for idx, pw in enumerate(powers):
        er, ei = a_pow(float(pw))
        apw_ref[idx, 0] = jnp.concatenate([er, er], axis=-1)
        apw_ref[idx, 1] = jnp.concatenate([-ei, ei], axis=-1)
        apw_ref[idx, 2] = jnp.concatenate([ei, -ei], axis=-1)


def _ssm_prep_call(a_re, a_im, log_dt, b_re, b_im, c_re, c_im, *, lc, dseq, powers):
    depth = a_re.shape[0]
    g, p, c = N_SSM_GROUPS, SSM_STATE, SSM_GROUP
    npw = len(powers)
    a_spec = pl.BlockSpec((None, g, 1, p), lambda l: (l, 0, 0, 0))
    m_spec = pl.BlockSpec((None, g, c, p), lambda l: (l, 0, 0, 0))
    out4 = lambda a, b: pl.BlockSpec((None, g, a, b), lambda l: (l, 0, 0, 0))
    shape4 = lambda a, b, d: jax.ShapeDtypeStruct((depth, g, a, b), d)
    ws = LANES
    return pl.pallas_call(
        functools.partial(_ssm_prep_kernel, lc=lc, dseq=dseq, powers=powers),
        grid=(depth,),
        in_specs=[a_spec, a_spec, a_spec, m_spec, m_spec, m_spec, m_spec],
        out_specs=[out4(c, lc * c), out4(lc * c, 4 * p), out4(lc * c, 2 * p), out4(ws, 4 * p), out4(ws, 2 * p),
                   pl.BlockSpec((None, npw, 3, g, 1, 2 * p), lambda l: (l, 0, 0, 0, 0, 0))],
        out_shape=[shape4(c, lc * c, F32), shape4(lc * c, 4 * p, BF16), shape4(lc * c, 2 * p, BF16),
                   shape4(ws, 4 * p, BF16), shape4(ws, 2 * p, BF16),
                   jax.ShapeDtypeStruct((depth, npw, 3, g, 1, 2 * p), F32)],
        compiler_params=_cparams("arbitrary"),
        name="ssm_prep",
    )(a_re.reshape(depth, g, 1, p), a_im.reshape(depth, g, 1, p),
      jnp.broadcast_to(log_dt[:, :, None, None], (depth, g, 1, p)),
      jnp.swapaxes(b_re, -1, -2), jnp.swapaxes(b_im, -1, -2), c_re, c_im)


def _ssm_kernel(*refs, nsteps, first_power, nchunk, nseq, has_h0):
    if has_h0:
        u_ref, r_ref, n_ref, mt_ref, apw_ref, h0_ref, h0s_ref, y_ref, hend_ref, toe_scr = refs
    else:
        u_ref, r_ref, n_ref, mt_ref, apw_ref, y_ref, hend_ref, toe_scr = refs
    w = u_ref.shape[1]
    c = SSM_GROUP
    r = r_ref[...]
    lane = lax.broadcasted_iota(jnp.int32, r.shape, 1)
    toe_scr[0:c, :] = r.astype(BF16)
    for s in range(1, w // c):
        toe_scr[s * c:(s + 1) * c, :] = jnp.where(lane >= s * c, pltpu.roll(r, s * c, axis=1), 0.0).astype(BF16)
    uf = u_ref[...]
    y = _dot(uf, toe_scr[...])
    st = _dot(uf, n_ref[...])
    ws = 2 * SSM_STATE
    h = st[:, :ws]
    hs = st[:, ws:]
    kidx = lax.broadcasted_iota(jnp.int32, h.shape, 0) & (nchunk - 1)
    shift = lambda x, d: jnp.where(kidx >= d, pltpu.roll(x, d, axis=0), 0.0)
    for i in range(nsteps):
        d = 1 << i
        pw = first_power + i
        a1, a2, a3 = apw_ref[pw, 0], apw_ref[pw, 1], apw_ref[pw, 2]
        hd = shift(h, d)
        hsd = shift(hs, d)
        h, hs = h + a1 * hd + a2 * hsd, hs + a1 * hsd + a3 * hd
    if has_h0:
        h0 = h0_ref[...]
        hin = h0
        h = h + apw_ref[first_power, 0] * h0 + apw_ref[first_power, 1] * h0s_ref[...]
    else:
        hin = shift(h, 1)
    y_ref[...] = y + lax.dot_general(hin.astype(BF16), mt_ref[...], (((1,), (1,)), ((), ())),
                                     preferred_element_type=F32)
    if nchunk == 1:
        hend_ref[...] = h
    else:
        hend_ref[...] = jnp.concatenate([h[(b + 1) * nchunk - 1:(b + 1) * nchunk, :] for b in range(nseq)], axis=0)


def _ssm_call(uf, r, nmat, mt, apw, h0=None, h0s=None, *, layer, nchunk, nsteps, first_power):
    g, rows, w = uf.shape
    assert nchunk & (nchunk - 1) == 0, "chunks per sequence must be a power of two"
    nseq = rows // nchunk
    npw = apw.shape[1]
    gspec = lambda a, b: pl.BlockSpec((None, a, b), lambda i: (i, 0, 0))
    lgspec = lambda a, b: pl.BlockSpec((None, None, a, b), lambda i: (layer, i, 0, 0))
    in_specs = [gspec(rows, w), lgspec(SSM_GROUP, w), lgspec(w, 4 * SSM_STATE), lgspec(w, 2 * SSM_STATE),
                pl.BlockSpec((None, npw, 3, None, 1, 2 * SSM_STATE), lambda i: (layer, 0, 0, i, 0, 0))]
    args = [uf, r, nmat, mt, apw]
    if h0 is not None:
        in_specs += [gspec(rows, 2 * SSM_STATE)] * 2
        args += [h0, h0s]
    return pl.pallas_call(
        functools.partial(_ssm_kernel, nsteps=nsteps, first_power=first_power, nchunk=nchunk, nseq=nseq,
                          has_h0=h0 is not None),
        grid=(g,),
        in_specs=in_specs,
        out_specs=[gspec(rows, w), gspec(nseq, 2 * SSM_STATE)],
        out_shape=[jax.ShapeDtypeStruct((g, rows, w), F32),
                   jax.ShapeDtypeStruct((g, nseq, 2 * SSM_STATE), F32)],
        scratch_shapes=[pltpu.VMEM((w, w), BF16)],
        compiler_params=_cparams("arbitrary"),
        name="ssm",
    )(*args)


def _softplus(z):
    return jnp.maximum(z, 0.0) + jnp.log1p(jnp.exp(-jnp.abs(z)))


def _lru_gates(xc, wg_ref, bg_ref, lam_ref):
    gl = _dot(xc.astype(BF16), wg_ref[...]) + bg_ref[...]
    r = jax.nn.sigmoid(gl[:, :D_LRU])
    gi = jax.nn.sigmoid(gl[:, D_LRU:])
    log_a = -LRU_C * r * _softplus(-lam_ref[...])
    a = jnp.exp(log_a)
    mult = jnp.sqrt(1.0 - a * a)
    return a, mult, gi


def _lru_prompt_kernel(xr_ref, yg_ref, cw_ref, cb_ref, wg_ref, bg_ref, lam_ref, o_ref, hl_ref,
                       xp_scr, hc_scr):
    t = pl.program_id(1)
    tl = xr_ref.shape[0]
    halo = SUBLANES

    @pl.when(t == 0)
    def _():
        xp_scr[0:halo, :] = jnp.zeros((halo, D_LRU), F32)
        hc_scr[...] = jnp.zeros((1, D_LRU), F32)

    xp_scr[halo:halo + tl, :] = xr_ref[...]
    xc = cb_ref[...]
    for j in range(LRU_CONV):
        off = halo - (LRU_CONV - 1) + j
        xc = xc + cw_ref[j:j + 1, :] * xp_scr[off:off + tl, :]
    a, mult, gi = _lru_gates(xc, wg_ref, bg_ref, lam_ref)
    row = lax.broadcasted_iota(jnp.int32, (tl, D_LRU), 0)
    mult = jnp.where((row == 0) & (t == 0), 1.0, mult)
    b = mult * gi * xc
    d = 1
    while d < tl:
        keep = row >= d
        a_sh = jnp.where(keep, pltpu.roll(a, d, axis=0), 1.0)
        b_sh = jnp.where(keep, pltpu.roll(b, d, axis=0), 0.0)
        b = a * b_sh + b
        a = a * a_sh
        d *= 2
    h = a * hc_scr[...] + b
    o_ref[...] = h * jax.nn.gelu(yg_ref[...])
    last = h[tl - 1:tl, :]
    hc_scr[...] = last
    hl_ref[...] = jnp.broadcast_to(last, (SUBLANES, D_LRU))
    xp_scr[0:halo, :] = xp_scr[tl:tl + halo, :]


def _lru_weight_specs(layer):
    return [_layer_spec((LRU_CONV, D_LRU), layer), _layer_spec((1, D_LRU), layer),
            _layer_spec((D_LRU, 2 * D_LRU), layer), _layer_spec((1, 2 * D_LRU), layer),
            _layer_spec((1, D_LRU), layer)]


def _lru_weight_args(w):
    return (w["lru_conv_w"], w["lru_conv_b"], w["lru_wg"], w["lru_bg"], w["lru_lambda"])


def _lru_prompt_call(xr, yg, w, *, layer, bsz, seq_len):
    tl = min(LRU_TILE, seq_len)
    nt = seq_len // tl
    row = pl.BlockSpec((tl, D_LRU), lambda b, t: (b * nt + t, 0))
    return pl.pallas_call(
        _lru_prompt_kernel,
        grid=(bsz, nt),
        in_specs=[row, row] + _lru_weight_specs(layer),
        out_specs=[row, pl.BlockSpec((None, SUBLANES, D_LRU), lambda b, t: (b, 0, 0))],
        out_shape=[jax.ShapeDtypeStruct((bsz * seq_len, D_LRU), F32),
                   jax.ShapeDtypeStruct((bsz, SUBLANES, D_LRU), F32)],
        scratch_shapes=[pltpu.VMEM((tl + 2 * SUBLANES, D_LRU), F32), pltpu.VMEM((1, D_LRU), F32)],
        compiler_params=_cparams("arbitrary", "arbitrary"),
        name="lru_prompt",
    )(xr, yg, *_lru_weight_args(w))


def _lru_sample_kernel(xr_ref, yg_ref, buf_ref, h0_ref, cw_ref, cb_ref, wg_ref, bg_ref, lam_ref,
                       o_ref, hl_ref, *, dseq):
    n = h0_ref.shape[0]
    xp = [buf_ref[j] for j in range(LRU_CONV - 1)] + [xr_ref[pl.ds(t * n, n), :] for t in range(dseq)]
    xcs = []
    for t in range(dseq):
        xc = cb_ref[...]
        for j in range(LRU_CONV):
            xc = xc + cw_ref[j:j + 1, :] * xp[t + j]
        xcs.append(xc)
    xc = jnp.concatenate(xcs, axis=0)
    a, mult, gi = _lru_gates(xc, wg_ref, bg_ref, lam_ref)
    b = mult * gi * xc
    h = h0_ref[...]
    for t in range(dseq):
        h = a[t * n:(t + 1) * n] * h + b[t * n:(t + 1) * n]
        o_ref[pl.ds(t * n, n), :] = h * jax.nn.gelu(yg_ref[pl.ds(t * n, n), :])
    hl_ref[...] = h


def _lru_sample_call(xr, yg, buf_tm, h0, w, *, layer, dseq):
    n = h0.shape[0]
    full = lambda a: pl.BlockSpec(a.shape, lambda i: (0,) * a.ndim)
    return pl.pallas_call(
        functools.partial(_lru_sample_kernel, dseq=dseq),
        grid=(1,),
        in_specs=[full(xr), full(yg), full(buf_tm), full(h0)] + _lru_weight_specs(layer),
        out_specs=[pl.BlockSpec((dseq * n, D_LRU), lambda i: (0, 0)), pl.BlockSpec((n, D_LRU), lambda i: (0, 0))],
        out_shape=[jax.ShapeDtypeStruct((dseq * n, D_LRU), F32), jax.ShapeDtypeStruct((n, D_LRU), F32)],
        compiler_params=_cparams("arbitrary"),
        name="lru_sample",
    )(xr, yg, buf_tm, h0, *_lru_weight_args(w))


def _post_kernel(x_ref, oa_ref, ys_ref, u_ref, ol_ref, g1_ref, sh2_ref, sc2_ref, g2_ref,
                 d_ref, wglu_ref, bglu_ref, on_ref, wo_ref, n2_ref, wup_ref, cw_ref, cb_ref, wdn_ref,
                 halo_in_ref, xo_ref, halo_out_ref, h2_scr, acc_scr, halo_scr, ys_scr,
                 *, per_token, row_shift, group_major):
    t = pl.program_id(1)
    tm = x_ref.shape[0]
    hrows = halo_in_ref.shape[0]

    @pl.when(t == 0)
    def _():
        halo_scr[...] = halo_in_ref[...]

    if group_major:
        _from_group_major(ys_ref, ys_scr, ys_ref.shape[1], SSM_CHUNK)
        ys = jnp.concatenate([ys_scr[h] for h in range(SSM_HALVES)], axis=1)
    else:
        ys = ys_ref[...]
    ys = ys + d_ref[...] * u_ref[...]
    gs = jax.nn.gelu(ys)
    o_ssm = gs * jax.nn.sigmoid(_dot(gs.astype(BF16), wglu_ref[...]) + bglu_ref[...])
    on = on_ref[...]
    c1 = D_ATTN
    c2 = c1 + D_SSM
    o = jnp.concatenate([_rms(oa_ref[...], on[:, :c1]), _rms(o_ssm, on[:, c1:c2]),
                         _rms(ol_ref[...], on[:, c2:])], axis=-1)
    x1 = x_ref[...] + _mod_rows(g1_ref, per_token, tm) * _dot(o.astype(BF16), wo_ref[...])
    h2 = _rms(x1, n2_ref[...]) * (1.0 + _mod_rows(sc2_ref, per_token, tm)) + _mod_rows(sh2_ref, per_token, tm)
    h2_scr[...] = h2.astype(BF16)
    rowh = lax.broadcasted_iota(jnp.int32, (hrows, FF_CHUNK), 0)
    for c in range(N_FF_CHUNKS):
        halves = []
        for gv in range(2):
            cols = slice(gv * D_FF + c * FF_CHUNK, gv * D_FF + (c + 1) * FF_CHUNK)
            msub = tm // FF_ROW_SPLIT
            up = jnp.concatenate([_dot(h2_scr[r * msub:(r + 1) * msub, :], wup_ref[:, cols])
                                  for r in range(FF_ROW_SPLIT)], axis=0)
            halo = halo_scr[:, cols]
            y = cb_ref[:, cols] + cw_ref[FFN_CONV - 1:FFN_CONV, cols] * up
            for back in range(1, FFN_CONV):
                sh = back * row_shift
                r = pltpu.roll(up, sh, axis=0)
                hr = pltpu.roll(halo, sh, axis=0) if sh % hrows else halo
                head = jnp.where(rowh < sh, hr, r[:hrows])
                shifted = jnp.concatenate([head, r[hrows:]], axis=0)
                j = FFN_CONV - 1 - back
                y = y + cw_ref[j:j + 1, cols] * shifted
            halo_scr[:, cols] = up[tm - hrows:, :]
            halves.append(y)
        act = (jax.nn.gelu(halves[0]) * halves[1]).astype(BF16)
        contrib = _dot(act, wdn_ref[c * FF_CHUNK:(c + 1) * FF_CHUNK, :])
        if c == 0:
            acc_scr[...] = contrib
        else:
            acc_scr[...] += contrib
    xo_ref[...] = x1 + _mod_rows(g2_ref, per_token, tm) * acc_scr[...]
    halo_out_ref[...] = halo_scr[...]


def _post_call(x, oa, ys, u, ol, mod, w, halo_in, *, layer, per_token, seq_len, row_shift, mod_seqs=None):
    nt = x.shape[0]
    tm = min(TOKEN_TILE, nt)
    tiles_per_seq = seq_len // tm
    nseq = nt // seq_len
    hrows = halo_in.shape[1]
    row = lambda wd: pl.BlockSpec((tm, wd), lambda s, t: (s * tiles_per_seq + t, 0))
    mspec = lambda piece: _mod_spec(layer, piece, per_token, lambda s, t: s)(mod_seqs)
    halo_spec = pl.BlockSpec((None, hrows, 2 * D_FF), lambda s, t: (s, 0, 0))
    group_major = ys.ndim == 3
    if group_major:
        ys_spec = pl.BlockSpec((N_SSM_GROUPS, tm // SSM_CHUNK, SSM_CHUNK * SSM_GROUP),
                               lambda s, t: (0, s * tiles_per_seq + t, 0))
    else:
        ys_spec = row(D_SSM)
    return pl.pallas_call(
        functools.partial(_post_kernel, per_token=per_token, row_shift=row_shift, group_major=group_major),
        grid=(nseq, tiles_per_seq),
        in_specs=[
            row(D_MODEL), row(D_ATTN), ys_spec, row(D_SSM), row(D_LRU),
            mspec(2), mspec(3), mspec(4), mspec(5),
            _layer_spec((1, D_SSM), layer), _layer_spec((D_SSM, D_SSM), layer), _layer_spec((1, D_SSM), layer),
            _layer_spec((1, D_MODEL), layer), _layer_spec((D_MODEL, D_MODEL), layer, True),
            _layer_spec((1, D_MODEL), layer),
            _layer_spec((D_MODEL, 2 * D_FF), layer, True),
            _layer_spec((FFN_CONV, 2 * D_FF), layer),
            _layer_spec((1, 2 * D_FF), layer),
            _layer_spec((D_FF, D_MODEL), layer, True),
            halo_spec,
        ],
        out_specs=[row(D_MODEL), halo_spec],
        out_shape=[jax.ShapeDtypeStruct((nt, D_MODEL), F32),
                   jax.ShapeDtypeStruct((nseq, hrows, 2 * D_FF), F32)],
        scratch_shapes=[pltpu.VMEM((tm, D_MODEL), BF16), pltpu.VMEM((tm, D_MODEL), F32),
                        pltpu.VMEM((hrows, 2 * D_FF), F32),
                        pltpu.VMEM((SSM_HALVES, tm, LANES), F32)],
        compiler_params=_cparams("arbitrary", "arbitrary"),
        name="post",
    )(x, oa, ys, u, ol, mod, mod, mod, mod, w["ssm_d"], w["ssm_w_glu"], w["ssm_b_glu"], w["out_norm"],
      w["w_o"], w["norm2"], w["w_up"], w["ffn_conv_w"], w["ffn_conv_b"], w["w_down"], halo_in)


def _block_diag(w):
    depth, nb, bs, _ = w.shape
    eye = jnp.eye(nb, dtype=w.dtype)
    return jnp.einsum("lhij,hk->lhikj", w, eye).reshape(depth, nb * bs, nb * bs)


def _prepare_weights(p):
    depth = p["w_in"].shape[0]
    row = lambda a: a.reshape(depth, 1, -1)
    return dict(
        norm1=row(p["norm1"]), norm2=row(p["norm2"]), out_norm=row(p["out_norm"]),
        w_in=p["w_in"].astype(BF16), w_o=p["w_o"].astype(BF16),
        q_norm=row(jnp.tile(p["q_norm"], (1, N_HEADS))), k_norm=row(jnp.tile(p["k_norm"], (1, N_KV_HEADS))),
        ssm_d=row(p["ssm_d"]), ssm_w_glu=p["ssm_w_glu"].astype(BF16), ssm_b_glu=row(p["ssm_b_glu"]),
        lru_conv_w=p["lru_conv_w"], lru_conv_b=row(p["lru_conv_b"]),
        lru_wg=jnp.concatenate([_block_diag(p["lru_w_a"]), _block_diag(p["lru_w_i"])], axis=-1).astype(BF16),
        lru_bg=row(jnp.concatenate([p["lru_b_a"], p["lru_b_i"]], axis=-1)),
        lru_lambda=row(p["lru_lambda"]),
        w_up=p["ffn_w_up"].astype(BF16), ffn_conv_w=p["ffn_conv_w"], ffn_conv_b=row(p["ffn_conv_b"]),
        w_down=p["ffn_w_down"].astype(BF16),
    )


def _prompt_layer(x, mod, w, seg, sinks, ssm_tabs, cos, sin, *, layer, bsz, seq_len, nsteps):
    nchunk = seq_len // SSM_CHUNK
    q, k, v, u, xr, yg, uf = _pre_call(x, mod, w, seg, cos, sin, layer=layer, per_token=False, seq_len=seq_len)
    oa = _attn_prompt_call(sinks, q, k, v, layer=layer, bsz=bsz, seq_len=seq_len)
    r_all, np_all, mtp_all, _, _, apw = ssm_tabs
    yf, hend = _ssm_call(uf, r_all, np_all, mtp_all, apw, layer=layer, nchunk=nchunk, nsteps=nsteps,
                         first_power=1)
    ol, hl = _lru_prompt_call(xr, yg, w, layer=layer, bsz=bsz, seq_len=seq_len)
    halo0 = jnp.zeros((bsz, SUBLANES, 2 * D_FF), F32)
    x_new, halo = _post_call(x, oa, yf, u, ol, mod, w, halo0, layer=layer, per_token=False, seq_len=seq_len,
                             row_shift=1)
    keep = min(WINDOW, seq_len)
    last = lambda a, nrows: a.reshape(bsz, seq_len, a.shape[-1])[:, seq_len - nrows:]
    hend_b = jnp.transpose(hend, (1, 0, 2))
    states = (last(k, keep).reshape(bsz, keep, N_KV_HEADS, HEAD_DIM),
              last(v, keep).reshape(bsz, keep, N_KV_HEADS, HEAD_DIM),
              hend_b[..., :SSM_STATE], hend_b[..., SSM_STATE:],
              hl[:, 0, :],
              last(xr, LRU_CONV - 1),
              halo[:, SUBLANES - (FFN_CONV - 1):, :])
    return x_new, states


def _sample_layer(x, mod, w, seg, sinks, ssm_tabs, cos, sin, st, *, layer, n, dseq):
    g = N_SSM_GROUPS
    ck, cv, s_re, s_im, lru_h, lru_conv, ffn_conv = st
    nt = n * dseq
    q, k, v, u, xr, yg = _pre_call(x, mod, w, seg, cos, sin, layer=layer, per_token=True, seq_len=nt, nseq=n)
    q_sm = jnp.transpose(q.reshape(dseq, n, N_KV_HEADS, KV_GROUP, HEAD_DIM), (1, 2, 0, 3, 4))
    q_sm = q_sm.reshape(n, N_KV_HEADS, dseq * KV_GROUP, HEAD_DIM)
    kn = jnp.transpose(k.reshape(dseq, n, D_KV), (1, 0, 2))
    vn = jnp.transpose(v.reshape(dseq, n, D_KV), (1, 0, 2))
    wbuf = ck.shape[1]
    o_sm, sk, sv = _attn_sample_call(sinks, q_sm, kn, vn, ck.reshape(n, wbuf, D_KV), cv.reshape(n, wbuf, D_KV),
                                     layer=layer, dseq=dseq)
    oa = jnp.transpose(o_sm.reshape(n, N_KV_HEADS, dseq, KV_GROUP, HEAD_DIM), (2, 0, 1, 3, 4)).reshape(nt, D_ATTN)
    r_all, _, _, ns_all, mts_all, apw = ssm_tabs
    wpad = LANES - dseq * SSM_GROUP
    uf = jnp.transpose(u.reshape(dseq, n, g, SSM_GROUP), (2, 1, 0, 3)).reshape(g, n, dseq * SSM_GROUP)
    uf = jnp.pad(uf, ((0, 0), (0, 0), (0, wpad))).astype(BF16)
    h_re = jnp.transpose(s_re, (1, 0, 2))
    h_im = jnp.transpose(s_im, (1, 0, 2))
    h0 = jnp.concatenate([h_re, h_im], axis=-1)
    h0s = jnp.concatenate([h_im, h_re], axis=-1)
    yf, hend = _ssm_call(uf, r_all, ns_all, mts_all, apw, h0, h0s, layer=layer, nchunk=1, nsteps=0, first_power=0)
    yf = yf[:, :, :dseq * SSM_GROUP]
    ys = jnp.transpose(yf.reshape(g, n, dseq, SSM_GROUP), (2, 1, 0, 3)).reshape(nt, D_SSM)
    ol, hl = _lru_sample_call(xr, yg, jnp.transpose(lru_conv, (1, 0, 2)), lru_h, w, layer=layer, dseq=dseq)
    nconv = FFN_CONV - 1
    halo0 = jnp.transpose(ffn_conv, (1, 0, 2)).reshape(1, nconv * n, 2 * D_FF)
    x_new, halo = _post_call(x, oa, ys, u, ol, mod, w, halo0, layer=layer, per_token=True, seq_len=nt,
                             row_shift=n, mod_seqs=n)
    hend_b = jnp.transpose(hend, (1, 0, 2))
    xr_tm = xr.reshape(dseq, n, D_LRU)
    lru_conv_all = jnp.concatenate([jnp.transpose(lru_conv, (1, 0, 2)), xr_tm], axis=0)
    states = (sk.reshape(n, wbuf, N_KV_HEADS, HEAD_DIM), sv.reshape(n, wbuf, N_KV_HEADS, HEAD_DIM),
              hend_b[..., :SSM_STATE], hend_b[..., SSM_STATE:],
              hl,
              jnp.transpose(lru_conv_all[dseq:], (1, 0, 2)),
              jnp.transpose(halo.reshape(nconv, n, 2 * D_FF), (1, 0, 2)))
    return x_new, states


def kernel(x_prompt, x_sample, cache_k, cache_v, state_ssm_re, state_ssm_im, state_lru_h, state_lru_conv,
           state_ffn_conv, c_prompt, c_sample, w_ada, b_ada, norm1, w_in, q_norm, k_norm, sinks, ssm_a_re,
           ssm_a_im, ssm_b_re, ssm_b_im, ssm_c_re, ssm_c_im, ssm_d, ssm_log_dt, ssm_w_glu, ssm_b_glu,
           lru_conv_w, lru_conv_b, lru_w_a, lru_b_a, lru_w_i, lru_b_i, lru_lambda, out_norm, w_o, norm2,
           ffn_w_up, ffn_conv_w, ffn_conv_b, ffn_w_down):
    bsz, seq_len = x_prompt.shape[:2]
    n, dseq = x_sample.shape[:2]
    depth = w_in.shape[0]
    params = dict(norm1=norm1, w_in=w_in, q_norm=q_norm, k_norm=k_norm, ssm_d=ssm_d,
                  ssm_w_glu=ssm_w_glu, ssm_b_glu=ssm_b_glu, lru_conv_w=lru_conv_w, lru_conv_b=lru_conv_b,
                  lru_w_a=lru_w_a, lru_b_a=lru_b_a, lru_w_i=lru_w_i, lru_b_i=lru_b_i, lru_lambda=lru_lambda,
                  out_norm=out_norm, w_o=w_o, norm2=norm2, ffn_w_up=ffn_w_up, ffn_conv_w=ffn_conv_w,
                  ffn_conv_b=ffn_conv_b, ffn_w_down=ffn_w_down)
    weights = _prepare_weights(params)
    seg = (jnp.arange(D_ATTN)[:, None] // HEAD_DIM == jnp.arange(D_ATTN)[None, :] // HEAD_DIM).astype(BF16)

    rows = n + bsz
    pad = (-rows) % SUBLANES
    c_all = jnp.concatenate([c_sample, c_prompt, jnp.zeros((pad, D_MODEL), F32)], axis=0)
    mod_all = _ada_call(c_all, w_ada, b_ada)
    mod_p = jnp.broadcast_to(mod_all[:, :, n:n + bsz, None, :], (depth, N_MOD, bsz, SUBLANES, D_MODEL))

    cos_p, sin_p = _rope_call(jnp.arange(seq_len, dtype=F32))
    cos_s, sin_s = _rope_call(jnp.repeat(PAST_LEN + jnp.arange(dseq, dtype=F32), n))

    nchunk = seq_len // SSM_CHUNK
    nsteps = max(nchunk - 1, 0).bit_length()
    powers = (dseq,) + tuple(SSM_CHUNK * (1 << i) for i in range(nsteps))
    assert dseq * SSM_GROUP <= LANES
    ssm_tabs = _ssm_prep_call(ssm_a_re, ssm_a_im, ssm_log_dt, ssm_b_re, ssm_b_im, ssm_c_re, ssm_c_im,
                              lc=SSM_CHUNK, dseq=dseq, powers=powers)

    xp = x_prompt.reshape(bsz * seq_len, D_MODEL)
    xs = jnp.transpose(x_sample, (1, 0, 2)).reshape(dseq * n, D_MODEL)
    new_p, new_s = [], []
    for i in range(depth):
        xp, st_p = _prompt_layer(xp, mod_p, weights, seg, sinks, ssm_tabs, cos_p, sin_p, layer=i, bsz=bsz,
                                 seq_len=seq_len, nsteps=nsteps)
        st_in = (cache_k[i], cache_v[i], state_ssm_re[i], state_ssm_im[i], state_lru_h[i], state_lru_conv[i],
                 state_ffn_conv[i])
        xs, st_s = _sample_layer(xs, mod_all, weights, seg, sinks, ssm_tabs, cos_s, sin_s, st_in, layer=i, n=n,
                                 dseq=dseq)
        new_p.append(st_p)
        new_s.append(st_s)
    pk, pv, p_re, p_im, p_lh, p_lc, p_fc = [jnp.stack(s) for s in zip(*new_p)]
    sk, sv, s_re, s_im, s_lh, s_lc, s_fc = [jnp.stack(s) for s in zip(*new_s)]
    y_p = xp.reshape(bsz, seq_len, D_MODEL)
    y_s = jnp.transpose(xs.reshape(dseq, n, D_MODEL), (1, 0, 2))
    return (y_p, y_s, pk, pv, p_re, p_im, p_lh, p_lc, p_fc, sk, sv, s_re, s_im, s_lh, s_lc, s_fc)
```

```python
import functools

import jax
import jax.numpy as jnp
from jax import lax
from jax.experimental import pallas as pl
from jax.experimental.pallas import tpu as pltpu

F32 = jnp.float32
BF16 = jnp.bfloat16

D_MODEL = 1024
HEAD_DIM = 64
N_HEADS = 8
N_KV_HEADS = 2
KV_GROUP = N_HEADS // N_KV_HEADS
D_ATTN = N_HEADS * HEAD_DIM
D_KV = N_KV_HEADS * HEAD_DIM
WINDOW = 128
ROPE_THETA = 10000.0
PAST_LEN = 8192
D_SSM = 256
SSM_GROUP = 16
N_SSM_GROUPS = 16
SSM_STATE = 64
D_LRU = 256
N_LRU_BLOCKS = 4
LRU_BLOCK = 64
LRU_CONV = 4
LRU_C = 8.0
D_FF = 2816
FFN_CONV = 3
D_IN = D_ATTN + 2 * D_KV + D_SSM + 2 * D_LRU
N_MOD = 6
EPS = 1e-6

SUBLANES = 8
LANES = 128
VMEM_LIMIT_BYTES = 56 * 1024 * 1024

TOKEN_TILE = 512
ATTN_BLOCK = WINDOW
ATTN_SUBBLOCKS = 4
SSM_CHUNK = 32
LRU_TILE = 512
FF_CHUNK = 256
N_FF_CHUNKS = D_FF // FF_CHUNK
FF_ROW_SPLIT = 1
POST_TILE = 256
SAMPLE_ATTN_SEQS = 16
SSM_HALVES = D_SSM // LANES
GROUPS_PER_HALF = LANES // SSM_GROUP


def _cparams(*sem):
    return pltpu.CompilerParams(dimension_semantics=sem, vmem_limit_bytes=VMEM_LIMIT_BYTES)


def _dot(a, b):
    return jnp.dot(a, b, preferred_element_type=F32)


def _split_bf16(x):
    hi = x.astype(BF16)
    lo = (x - hi.astype(F32)).astype(BF16)
    return hi, lo


def _layer_spec(shape, layer, single=False):
    nd = len(shape)
    idx = lambda *_: (layer,) + (0,) * nd
    if single:
        return pl.BlockSpec((None,) + tuple(shape), idx, pipeline_mode=pl.Buffered(1))
    return pl.BlockSpec((None,) + tuple(shape), idx)


def _ada_kernel(c_ref, w_ref, b_ref, o_ref):
    c = c_ref[...]
    s = jax.nn.silu(c).astype(BF16)
    o_ref[...] = _dot(s, w_ref[...].astype(BF16)) + b_ref[...]


def _ada_call(c_all, w_ada, b_ada):
    depth = w_ada.shape[0]
    rows = c_all.shape[0]
    return pl.pallas_call(
        _ada_kernel,
        grid=(depth, N_MOD),
        in_specs=[
            pl.BlockSpec((rows, D_MODEL), lambda l, j: (0, 0)),
            pl.BlockSpec((None, D_MODEL, D_MODEL), lambda l, j: (l, 0, j)),
            pl.BlockSpec((None, 1, D_MODEL), lambda l, j: (l, 0, j)),
        ],
        out_specs=pl.BlockSpec((None, None, rows, D_MODEL), lambda l, j: (l, j, 0, 0)),
        out_shape=jax.ShapeDtypeStruct((depth, N_MOD, rows, D_MODEL), F32),
        compiler_params=_cparams("arbitrary", "arbitrary"),
        name="ada",
    )(c_all, w_ada, b_ada.reshape(depth, 1, N_MOD * D_MODEL))


def _rope_kernel(pos_ref, cos_ref, sin_ref):
    pos = pos_ref[...]
    lane = lax.broadcasted_iota(jnp.int32, pos.shape, 1)
    half = HEAD_DIM // 2
    j = (lane & (half - 1)).astype(F32)
    inv = ROPE_THETA ** (-j / half)
    ang = pos * inv
    cos_ref[...] = jnp.cos(ang)
    s = jnp.sin(ang)
    sin_ref[...] = jnp.where((lane & (HEAD_DIM - 1)) < half, -s, s)


def _rope_call(pos_rows):
    t = pos_rows.shape[0]
    pos_b = jnp.broadcast_to(pos_rows[:, None], (t, LANES))
    return pl.pallas_call(
        _rope_kernel,
        out_shape=(jax.ShapeDtypeStruct((t, LANES), F32), jax.ShapeDtypeStruct((t, LANES), F32)),
        name="rope",
    )(pos_b)


def _mod_rows(ref, per_token, tm):
    if not per_token:
        return ref[0:1, :]
    m = ref[...]
    reps = tm // m.shape[0]
    return jnp.concatenate([m] * reps, axis=0) if reps > 1 else m


def _mod_spec(layer, piece, per_token, seq_of_step):
    if per_token:
        return lambda n: pl.BlockSpec((None, None, n, D_MODEL), lambda *g: (layer, piece, 0, 0))
    return lambda n: pl.BlockSpec((None, None, None, SUBLANES, D_MODEL),
                                  lambda *g: (layer, piece, seq_of_step(*g), 0, 0))


def _rms(x, gain):
    return x * lax.rsqrt(jnp.mean(x * x, axis=-1, keepdims=True) + EPS) * gain


def _head_rms(t, seg, gain):
    hi, lo = _split_bf16(t * t)
    ss = _dot(hi, seg) + _dot(lo, seg)
    return t * lax.rsqrt(ss * (1.0 / HEAD_DIM) + EPS) * gain


def _rope(t, cos, sin):
    width = t.shape[1]
    reps = width // LANES
    if reps > 1:
        cos = jnp.concatenate([cos] * reps, axis=1)
        sin = jnp.concatenate([sin] * reps, axis=1)
    lane = lax.broadcasted_iota(jnp.int32, t.shape, 1)
    half = HEAD_DIM // 2
    up = pltpu.roll(t, width - half, axis=1)
    dn = pltpu.roll(t, half, axis=1)
    rot = jnp.where((lane & (HEAD_DIM - 1)) < half, up, dn)
    return t * cos + rot * sin


def _unit_transpose8(vs):
    lane = lax.broadcasted_iota(jnp.int32, vs[0].shape, 1)
    unit = lane >> 4
    for b in range(3):
        d = 1 << b
        bit = (unit >> b) & 1
        new = list(vs)
        for i in range(8):
            if (i >> b) & 1 == 0:
                lo, hi = vs[i], vs[i + d]
                new[i] = jnp.where(bit == 0, lo, pltpu.roll(hi, d * SSM_GROUP, axis=1))
                new[i + d] = jnp.where(bit == 1, hi, pltpu.roll(lo, LANES - d * SSM_GROUP, axis=1))
        vs = new
    return vs


def _to_group_major(u_ref, nk, lc):
    outs = [[None] * (lc // 8) for _ in range(N_SSM_GROUPS)]
    for h in range(SSM_HALVES):
        for tb in range(lc // 8):
            vs = [u_ref[h, pl.ds(tb * 8 + tp, nk, stride=lc), :] for tp in range(8)]
            ws = _unit_transpose8(vs)
            for gp in range(GROUPS_PER_HALF):
                outs[h * GROUPS_PER_HALF + gp][tb] = ws[gp]
    return [jnp.concatenate(o, axis=1) for o in outs]


def _from_group_major(yf_ref, ys_ref, nk, lc):
    for h in range(SSM_HALVES):
        for tb in range(lc // 8):
            ws = [yf_ref[h * GROUPS_PER_HALF + gp, :, tb * LANES:(tb + 1) * LANES] for gp in range(GROUPS_PER_HALF)]
            vs = _unit_transpose8(ws)
            for tp in range(8):
                ys_ref[h, pl.ds(tb * 8 + tp, nk, stride=lc), :] = vs[tp]


def _pre_kernel(x_ref, sh_ref, sc_ref, n1_ref, w_ref, qn_ref, kn_ref, seg_ref, cos_ref, sin_ref,
                q_ref, k_ref, v_ref, u_ref, xr_ref, yg_ref, *maybe_uf_ref, per_token):
    x = x_ref[...]
    tm = x.shape[0]
    h = _rms(x, n1_ref[...]) * (1.0 + _mod_rows(sc_ref, per_token, tm)) + _mod_rows(sh_ref, per_token, tm)
    c1 = D_ATTN
    c2 = c1 + D_KV
    c3 = c2 + D_KV
    c4 = c3 + D_SSM
    c5 = c4 + D_LRU
    hb = h.astype(BF16)
    proj_q = _dot(hb, w_ref[:, :c1])
    proj = _dot(hb, w_ref[:, c1:])
    c2, c3, c4, c5 = c2 - c1, c3 - c1, c4 - c1, c5 - c1
    cos = cos_ref[...]
    sin = sin_ref[...]
    seg = seg_ref[...]
    qn = _head_rms(proj_q, seg, qn_ref[...])
    kn = _head_rms(proj[:, :c2], seg[:D_KV, :D_KV], kn_ref[...])
    v_ref[...] = proj[:, c2:c3]
    u_ref[...] = proj[:, c3:c4]
    xr_ref[...] = proj[:, c4:c5]
    yg_ref[...] = proj[:, c5:]
    if maybe_uf_ref:
        uf_ref, u_scr = maybe_uf_ref
        for hh in range(SSM_HALVES):
            u_scr[hh] = proj[:, c3 + hh * LANES:c3 + (hh + 1) * LANES]
        groups = _to_group_major(u_scr, uf_ref.shape[1], SSM_CHUNK)
        for g in range(N_SSM_GROUPS):
            uf_ref[g] = groups[g].astype(BF16)
    q_ref[...] = _rope(qn, cos, sin).astype(BF16)
    k_ref[...] = _rope(kn, cos, sin)


def _pre_call(x, mod, w, seg, cos, sin, *, layer, per_token, seq_len, nseq=None):
    nt = x.shape[0]
    tm = min(TOKEN_TILE, nt)
    tiles_per_seq = seq_len // tm
    if per_token:
        tab_spec = pl.BlockSpec((tm, LANES), lambda i: (i, 0))
    else:
        tab_spec = pl.BlockSpec((tm, LANES), lambda i: (i % tiles_per_seq, 0))
    mspec = lambda piece: _mod_spec(layer, piece, per_token, lambda i: i // tiles_per_seq)(nseq)
    row = lambda wd: pl.BlockSpec((tm, wd), lambda i: (i, 0))
    widths = (D_ATTN, D_KV, D_KV, D_SSM, D_LRU, D_LRU)
    dtypes = (BF16, F32, F32, F32, F32, F32)
    out_specs = [row(wd) for wd in widths]
    out_shape = [jax.ShapeDtypeStruct((nt, wd), d) for wd, d in zip(widths, dtypes)]
    scratch = []
    if not per_token:
        nk = tm // SSM_CHUNK
        wf = SSM_CHUNK * SSM_GROUP
        out_specs.append(pl.BlockSpec((N_SSM_GROUPS, nk, wf), lambda i: (0, i, 0)))
        out_shape.append(jax.ShapeDtypeStruct((N_SSM_GROUPS, nt // SSM_CHUNK, wf), BF16))
        scratch.append(pltpu.VMEM((SSM_HALVES, tm, LANES), F32))
    return pl.pallas_call(
        functools.partial(_pre_kernel, per_token=per_token),
        grid=(nt // tm,),
        in_specs=[
            row(D_MODEL), mspec(0), mspec(1),
            _layer_spec((1, D_MODEL), layer),
            _layer_spec((D_MODEL, D_IN), layer, True),
            _layer_spec((1, D_ATTN), layer),
            _layer_spec((1, D_KV), layer),
            pl.BlockSpec((D_ATTN, D_ATTN), lambda i: (0, 0)),
            tab_spec, tab_spec,
        ],
        out_specs=out_specs,
        out_shape=out_shape,
        scratch_shapes=scratch,
        compiler_params=_cparams("arbitrary"),
        name="pre",
    )(x, mod, mod, w["norm1"], w["w_in"], w["q_norm"], w["k_norm"], seg, cos, sin)


def _attn_prompt_kernel(sink_ref, q_ref, kc_ref, kp_ref, vc_ref, vp_ref, o_ref, *, layer, nsub):
    assert (KV_GROUP, N_KV_HEADS, 2 * HEAD_DIM) == (4, 2, LANES)
    i = pl.program_id(1)
    bq = ATTN_BLOCK
    kcat = jnp.concatenate([kp_ref[...], kc_ref[...]], axis=0).astype(BF16)
    vcat = jnp.concatenate([vp_ref[...], vc_ref[...]], axis=0).astype(BF16)
    low = lax.broadcasted_iota(jnp.int32, kcat.shape, 1) < HEAD_DIM

    def half_placed(x):
        zero = jnp.zeros_like(x)
        h0_lo = jnp.where(low, x, zero)
        h1_hi = jnp.where(low, zero, x)
        return [[h0_lo, pltpu.roll(h0_lo, HEAD_DIM, axis=1)], [pltpu.roll(h1_hi, HEAD_DIM, axis=1), h1_hi]]

    kz = half_placed(kcat)
    vz = half_placed(vcat)
    qi = lax.broadcasted_iota(jnp.int32, (bq, 2 * bq), 0)
    si = lax.broadcasted_iota(jnp.int32, (bq, 2 * bq), 1)
    diff = qi + bq - si
    in_window = (diff >= 0) & (diff < WINDOW)
    first_pair = lax.broadcasted_iota(jnp.int32, (2 * bq, 1), 0) < bq
    heads = [(h, e) for h in range(N_KV_HEADS) for e in range(2)]

    def score_block(j):
        band = slice(j * bq, (j + 2) * bq)
        qrows = slice(j * bq, (j + 1) * bq)
        scores = []
        for h, e in heads:
            qh = jnp.concatenate([q_ref[qrows, (2 * h + p) * LANES:(2 * h + p + 1) * LANES] for p in range(2)],
                                 axis=0)
            s = lax.dot_general(qh, kz[h][e][band], (((1,), (1,)), ((), ())), preferred_element_type=F32)
            scores.append(s * (HEAD_DIM ** -0.5))
        return scores

    nxt = score_block(0)
    for j in range(nsub):
        scores = nxt
        if j + 1 < nsub:
            nxt = score_block(j + 1)
        if j == 0:
            allowed = in_window & ((si >= bq) | (i > 0))
        else:
            allowed = in_window
        allowed = jnp.concatenate([allowed, allowed], axis=0)
        band = slice(j * bq, (j + 2) * bq)
        qrows = slice(j * bq, (j + 1) * bq)
        probs = []
        for (h, e), s in zip(heads, scores):
            s = jnp.where(allowed, s, -jnp.inf)
            sink = jnp.where(first_pair, sink_ref[layer, KV_GROUP * h + e], sink_ref[layer, KV_GROUP * h + 2 + e])
            m = jnp.maximum(jnp.max(s, axis=-1, keepdims=True), sink)
            p = jnp.exp(s - m)
            denom = jnp.sum(p, axis=-1, keepdims=True) + jnp.exp(sink - m)
            probs.append((p.astype(BF16), denom))
        outs = [_dot(p, vz[h][e][band]) / denom for (h, e), (p, denom) in zip(heads, probs)]
        for h in range(N_KV_HEADS):
            o_pair = outs[2 * h] + outs[2 * h + 1]
            o_ref[qrows, (2 * h) * LANES:(2 * h + 1) * LANES] = o_pair[:bq]
            o_ref[qrows, (2 * h + 1) * LANES:(2 * h + 2) * LANES] = o_pair[bq:]


def _attn_prompt_call(sinks, q, k, v, *, layer, bsz, seq_len):
    nsub = min(ATTN_SUBBLOCKS, seq_len // ATTN_BLOCK)
    bq = nsub * ATTN_BLOCK
    nb = seq_len // bq
    nb_small = seq_len // ATTN_BLOCK
    cur = lambda w: pl.BlockSpec((bq, w), lambda b, i: (b * nb + i, 0))
    prev = lambda w: pl.BlockSpec((ATTN_BLOCK, w), lambda b, i: (b * nb_small + jnp.maximum(i * nsub - 1, 0), 0))
    return pl.pallas_call(
        functools.partial(_attn_prompt_kernel, layer=layer, nsub=nsub),
        grid=(bsz, nb),
        in_specs=[pl.BlockSpec(memory_space=pltpu.SMEM), cur(D_ATTN), cur(D_KV), prev(D_KV), cur(D_KV),
                  prev(D_KV)],
        out_specs=cur(D_ATTN),
        out_shape=jax.ShapeDtypeStruct((bsz * seq_len, D_ATTN), F32),
        compiler_params=_cparams("arbitrary", "arbitrary"),
        name="attn_prompt",
    )(sinks, q, k, k, v, v)


def _attn_sample_kernel(sink_ref, q_ref, kn_ref, vn_ref, ck_ref, cv_ref, o_ref, sk_ref, sv_ref, *, layer, dseq):
    kk = jnp.concatenate([ck_ref[...], kn_ref[...]], axis=1)
    vv = jnp.concatenate([cv_ref[...], vn_ref[...]], axis=1)
    wbuf = ck_ref.shape[1]
    sk_ref[...] = kk[:, dseq:, :]
    sv_ref[...] = vv[:, dseq:, :]
    nq = dseq * KV_GROUP
    nk = wbuf + dseq
    qrow = lax.broadcasted_iota(jnp.int32, (nq, nk), 0)
    j = lax.broadcasted_iota(jnp.int32, (nq, nk), 1)
    diff = (qrow >> 2) + wbuf - j
    allowed = ((diff >= 0) & (diff < WINDOW))[None]
    g_of_row = lax.broadcasted_iota(jnp.int32, (nq, 1), 0) & (KV_GROUP - 1)
    kkb = kk.astype(BF16)
    vvb = vv.astype(BF16)
    for h in range(N_KV_HEADS):
        kh = kkb[:, :, h * HEAD_DIM:(h + 1) * HEAD_DIM]
        vh = vvb[:, :, h * HEAD_DIM:(h + 1) * HEAD_DIM]
        qh = q_ref[:, h]
        s = jnp.einsum("sqd,skd->sqk", qh, kh, preferred_element_type=F32) * (HEAD_DIM ** -0.5)
        s = jnp.where(allowed, s, -jnp.inf)
        sink = jnp.full((nq, 1), sink_ref[layer, h * KV_GROUP], F32)
        for g in range(1, KV_GROUP):
            sink = jnp.where(g_of_row == g, sink_ref[layer, h * KV_GROUP + g], sink)
        sink = sink[None]
        m = jnp.maximum(jnp.max(s, axis=-1, keepdims=True), sink)
        p = jnp.exp(s - m)
        denom = jnp.sum(p, axis=-1, keepdims=True) + jnp.exp(sink - m)
        o = jnp.einsum("sqk,skd->sqd", p.astype(BF16), vh, preferred_element_type=F32) / denom
        o_ref[:, h] = o


def _attn_sample_call(sinks, q, kn, vn, ck, cv, *, layer, dseq):
    n, wbuf = ck.shape[0], ck.shape[1]
    sb = min(SAMPLE_ATTN_SEQS, n)
    nq = dseq * KV_GROUP
    qspec = pl.BlockSpec((sb, N_KV_HEADS, nq, HEAD_DIM), lambda i: (i, 0, 0, 0))
    nspec = pl.BlockSpec((sb, dseq, D_KV), lambda i: (i, 0, 0))
    cspec = pl.BlockSpec((sb, wbuf, D_KV), lambda i: (i, 0, 0))
    return pl.pallas_call(
        functools.partial(_attn_sample_kernel, layer=layer, dseq=dseq),
        grid=(n // sb,),
        in_specs=[pl.BlockSpec(memory_space=pltpu.SMEM), qspec, nspec, nspec, cspec, cspec],
        out_specs=[qspec, cspec, cspec],
        out_shape=[jax.ShapeDtypeStruct((n, N_KV_HEADS, nq, HEAD_DIM), F32),
                   jax.ShapeDtypeStruct((n, wbuf, D_KV), F32),
                   jax.ShapeDtypeStruct((n, wbuf, D_KV), F32)],
        compiler_params=_cparams("arbitrary"),
        name="attn_sample",
    )(sinks, q, kn, vn, ck, cv)


def _bdot3(a, b):
    dn = (((2,), (2,)), ((0,), (0,)))
    ah, al = _split_bf16(a)
    bh, bl = _split_bf16(b)
    d = lambda x, y: lax.dot_general(x, y, dn, preferred_element_type=F32)
    return d(ah, bh) + d(ah, bl) + d(al, bh)


def _ssm_prep_kernel(are_ref, aim_ref, ldt_ref, bre_ref, bim_ref, cre_ref, cim_ref,
                     r_ref, np_ref, mtp_ref, ns_ref, mts_ref, apw_ref, *, lc, dseq, powers):
    c = SSM_GROUP
    a_re = are_ref[...]
    a_im = aim_ref[...]
    dt = jnp.exp(ldt_ref[...])
    zr = a_re * dt
    zi = a_im * dt

    def a_pow(j):
        mag = jnp.exp(zr * j)
        return mag * jnp.cos(zi * j), mag * jnp.sin(zi * j)

    abr, abi = a_pow(1.0)
    xr = abr - 1.0
    den = a_re * a_re + a_im * a_im
    coef_r = (xr * a_re + abi * a_im) / den
    coef_i = (abi * a_re - xr * a_im) / den
    btr = bre_ref[...]
    bti = bim_ref[...]
    bbr = coef_r * btr - coef_i * bti
    bbi = coef_r * bti + coef_i * btr
    cre = cre_ref[...]
    cim = cim_ref[...]
    ns_ref[...] = jnp.zeros_like(ns_ref)
    mts_ref[...] = jnp.zeros_like(mts_ref)
    cars, cais = [], []
    for j in range(lc + 1):
        er, ei = a_pow(float(j))
        car = cre * er - cim * ei
        cai = cre * ei + cim * er
        if j < lc:
            cars.append(car)
            cais.append(cai)
            nbr = er * bbr - ei * bbi
            nbi = er * bbi + ei * bbr
            ncat = jnp.concatenate([nbr, nbi, nbi, nbr], axis=-1).astype(BF16)
            s = lc - 1 - j
            np_ref[:, s * c:(s + 1) * c, :] = ncat
            if j < dseq:
                s = dseq - 1 - j
                ns_ref[:, s * c:(s + 1) * c, :] = ncat
        if j >= 1:
            mcat = jnp.concatenate([car, -cai], axis=-1).astype(BF16)
            mtp_ref[:, (j - 1) * c:j * c, :] = mcat
            if j <= dseq:
                mts_ref[:, (j - 1) * c:j * c, :] = mcat
    ca_r = jnp.concatenate(cars, axis=1)
    ca_i = jnp.concatenate(cais, axis=1)
    r_ref[...] = _bdot3(bbr, ca_r) - _bdot3(bbi, ca_i)
    for idx, pw in enumerate(powers):
        er, ei = a_pow(float(pw))
        apw_ref[idx, 0] = jnp.concatenate([er, er], axis=-1)
        apw_ref[idx, 1] = jnp.concatenate([-ei, ei], axis=-1)
        apw_ref[idx, 2] = jnp.concatenate([ei, -ei], axis=-1)


def _ssm_prep_call(a_re, a_im, log_dt, b_re, b_im, c_re, c_im, *, lc, dseq, powers):
    depth = a_re.shape[0]
    g, p, c = N_SSM_GROUPS, SSM_STATE, SSM_GROUP
    npw = len(powers)
    a_spec = pl.BlockSpec((None, g, 1, p), lambda l: (l, 0, 0, 0))
    m_spec = pl.BlockSpec((None, g, c, p), lambda l: (l, 0, 0, 0))
    out4 = lambda a, b: pl.BlockSpec((None, g, a, b), lambda l: (l, 0, 0, 0))
    shape4 = lambda a, b, d: jax.ShapeDtypeStruct((depth, g, a, b), d)
    ws = LANES
    return pl.pallas_call(
        functools.partial(_ssm_prep_kernel, lc=lc, dseq=dseq, powers=powers),
        grid=(depth,),
        in_specs=[a_spec, a_spec, a_spec, m_spec, m_spec, m_spec, m_spec],
        out_specs=[out4(c, lc * c), out4(lc * c, 4 * p), out4(lc * c, 2 * p), out4(ws, 4 * p), out4(ws, 2 * p),
                   pl.BlockSpec((None, npw, 3, g, 1, 2 * p), lambda l: (l, 0, 0, 0, 0, 0))],
        out_shape=[shape4(c, lc * c, F32), shape4(lc * c, 4 * p, BF16), shape4(lc * c, 2 * p, BF16),
                   shape4(ws, 4 * p, BF16), shape4(ws, 2 * p, BF16),
                   jax.ShapeDtypeStruct((depth, npw, 3, g, 1, 2 * p), F32)],
        compiler_params=_cparams("arbitrary"),
        name="ssm_prep",
    )(a_re.reshape(depth, g, 1, p), a_im.reshape(depth, g, 1, p),
      jnp.broadcast_to(log_dt[:, :, None, None], (depth, g, 1, p)),
      jnp.swapaxes(b_re, -1, -2), jnp.swapaxes(b_im, -1, -2), c_re, c_im)


def _ssm_kernel(*refs, nsteps, first_power, nchunk, nseq, has_h0):
    if has_h0:
        u_ref, r_ref, n_ref, mt_ref, apw_ref, h0_ref, h0s_ref, y_ref, hend_ref, toe_scr = refs
    else:
        u_ref, r_ref, n_ref, mt_ref, apw_ref, y_ref, hend_ref, toe_scr = refs
    w = u_ref.shape[1]
    c = SSM_GROUP
    r = r_ref[...]
    lane = lax.broadcasted_iota(jnp.int32, r.shape, 1)
    toe_scr[0:c, :] = r.astype(BF16)
    for s in range(1, w // c):
        toe_scr[s * c:(s + 1) * c, :] = jnp.where(lane >= s * c, pltpu.roll(r, s * c, axis=1), 0.0).astype(BF16)
    uf = u_ref[...]
    y = _dot(uf, toe_scr[...])
    st = _dot(uf, n_ref[...])
    ws = 2 * SSM_STATE
    h = st[:, :ws]
    hs = st[:, ws:]
    kidx = lax.broadcasted_iota(jnp.int32, h.shape, 0) & (nchunk - 1)
    shift = lambda x, d: jnp.where(kidx >= d, pltpu.roll(x, d, axis=0), 0.0)
    for i in range(nsteps):
        d = 1 << i
        pw = first_power + i
        a1, a2, a3 = apw_ref[pw, 0], apw_ref[pw, 1], apw_ref[pw, 2]
        hd = shift(h, d)
        hsd = shift(hs, d)
        h, hs = h + a1 * hd + a2 * hsd, hs + a1 * hsd + a3 * hd
    if has_h0:
        h0 = h0_ref[...]
        hin = h0
        h = h + apw_ref[first_power, 0] * h0 + apw_ref[first_power, 1] * h0s_ref[...]
    else:
        hin = shift(h, 1)
    y_ref[...] = y + lax.dot_general(hin.astype(BF16), mt_ref[...], (((1,), (1,)), ((), ())),
                                     preferred_element_type=F32)
    if nchunk == 1:
        hend_ref[...] = h
    else:
        hend_ref[...] = jnp.concatenate([h[(b + 1) * nchunk - 1:(b + 1) * nchunk, :] for b in range(nseq)], axis=0)


def _ssm_call(uf, r, nmat, mt, apw, h0=None, h0s=None, *, layer, nchunk, nsteps, first_power):
    g, rows, w = uf.shape
    assert nchunk & (nchunk - 1) == 0, "chunks per sequence must be a power of two"
    nseq = rows // nchunk
    npw = apw.shape[1]
    gspec = lambda a, b: pl.BlockSpec((None, a, b), lambda i: (i, 0, 0))
    lgspec = lambda a, b: pl.BlockSpec((None, None, a, b), lambda i: (layer, i, 0, 0))
    in_specs = [gspec(rows, w), lgspec(SSM_GROUP, w), lgspec(w, 4 * SSM_STATE), lgspec(w, 2 * SSM_STATE),
                pl.BlockSpec((None, npw, 3, None, 1, 2 * SSM_STATE), lambda i: (layer, 0, 0, i, 0, 0))]
    args = [uf, r, nmat, mt, apw]
    if h0 is not None:
        in_specs += [gspec(rows, 2 * SSM_STATE)] * 2
        args += [h0, h0s]
    return pl.pallas_call(
        functools.partial(_ssm_kernel, nsteps=nsteps, first_power=first_power, nchunk=nchunk, nseq=nseq,
                          has_h0=h0 is not None),
        grid=(g,),
        in_specs=in_specs,
        out_specs=[gspec(rows, w), gspec(nseq, 2 * SSM_STATE)],
        out_shape=[jax.ShapeDtypeStruct((g, rows, w), F32),
                   jax.ShapeDtypeStruct((g, nseq, 2 * SSM_STATE), F32)],
        scratch_shapes=[pltpu.VMEM((w, w), BF16)],
        compiler_params=_cparams("arbitrary"),
        name="ssm",
    )(*args)


def _softplus(z):
    return jnp.maximum(z, 0.0) + jnp.log1p(jnp.exp(-jnp.abs(z)))


def _lru_gates(xc, wg_ref, bg_ref, lam_ref):
    gl = _dot(xc.astype(BF16), wg_ref[...]) + bg_ref[...]
    r = jax.nn.sigmoid(gl[:, :D_LRU])
    gi = jax.nn.sigmoid(gl[:, D_LRU:])
    log_a = -LRU_C * r * _softplus(-lam_ref[...])
    a = jnp.exp(log_a)
    mult = jnp.sqrt(1.0 - a * a)
    return a, mult, gi


def _lru_prompt_kernel(xr_ref, yg_ref, cw_ref, cb_ref, wg_ref, bg_ref, lam_ref, o_ref, hl_ref,
                       xp_scr, hc_scr):
    t = pl.program_id(1)
    tl = xr_ref.shape[0]
    halo = SUBLANES

    @pl.when(t == 0)
    def _():
        xp_scr[0:halo, :] = jnp.zeros((halo, D_LRU), F32)
        hc_scr[...] = jnp.zeros((1, D_LRU), F32)

    xp_scr[halo:halo + tl, :] = xr_ref[...]
    xc = cb_ref[...]
    for j in range(LRU_CONV):
        off = halo - (LRU_CONV - 1) + j
        xc = xc + cw_ref[j:j + 1, :] * xp_scr[off:off + tl, :]
    a, mult, gi = _lru_gates(xc, wg_ref, bg_ref, lam_ref)
    row = lax.broadcasted_iota(jnp.int32, (tl, D_LRU), 0)
    mult = jnp.where((row == 0) & (t == 0), 1.0, mult)
    b = mult * gi * xc
    d = 1
    while d < tl:
        keep = row >= d
        a_sh = jnp.where(keep, pltpu.roll(a, d, axis=0), 1.0)
        b_sh = jnp.where(keep, pltpu.roll(b, d, axis=0), 0.0)
        b = a * b_sh + b
        a = a * a_sh
        d *= 2
    h = a * hc_scr[...] + b
    o_ref[...] = h * jax.nn.gelu(yg_ref[...])
    last = h[tl - 1:tl, :]
    hc_scr[...] = last
    hl_ref[...] = jnp.broadcast_to(last, (SUBLANES, D_LRU))
    xp_scr[0:halo, :] = xp_scr[tl:tl + halo, :]


def _lru_weight_specs(layer):
    return [_layer_spec((LRU_CONV, D_LRU), layer), _layer_spec((1, D_LRU), layer),
            _layer_spec((D_LRU, 2 * D_LRU), layer), _layer_spec((1, 2 * D_LRU), layer),
            _layer_spec((1, D_LRU), layer)]


def _lru_weight_args(w):
    return (w["lru_conv_w"], w["lru_conv_b"], w["lru_wg"], w["lru_bg"], w["lru_lambda"])


def _lru_prompt_call(xr, yg, w, *, layer, bsz, seq_len):
    tl = min(LRU_TILE, seq_len)
    nt = seq_len // tl
    row = pl.BlockSpec((tl, D_LRU), lambda b, t: (b * nt + t, 0))
    return pl.pallas_call(
        _lru_prompt_kernel,
        grid=(bsz, nt),
        in_specs=[row, row] + _lru_weight_specs(layer),
        out_specs=[row, pl.BlockSpec((None, SUBLANES, D_LRU), lambda b, t: (b, 0, 0))],
        out_shape=[jax.ShapeDtypeStruct((bsz * seq_len, D_LRU), F32),
                   jax.ShapeDtypeStruct((bsz, SUBLANES, D_LRU), F32)],
        scratch_shapes=[pltpu.VMEM((tl + 2 * SUBLANES, D_LRU), F32), pltpu.VMEM((1, D_LRU), F32)],
        compiler_params=_cparams("arbitrary", "arbitrary"),
        name="lru_prompt",
    )(xr, yg, *_lru_weight_args(w))


def _lru_sample_kernel(xr_ref, yg_ref, buf_ref, h0_ref, cw_ref, cb_ref, wg_ref, bg_ref, lam_ref,
                       o_ref, hl_ref, *, dseq):
    n = h0_ref.shape[0]
    xp = [buf_ref[j] for j in range(LRU_CONV - 1)] + [xr_ref[pl.ds(t * n, n), :] for t in range(dseq)]
    xcs = []
    for t in range(dseq):
        xc = cb_ref[...]
        for j in range(LRU_CONV):
            xc = xc + cw_ref[j:j + 1, :] * xp[t + j]
        xcs.append(xc)
    xc = jnp.concatenate(xcs, axis=0)
    a, mult, gi = _lru_gates(xc, wg_ref, bg_ref, lam_ref)
    b = mult * gi * xc
    h = h0_ref[...]
    for t in range(dseq):
        h = a[t * n:(t + 1) * n] * h + b[t * n:(t + 1) * n]
        o_ref[pl.ds(t * n, n), :] = h * jax.nn.gelu(yg_ref[pl.ds(t * n, n), :])
    hl_ref[...] = h


def _lru_sample_call(xr, yg, buf_tm, h0, w, *, layer, dseq):
    n = h0.shape[0]
    full = lambda a: pl.BlockSpec(a.shape, lambda i: (0,) * a.ndim)
    return pl.pallas_call(
        functools.partial(_lru_sample_kernel, dseq=dseq),
        grid=(1,),
        in_specs=[full(xr), full(yg), full(buf_tm), full(h0)] + _lru_weight_specs(layer),
        out_specs=[pl.BlockSpec((dseq * n, D_LRU), lambda i: (0, 0)), pl.BlockSpec((n, D_LRU), lambda i: (0, 0))],
        out_shape=[jax.ShapeDtypeStruct((dseq * n, D_LRU), F32), jax.ShapeDtypeStruct((n, D_LRU), F32)],
        compiler_params=_cparams("arbitrary"),
        name="lru_sample",
    )(xr, yg, buf_tm, h0, *_lru_weight_args(w))


def _post_kernel(x_ref, oa_ref, ys_ref, u_ref, ol_ref, g1_ref, sh2_ref, sc2_ref, g2_ref,
                 d_ref, wglu_ref, bglu_ref, on_ref, wo_ref, n2_ref, wup_ref, cw_ref, cb_ref, wdn_ref,
                 halo_in_ref, xo_ref, halo_out_ref, h2_scr, acc_scr, halo_scr, ys_scr,
                 *, per_token, row_shift, group_major):
    t = pl.program_id(1)
    tm = x_ref.shape[0]
    hrows = halo_in_ref.shape[0]

    @pl.when(t == 0)
    def _():
        halo_scr[...] = halo_in_ref[...]

    if group_major:
        _from_group_major(ys_ref, ys_scr, ys_ref.shape[1], SSM_CHUNK)
        ys = jnp.concatenate([ys_scr[h] for h in range(SSM_HALVES)], axis=1)
    else:
        ys = ys_ref[...]
    ys = ys + d_ref[...] * u_ref[...]
    gs = jax.nn.gelu(ys)
    o_ssm = gs * jax.nn.sigmoid(_dot(gs.astype(BF16), wglu_ref[...]) + bglu_ref[...])
    on = on_ref[...]
    c1 = D_ATTN
    c2 = c1 + D_SSM
    o = jnp.concatenate([_rms(oa_ref[...], on[:, :c1]), _rms(o_ssm, on[:, c1:c2]),
                         _rms(ol_ref[...], on[:, c2:])], axis=-1)
    x1 = x_ref[...] + _mod_rows(g1_ref, per_token, tm) * _dot(o.astype(BF16), wo_ref[...])
    h2 = _rms(x1, n2_ref[...]) * (1.0 + _mod_rows(sc2_ref, per_token, tm)) + _mod_rows(sh2_ref, per_token, tm)
    h2_scr[...] = h2.astype(BF16)
    rowh = lax.broadcasted_iota(jnp.int32, (hrows, FF_CHUNK), 0)
    msub = tm // FF_ROW_SPLIT

    def ff_cols(c, gv):
        return slice(gv * D_FF + c * FF_CHUNK, gv * D_FF + (c + 1) * FF_CHUNK)

    def up_proj(c):
        return [jnp.concatenate([_dot(h2_scr[r * msub:(r + 1) * msub, :], wup_ref[:, ff_cols(c, gv)])
                                 for r in range(FF_ROW_SPLIT)], axis=0) for gv in range(2)]

    ups = up_proj(0)
    for c in range(N_FF_CHUNKS):
        cur = ups
        if c + 1 < N_FF_CHUNKS:
            ups = up_proj(c + 1)
        halves = []
        for gv in range(2):
            cols = ff_cols(c, gv)
            up = cur[gv]
            halo = halo_scr[:, cols]
            y = cb_ref[:, cols] + cw_ref[FFN_CONV - 1:FFN_CONV, cols] * up
            for back in range(1, FFN_CONV):
                sh = back * row_shift
                r = pltpu.roll(up, sh, axis=0)
                hr = pltpu.roll(halo, sh, axis=0) if sh % hrows else halo
                head = jnp.where(rowh < sh, hr, r[:hrows])
                shifted = jnp.concatenate([head, r[hrows:]], axis=0)
                j = FFN_CONV - 1 - back
                y = y + cw_ref[j:j + 1, cols] * shifted
            halo_scr[:, cols] = up[tm - hrows:, :]
            halves.append(y)
        act = (jax.nn.gelu(halves[0]) * halves[1]).astype(BF16)
        contrib = _dot(act, wdn_ref[c * FF_CHUNK:(c + 1) * FF_CHUNK, :])
        if c == 0:
            acc_scr[...] = contrib
        else:
            acc_scr[...] += contrib
    xo_ref[...] = x1 + _mod_rows(g2_ref, per_token, tm) * acc_scr[...]
    halo_out_ref[...] = halo_scr[...]


def _post_call(x, oa, ys, u, ol, mod, w, halo_in, *, layer, per_token, seq_len, row_shift, mod_seqs=None):
    nt = x.shape[0]
    tm = nt if per_token else min(POST_TILE, nt)
    tiles_per_seq = seq_len // tm
    nseq = nt // seq_len
    hrows = halo_in.shape[1]
    row = lambda wd: pl.BlockSpec((tm, wd), lambda s, t: (s * tiles_per_seq + t, 0))
    mspec = lambda piece: _mod_spec(layer, piece, per_token, lambda s, t: s)(mod_seqs)
    halo_spec = pl.BlockSpec((None, hrows, 2 * D_FF), lambda s, t: (s, 0, 0))
    group_major = ys.ndim == 3
    if group_major:
        ys_spec = pl.BlockSpec((N_SSM_GROUPS, tm // SSM_CHUNK, SSM_CHUNK * SSM_GROUP),
                               lambda s, t: (0, s * tiles_per_seq + t, 0))
    else:
        ys_spec = row(D_SSM)
    return pl.pallas_call(
        functools.partial(_post_kernel, per_token=per_token, row_shift=row_shift, group_major=group_major),
        grid=(nseq, tiles_per_seq),
        in_specs=[
            row(D_MODEL), row(D_ATTN), ys_spec, row(D_SSM), row(D_LRU),
            mspec(2), mspec(3), mspec(4), mspec(5),
            _layer_spec((1, D_SSM), layer), _layer_spec((D_SSM, D_SSM), layer), _layer_spec((1, D_SSM), layer),
            _layer_spec((1, D_MODEL), layer), _layer_spec((D_MODEL, D_MODEL), layer, True),
            _layer_spec((1, D_MODEL), layer),
            _layer_spec((D_MODEL, 2 * D_FF), layer, True),
            _layer_spec((FFN_CONV, 2 * D_FF), layer),
            _layer_spec((1, 2 * D_FF), layer),
            _layer_spec((D_FF, D_MODEL), layer, True),
            halo_spec,
        ],
        out_specs=[row(D_MODEL), halo_spec],
        out_shape=[jax.ShapeDtypeStruct((nt, D_MODEL), F32),
                   jax.ShapeDtypeStruct((nseq, hrows, 2 * D_FF), F32)],
        scratch_shapes=[pltpu.VMEM((tm, D_MODEL), BF16), pltpu.VMEM((tm, D_MODEL), F32),
                        pltpu.VMEM((hrows, 2 * D_FF), F32),
                        pltpu.VMEM((SSM_HALVES, tm, LANES), F32)],
        compiler_params=_cparams("arbitrary", "arbitrary"),
        name="post",
    )(x, oa, ys, u, ol, mod, mod, mod, mod, w["ssm_d"], w["ssm_w_glu"], w["ssm_b_glu"], w["out_norm"],
      w["w_o"], w["norm2"], w["w_up"], w["ffn_conv_w"], w["ffn_conv_b"], w["w_down"], halo_in)


def _block_diag(w):
    depth, nb, bs, _ = w.shape
    eye = jnp.eye(nb, dtype=w.dtype)
    return jnp.einsum("lhij,hk->lhikj", w, eye).reshape(depth, nb * bs, nb * bs)


def _prepare_weights(p):
    depth = p["w_in"].shape[0]
    row = lambda a: a.reshape(depth, 1, -1)
    return dict(
        norm1=row(p["norm1"]), norm2=row(p["norm2"]), out_norm=row(p["out_norm"]),
        w_in=p["w_in"].astype(BF16), w_o=p["w_o"].astype(BF16),
        q_norm=row(jnp.tile(p["q_norm"], (1, N_HEADS))), k_norm=row(jnp.tile(p["k_norm"], (1, N_KV_HEADS))),
        ssm_d=row(p["ssm_d"]), ssm_w_glu=p["ssm_w_glu"].astype(BF16), ssm_b_glu=row(p["ssm_b_glu"]),
        lru_conv_w=p["lru_conv_w"], lru_conv_b=row(p["lru_conv_b"]),
        lru_wg=jnp.concatenate([_block_diag(p["lru_w_a"]), _block_diag(p["lru_w_i"])], axis=-1).astype(BF16),
        lru_bg=row(jnp.concatenate([p["lru_b_a"], p["lru_b_i"]], axis=-1)),
        lru_lambda=row(p["lru_lambda"]),
        w_up=p["ffn_w_up"].astype(BF16), ffn_conv_w=p["ffn_conv_w"], ffn_conv_b=row(p["ffn_conv_b"]),
        w_down=p["ffn_w_down"].astype(BF16),
    )


def _prompt_layer(x, mod, w, seg, sinks, ssm_tabs, cos, sin, *, layer, bsz, seq_len, nsteps):
    nchunk = seq_len // SSM_CHUNK
    q, k, v, u, xr, yg, uf = _pre_call(x, mod, w, seg, cos, sin, layer=layer, per_token=False, seq_len=seq_len)
    oa = _attn_prompt_call(sinks, q, k, v, layer=layer, bsz=bsz, seq_len=seq_len)
    r_all, np_all, mtp_all, _, _, apw = ssm_tabs
    yf, hend = _ssm_call(uf, r_all, np_all, mtp_all, apw, layer=layer, nchunk=nchunk, nsteps=nsteps,
                         first_power=1)
    ol, hl = _lru_prompt_call(xr, yg, w, layer=layer, bsz=bsz, seq_len=seq_len)
    halo0 = jnp.zeros((bsz, SUBLANES, 2 * D_FF), F32)
    x_new, halo = _post_call(x, oa, yf, u, ol, mod, w, halo0, layer=layer, per_token=False, seq_len=seq_len,
                             row_shift=1)
    keep = min(WINDOW, seq_len)
    last = lambda a, nrows: a.reshape(bsz, seq_len, a.shape[-1])[:, seq_len - nrows:]
    hend_b = jnp.transpose(hend, (1, 0, 2))
    states = (last(k, keep).reshape(bsz, keep, N_KV_HEADS, HEAD_DIM),
              last(v, keep).reshape(bsz, keep, N_KV_HEADS, HEAD_DIM),
              hend_b[..., :SSM_STATE], hend_b[..., SSM_STATE:],
              hl[:, 0, :],
              last(xr, LRU_CONV - 1),
              halo[:, SUBLANES - (FFN_CONV - 1):, :])
    return x_new, states


def _sample_layer(x, mod, w, seg, sinks, ssm_tabs, cos, sin, st, *, layer, n, dseq):
    g = N_SSM_GROUPS
    ck, cv, s_re, s_im, lru_h, lru_conv, ffn_conv = st
    nt = n * dseq
    q, k, v, u, xr, yg = _pre_call(x, mod, w, seg, cos, sin, layer=layer, per_token=True, seq_len=nt, nseq=n)
    q_sm = jnp.transpose(q.reshape(dseq, n, N_KV_HEADS, KV_GROUP, HEAD_DIM), (1, 2, 0, 3, 4))
    q_sm = q_sm.reshape(n, N_KV_HEADS, dseq * KV_GROUP, HEAD_DIM)
    kn = jnp.transpose(k.reshape(dseq, n, D_KV), (1, 0, 2))
    vn = jnp.transpose(v.reshape(dseq, n, D_KV), (1, 0, 2))
    wbuf = ck.shape[1]
    o_sm, sk, sv = _attn_sample_call(sinks, q_sm, kn, vn, ck.reshape(n, wbuf, D_KV), cv.reshape(n, wbuf, D_KV),
                                     layer=layer, dseq=dseq)
    oa = jnp.transpose(o_sm.reshape(n, N_KV_HEADS, dseq, KV_GROUP, HEAD_DIM), (2, 0, 1, 3, 4)).reshape(nt, D_ATTN)
    r_all, _, _, ns_all, mts_all, apw = ssm_tabs
    wpad = LANES - dseq * SSM_GROUP
    uf = jnp.transpose(u.reshape(dseq, n, g, SSM_GROUP), (2, 1, 0, 3)).reshape(g, n, dseq * SSM_GROUP)
    uf = jnp.pad(uf, ((0, 0), (0, 0), (0, wpad))).astype(BF16)
    h_re = jnp.transpose(s_re, (1, 0, 2))
    h_im = jnp.transpose(s_im, (1, 0, 2))
    h0 = jnp.concatenate([h_re, h_im], axis=-1)
    h0s = jnp.concatenate([h_im, h_re], axis=-1)
    yf, hend = _ssm_call(uf, r_all, ns_all, mts_all, apw, h0, h0s, layer=layer, nchunk=1, nsteps=0, first_power=0)
    yf = yf[:, :, :dseq * SSM_GROUP]
    ys = jnp.transpose(yf.reshape(g, n, dseq, SSM_GROUP), (2, 1, 0, 3)).reshape(nt, D_SSM)
    ol, hl = _lru_sample_call(xr, yg, jnp.transpose(lru_conv, (1, 0, 2)), lru_h, w, layer=layer, dseq=dseq)
    nconv = FFN_CONV - 1
    halo0 = jnp.transpose(ffn_conv, (1, 0, 2)).reshape(1, nconv * n, 2 * D_FF)
    x_new, halo = _post_call(x, oa, ys, u, ol, mod, w, halo0, layer=layer, per_token=True, seq_len=nt,
                             row_shift=n, mod_seqs=n)
    hend_b = jnp.transpose(hend, (1, 0, 2))
    xr_tm = xr.reshape(dseq, n, D_LRU)
    lru_conv_all = jnp.concatenate([jnp.transpose(lru_conv, (1, 0, 2)), xr_tm], axis=0)
    states = (sk.reshape(n, wbuf, N_KV_HEADS, HEAD_DIM), sv.reshape(n, wbuf, N_KV_HEADS, HEAD_DIM),
              hend_b[..., :SSM_STATE], hend_b[..., SSM_STATE:],
              hl,
              jnp.transpose(lru_conv_all[dseq:], (1, 0, 2)),
              jnp.transpose(halo.reshape(nconv, n, 2 * D_FF), (1, 0, 2)))
    return x_new, states


def kernel(x_prompt, x_sample, cache_k, cache_v, state_ssm_re, state_ssm_im, state_lru_h, state_lru_conv,
           state_ffn_conv, c_prompt, c_sample, w_ada, b_ada, norm1, w_in, q_norm, k_norm, sinks, ssm_a_re,
           ssm_a_im, ssm_b_re, ssm_b_im, ssm_c_re, ssm_c_im, ssm_d, ssm_log_dt, ssm_w_glu, ssm_b_glu,
           lru_conv_w, lru_conv_b, lru_w_a, lru_b_a, lru_w_i, lru_b_i, lru_lambda, out_norm, w_o, norm2,
           ffn_w_up, ffn_conv_w, ffn_conv_b, ffn_w_down):
    bsz, seq_len = x_prompt.shape[:2]
    n, dseq = x_sample.shape[:2]
    depth = w_in.shape[0]
    params = dict(norm1=norm1, w_in=w_in, q_norm=q_norm, k_norm=k_norm, ssm_d=ssm_d,
                  ssm_w_glu=ssm_w_glu, ssm_b_glu=ssm_b_glu, lru_conv_w=lru_conv_w, lru_conv_b=lru_conv_b,
                  lru_w_a=lru_w_a, lru_b_a=lru_b_a, lru_w_i=lru_w_i, lru_b_i=lru_b_i, lru_lambda=lru_lambda,
                  out_norm=out_norm, w_o=w_o, norm2=norm2, ffn_w_up=ffn_w_up, ffn_conv_w=ffn_conv_w,
                  ffn_conv_b=ffn_conv_b, ffn_w_down=ffn_w_down)
    weights = _prepare_weights(params)
    seg = (jnp.arange(D_ATTN)[:, None] // HEAD_DIM == jnp.arange(D_ATTN)[None, :] // HEAD_DIM).astype(BF16)

    rows = n + bsz
    pad = (-rows) % SUBLANES
    c_all = jnp.concatenate([c_sample, c_prompt, jnp.zeros((pad, D_MODEL), F32)], axis=0)
    mod_all = _ada_call(c_all, w_ada, b_ada)
    mod_p = jnp.broadcast_to(mod_all[:, :, n:n + bsz, None, :], (depth, N_MOD, bsz, SUBLANES, D_MODEL))

    cos_p, sin_p = _rope_call(jnp.arange(seq_len, dtype=F32))
    cos_s, sin_s = _rope_call(jnp.repeat(PAST_LEN + jnp.arange(dseq, dtype=F32), n))

    nchunk = seq_len // SSM_CHUNK
    nsteps = max(nchunk - 1, 0).bit_length()
    powers = (dseq,) + tuple(SSM_CHUNK * (1 << i) for i in range(nsteps))
    assert dseq * SSM_GROUP <= LANES
    ssm_tabs = _ssm_prep_call(ssm_a_re, ssm_a_im, ssm_log_dt, ssm_b_re, ssm_b_im, ssm_c_re, ssm_c_im,
                              lc=SSM_CHUNK, dseq=dseq, powers=powers)

    xp = x_prompt.reshape(bsz * seq_len, D_MODEL)
    xs = jnp.transpose(x_sample, (1, 0, 2)).reshape(dseq * n, D_MODEL)
    new_p, new_s = [], []
    for i in range(depth):
        xp, st_p = _prompt_layer(xp, mod_p, weights, seg, sinks, ssm_tabs, cos_p, sin_p, layer=i, bsz=bsz,
                                 seq_len=seq_len, nsteps=nsteps)
        st_in = (cache_k[i], cache_v[i], state_ssm_re[i], state_ssm_im[i], state_lru_h[i], state_lru_conv[i],
                 state_ffn_conv[i])
        xs, st_s = _sample_layer(xs, mod_all, weights, seg, sinks, ssm_tabs, cos_s, sin_s, st_in, layer=i, n=n,
                                 dseq=dseq)
        new_p.append(st_p)
        new_s.append(st_s)
    pk, pv, p_re, p_im, p_lh, p_lc, p_fc = [jnp.stack(s) for s in zip(*new_p)]
    sk, sv, s_re, s_im, s_lh, s_lc, s_fc = [jnp.stack(s) for s in zip(*new_s)]
    y_p = xp.reshape(bsz, seq_len, D_MODEL)
    y_s = jnp.transpose(xs.reshape(dseq, n, D_MODEL), (1, 0, 2))
    return (y_p, y_s, pk, pv, p_re, p_im, p_lh, p_lc, p_fc, sk, sv, s_re, s_im, s_lh, s_lc, s_fc)
```

```python
import functools

import jax
import jax.numpy as jnp
from jax import lax
from jax.experimental import pallas as pl
from jax.experimental.pallas import tpu as pltpu

F32 = jnp.float32
BF16 = jnp.bfloat16

D_MODEL = 1024
HEAD_DIM = 64
N_HEADS = 8
N_KV_HEADS = 2
KV_GROUP = N_HEADS // N_KV_HEADS
D_ATTN = N_HEADS * HEAD_DIM
D_KV = N_KV_HEADS * HEAD_DIM
WINDOW = 128
ROPE_THETA = 10000.0
PAST_LEN = 8192
D_SSM = 256
SSM_GROUP = 16
N_SSM_GROUPS = 16
SSM_STATE = 64
D_LRU = 256
N_LRU_BLOCKS = 4
LRU_BLOCK = 64
LRU_CONV = 4
LRU_C = 8.0
D_FF = 2816
FFN_CONV = 3
D_IN = D_ATTN + 2 * D_KV + D_SSM + 2 * D_LRU
N_MOD = 6
EPS = 1e-6

SUBLANES = 8
LANES = 128
VMEM_LIMIT_BYTES = 56 * 1024 * 1024

TOKEN_TILE = 512
ATTN_BLOCK = WINDOW
ATTN_SUBBLOCKS = 4
SSM_CHUNK = 32
LRU_TILE = 512
FF_CHUNK = 256
N_FF_CHUNKS = D_FF // FF_CHUNK
FF_ROW_SPLIT = 1
PRE_ROW_SPLIT = 2
POST_TILE = 256
SAMPLE_ATTN_SEQS = 16
SSM_HALVES = D_SSM // LANES
GROUPS_PER_HALF = LANES // SSM_GROUP


def _cparams(*sem):
    return pltpu.CompilerParams(dimension_semantics=sem, vmem_limit_bytes=VMEM_LIMIT_BYTES)


def _dot(a, b):
    return jnp.dot(a, b, preferred_element_type=F32)


def _split_bf16(x):
    hi = x.astype(BF16)
    lo = (x - hi.astype(F32)).astype(BF16)
    return hi, lo


def _layer_spec(shape, layer, single=False):
    nd = len(shape)
    idx = lambda *_: (layer,) + (0,) * nd
    if single:
        return pl.BlockSpec((None,) + tuple(shape), idx, pipeline_mode=pl.Buffered(1))
    return pl.BlockSpec((None,) + tuple(shape), idx)


def _ada_kernel(c_ref, w_ref, b_ref, o_ref):
    c = c_ref[...]
    s = jax.nn.silu(c).astype(BF16)
    o_ref[...] = _dot(s, w_ref[...].astype(BF16)) + b_ref[...]


def _ada_call(c_all, w_ada, b_ada):
    depth = w_ada.shape[0]
    rows = c_all.shape[0]
    return pl.pallas_call(
        _ada_kernel,
        grid=(depth, N_MOD),
        in_specs=[
            pl.BlockSpec((rows, D_MODEL), lambda l, j: (0, 0)),
            pl.BlockSpec((None, D_MODEL, D_MODEL), lambda l, j: (l, 0, j)),
            pl.BlockSpec((None, 1, D_MODEL), lambda l, j: (l, 0, j)),
        ],
        out_specs=pl.BlockSpec((None, None, rows, D_MODEL), lambda l, j: (l, j, 0, 0)),
        out_shape=jax.ShapeDtypeStruct((depth, N_MOD, rows, D_MODEL), F32),
        compiler_params=_cparams("arbitrary", "arbitrary"),
        name="ada",
    )(c_all, w_ada, b_ada.reshape(depth, 1, N_MOD * D_MODEL))


def _rope_kernel(pos_ref, cos_ref, sin_ref):
    pos = pos_ref[...]
    lane = lax.broadcasted_iota(jnp.int32, pos.shape, 1)
    half = HEAD_DIM // 2
    j = (lane & (half - 1)).astype(F32)
    inv = ROPE_THETA ** (-j / half)
    ang = pos * inv
    cos_ref[...] = jnp.cos(ang)
    s = jnp.sin(ang)
    sin_ref[...] = jnp.where((lane & (HEAD_DIM - 1)) < half, -s, s)


def _rope_call(pos_rows):
    t = pos_rows.shape[0]
    pos_b = jnp.broadcast_to(pos_rows[:, None], (t, LANES))
    return pl.pallas_call(
        _rope_kernel,
        out_shape=(jax.ShapeDtypeStruct((t, LANES), F32), jax.ShapeDtypeStruct((t, LANES), F32)),
        name="rope",
    )(pos_b)


def _mod_rows(ref, per_token, tm):
    if not per_token:
        return ref[0:1, :]
    m = ref[...]
    reps = tm // m.shape[0]
    return jnp.concatenate([m] * reps, axis=0) if reps > 1 else m


def _mod_spec(layer, piece, per_token, seq_of_step):
    if per_token:
        return lambda n: pl.BlockSpec((None, None, n, D_MODEL), lambda *g: (layer, piece, 0, 0))
    return lambda n: pl.BlockSpec((None, None, None, SUBLANES, D_MODEL),
                                  lambda *g: (layer, piece, seq_of_step(*g), 0, 0))


def _rms(x, gain):
    return x * lax.rsqrt(jnp.mean(x * x, axis=-1, keepdims=True) + EPS) * gain


def _head_rms(t, seg, gain):
    ss = _dot((t * t).astype(BF16), seg)
    return t * lax.rsqrt(ss * (1.0 / HEAD_DIM) + EPS) * gain


def _rope(t, cos, sin):
    width = t.shape[1]
    reps = width // LANES
    if reps > 1:
        cos = jnp.concatenate([cos] * reps, axis=1)
        sin = jnp.concatenate([sin] * reps, axis=1)
    lane = lax.broadcasted_iota(jnp.int32, t.shape, 1)
    half = HEAD_DIM // 2
    up = pltpu.roll(t, width - half, axis=1)
    dn = pltpu.roll(t, half, axis=1)
    rot = jnp.where((lane & (HEAD_DIM - 1)) < half, up, dn)
    return t * cos + rot * sin


def _unit_transpose8(vs):
    lane = lax.broadcasted_iota(jnp.int32, vs[0].shape, 1)
    unit = lane >> 4
    for b in range(3):
        d = 1 << b
        bit = (unit >> b) & 1
        new = list(vs)
        for i in range(8):
            if (i >> b) & 1 == 0:
                lo, hi = vs[i], vs[i + d]
                new[i] = jnp.where(bit == 0, lo, pltpu.roll(hi, d * SSM_GROUP, axis=1))
                new[i + d] = jnp.where(bit == 1, hi, pltpu.roll(lo, LANES - d * SSM_GROUP, axis=1))
        vs = new
    return vs


def _to_group_major(u_ref, nk, lc):
    outs = [[None] * (lc // 8) for _ in range(N_SSM_GROUPS)]
    for h in range(SSM_HALVES):
        for tb in range(lc // 8):
            vs = [u_ref[h, pl.ds(tb * 8 + tp, nk, stride=lc), :] for tp in range(8)]
            ws = _unit_transpose8(vs)
            for gp in range(GROUPS_PER_HALF):
                outs[h * GROUPS_PER_HALF + gp][tb] = ws[gp]
    return [jnp.concatenate(o, axis=1) for o in outs]


def _from_group_major(yf_ref, ys_ref, nk, lc):
    for h in range(SSM_HALVES):
        for tb in range(lc // 8):
            ws = [yf_ref[h * GROUPS_PER_HALF + gp, :, tb * LANES:(tb + 1) * LANES] for gp in range(GROUPS_PER_HALF)]
            vs = _unit_transpose8(ws)
            for tp in range(8):
                ys_ref[h, pl.ds(tb * 8 + tp, nk, stride=lc), :] = vs[tp]


def _pre_kernel(x_ref, sh_ref, sc_ref, n1_ref, w_ref, qn_ref, kn_ref, seg_ref, cos_ref, sin_ref,
                q_ref, k_ref, v_ref, u_ref, xr_ref, yg_ref, *maybe_uf_ref, per_token):
    tm = x_ref.shape[0]
    parts = PRE_ROW_SPLIT if tm % (PRE_ROW_SPLIT * SSM_CHUNK) == 0 else 1
    tp = tm // parts
    c1 = D_ATTN
    c2 = c1 + D_KV
    c3 = c2 + D_KV
    c4 = c3 + D_SSM
    c5 = c4 + D_LRU
    sc = _mod_rows(sc_ref, per_token, tm)
    sh = _mod_rows(sh_ref, per_token, tm)
    seg = seg_ref[...]

    def project(p):
        rows = slice(p * tp, (p + 1) * tp)
        scp, shp = (sc[rows], sh[rows]) if per_token else (sc, sh)
        hb = (_rms(x_ref[rows, :], n1_ref[...]) * (1.0 + scp) + shp).astype(BF16)
        return _dot(hb, w_ref[...])

    def finish(p, proj):
        rows = slice(p * tp, (p + 1) * tp)
        cos = cos_ref[rows, :]
        sin = sin_ref[rows, :]
        qn = _head_rms(proj[:, :c1], seg, qn_ref[...])
        kn = _head_rms(proj[:, c1:c2], seg[:D_KV, :D_KV], kn_ref[...])
        v_ref[rows, :] = proj[:, c2:c3]
        u_ref[rows, :] = proj[:, c3:c4]
        xr_ref[rows, :] = proj[:, c4:c5]
        yg_ref[rows, :] = proj[:, c5:]
        if maybe_uf_ref:
            u_scr = maybe_uf_ref[1]
            for hh in range(SSM_HALVES):
                u_scr[hh, rows, :] = proj[:, c3 + hh * LANES:c3 + (hh + 1) * LANES]
        q_ref[rows, :] = (_rope(qn, cos, sin) * (HEAD_DIM ** -0.5)).astype(BF16)
        k_ref[rows, :] = _rope(kn, cos, sin)

    nxt = project(0)
    for p in range(parts):
        cur = nxt
        if p + 1 < parts:
            nxt = project(p + 1)
        finish(p, cur)
    if maybe_uf_ref:
        uf_ref, u_scr = maybe_uf_ref
        groups = _to_group_major(u_scr, uf_ref.shape[1], SSM_CHUNK)
        for g in range(N_SSM_GROUPS):
            uf_ref[g] = groups[g].astype(BF16)


def _pre_call(x, mod, w, seg, cos, sin, *, layer, per_token, seq_len, nseq=None):
    nt = x.shape[0]
    tm = min(TOKEN_TILE, nt)
    tiles_per_seq = seq_len // tm
    if per_token:
        tab_spec = pl.BlockSpec((tm, LANES), lambda i: (i, 0))
    else:
        tab_spec = pl.BlockSpec((tm, LANES), lambda i: (i % tiles_per_seq, 0))
    mspec = lambda piece: _mod_spec(layer, piece, per_token, lambda i: i // tiles_per_seq)(nseq)
    row = lambda wd: pl.BlockSpec((tm, wd), lambda i: (i, 0))
    widths = (D_ATTN, D_KV, D_KV, D_SSM, D_LRU, D_LRU)
    dtypes = (BF16, F32, F32, F32, F32, F32)
    out_specs = [row(wd) for wd in widths]
    out_shape = [jax.ShapeDtypeStruct((nt, wd), d) for wd, d in zip(widths, dtypes)]
    scratch = []
    if not per_token:
        nk = tm // SSM_CHUNK
        wf = SSM_CHUNK * SSM_GROUP
        out_specs.append(pl.BlockSpec((N_SSM_GROUPS, nk, wf), lambda i: (0, i, 0)))
        out_shape.append(jax.ShapeDtypeStruct((N_SSM_GROUPS, nt // SSM_CHUNK, wf), BF16))
        scratch.append(pltpu.VMEM((SSM_HALVES, tm, LANES), F32))
    return pl.pallas_call(
        functools.partial(_pre_kernel, per_token=per_token),
        grid=(nt // tm,),
        in_specs=[
            row(D_MODEL), mspec(0), mspec(1),
            _layer_spec((1, D_MODEL), layer),
            _layer_spec((D_MODEL, D_IN), layer, True),
            _layer_spec((1, D_ATTN), layer),
            _layer_spec((1, D_KV), layer),
            pl.BlockSpec((D_ATTN, D_ATTN), lambda i: (0, 0)),
            tab_spec, tab_spec,
        ],
        out_specs=out_specs,
        out_shape=out_shape,
        scratch_shapes=scratch,
        compiler_params=_cparams("arbitrary"),
        name="pre",
    )(x, mod, mod, w["norm1"], w["w_in"], w["q_norm"], w["k_norm"], seg, cos, sin)


def _attn_prompt_kernel(sink_ref, q_ref, kc_ref, kp_ref, vc_ref, vp_ref, o_ref, *, layer, nsub):
    assert (KV_GROUP, N_KV_HEADS, 2 * HEAD_DIM) == (4, 2, LANES)
    i = pl.program_id(1)
    bq = ATTN_BLOCK
    kcat = jnp.concatenate([kp_ref[...], kc_ref[...]], axis=0).astype(BF16)
    vcat = jnp.concatenate([vp_ref[...], vc_ref[...]], axis=0).astype(BF16)
    low = lax.broadcasted_iota(jnp.int32, kcat.shape, 1) < HEAD_DIM

    def half_placed(x):
        zero = jnp.zeros_like(x)
        h0_lo = jnp.where(low, x, zero)
        h1_hi = jnp.where(low, zero, x)
        return [[h0_lo, pltpu.roll(h0_lo, HEAD_DIM, axis=1)], [pltpu.roll(h1_hi, HEAD_DIM, axis=1), h1_hi]]

    kz = half_placed(kcat)
    vz = half_placed(vcat)
    qi = lax.broadcasted_iota(jnp.int32, (bq, 2 * bq), 0)
    si = lax.broadcasted_iota(jnp.int32, (bq, 2 * bq), 1)
    diff = qi + bq - si
    in_window = (diff >= 0) & (diff < WINDOW)
    first_pair = lax.broadcasted_iota(jnp.int32, (2 * bq, 1), 0) < bq
    heads = [(h, e) for h in range(N_KV_HEADS) for e in range(2)]

    def score_block(j):
        band = slice(j * bq, (j + 2) * bq)
        qrows = slice(j * bq, (j + 1) * bq)
        scores = []
        for h, e in heads:
            qh = jnp.concatenate([q_ref[qrows, (2 * h + p) * LANES:(2 * h + p + 1) * LANES] for p in range(2)],
                                 axis=0)
            s = lax.dot_general(qh, kz[h][e][band], (((1,), (1,)), ((), ())), preferred_element_type=F32)
            scores.append(s)
        return scores

    nxt = score_block(0)
    for j in range(nsub):
        scores = nxt
        if j + 1 < nsub:
            nxt = score_block(j + 1)
        if j == 0:
            allowed = in_window & ((si >= bq) | (i > 0))
        else:
            allowed = in_window
        allowed = jnp.concatenate([allowed, allowed], axis=0)
        band = slice(j * bq, (j + 2) * bq)
        qrows = slice(j * bq, (j + 1) * bq)
        probs = []
        for (h, e), s in zip(heads, scores):
            s = jnp.where(allowed, s, -jnp.inf)
            sink = jnp.where(first_pair, sink_ref[layer, KV_GROUP * h + e], sink_ref[layer, KV_GROUP * h + 2 + e])
            m = jnp.maximum(jnp.max(s, axis=-1, keepdims=True), sink)
            p = jnp.exp(s - m)
            denom = jnp.sum(p, axis=-1, keepdims=True) + jnp.exp(sink - m)
            probs.append((p.astype(BF16), denom))
        outs = [_dot(p, vz[h][e][band]) / denom for (h, e), (p, denom) in zip(heads, probs)]
        for h in range(N_KV_HEADS):
            o_pair = outs[2 * h] + outs[2 * h + 1]
            o_ref[qrows, (2 * h) * LANES:(2 * h + 1) * LANES] = o_pair[:bq]
            o_ref[qrows, (2 * h + 1) * LANES:(2 * h + 2) * LANES] = o_pair[bq:]


def _attn_prompt_call(sinks, q, k, v, *, layer, bsz, seq_len):
    nsub = min(ATTN_SUBBLOCKS, seq_len // ATTN_BLOCK)
    bq = nsub * ATTN_BLOCK
    nb = seq_len // bq
    nb_small = seq_len // ATTN_BLOCK
    cur = lambda w: pl.BlockSpec((bq, w), lambda b, i: (b * nb + i, 0))
    prev = lambda w: pl.BlockSpec((ATTN_BLOCK, w), lambda b, i: (b * nb_small + jnp.maximum(i * nsub - 1, 0), 0))
    return pl.pallas_call(
        functools.partial(_attn_prompt_kernel, layer=layer, nsub=nsub),
        grid=(bsz, nb),
        in_specs=[pl.BlockSpec(memory_space=pltpu.SMEM), cur(D_ATTN), cur(D_KV), prev(D_KV), cur(D_KV),
                  prev(D_KV)],
        out_specs=cur(D_ATTN),
        out_shape=jax.ShapeDtypeStruct((bsz * seq_len, D_ATTN), F32),
        compiler_params=_cparams("arbitrary", "arbitrary"),
        name="attn_prompt",
    )(sinks, q, k, k, v, v)


def _attn_sample_kernel(sink_ref, q_ref, kn_ref, vn_ref, ck_ref, cv_ref, o_ref, sk_ref, sv_ref, *, layer, dseq):
    kk = jnp.concatenate([ck_ref[...], kn_ref[...]], axis=1)
    vv = jnp.concatenate([cv_ref[...], vn_ref[...]], axis=1)
    wbuf = ck_ref.shape[1]
    sk_ref[...] = kk[:, dseq:, :]
    sv_ref[...] = vv[:, dseq:, :]
    nq = dseq * KV_GROUP
    nk = wbuf + dseq
    qrow = lax.broadcasted_iota(jnp.int32, (nq, nk), 0)
    j = lax.broadcasted_iota(jnp.int32, (nq, nk), 1)
    diff = (qrow >> 2) + wbuf - j
    allowed = ((diff >= 0) & (diff < WINDOW))[None]
    g_of_row = lax.broadcasted_iota(jnp.int32, (nq, 1), 0) & (KV_GROUP - 1)
    kkb = kk.astype(BF16)
    vvb = vv.astype(BF16)
    for h in range(N_KV_HEADS):
        kh = kkb[:, :, h * HEAD_DIM:(h + 1) * HEAD_DIM]
        vh = vvb[:, :, h * HEAD_DIM:(h + 1) * HEAD_DIM]
        qh = q_ref[:, h]
        s = jnp.einsum("sqd,skd->sqk", qh, kh, preferred_element_type=F32)
        s = jnp.where(allowed, s, -jnp.inf)
        sink = jnp.full((nq, 1), sink_ref[layer, h * KV_GROUP], F32)
        for g in range(1, KV_GROUP):
            sink = jnp.where(g_of_row == g, sink_ref[layer, h * KV_GROUP + g], sink)
        sink = sink[None]
        m = jnp.maximum(jnp.max(s, axis=-1, keepdims=True), sink)
        p = jnp.exp(s - m)
        denom = jnp.sum(p, axis=-1, keepdims=True) + jnp.exp(sink - m)
        o = jnp.einsum("sqk,skd->sqd", p.astype(BF16), vh, preferred_element_type=F32) / denom
        o_ref[:, h] = o


def _attn_sample_call(sinks, q, kn, vn, ck, cv, *, layer, dseq):
    n, wbuf = ck.shape[0], ck.shape[1]
    sb = min(SAMPLE_ATTN_SEQS, n)
    nq = dseq * KV_GROUP
    qspec = pl.BlockSpec((sb, N_KV_HEADS, nq, HEAD_DIM), lambda i: (i, 0, 0, 0))
    nspec = pl.BlockSpec((sb, dseq, D_KV), lambda i: (i, 0, 0))
    cspec = pl.BlockSpec((sb, wbuf, D_KV), lambda i: (i, 0, 0))
    return pl.pallas_call(
        functools.partial(_attn_sample_kernel, layer=layer, dseq=dseq),
        grid=(n // sb,),
        in_specs=[pl.BlockSpec(memory_space=pltpu.SMEM), qspec, nspec, nspec, cspec, cspec],
        out_specs=[qspec, cspec, cspec],
        out_shape=[jax.ShapeDtypeStruct((n, N_KV_HEADS, nq, HEAD_DIM), F32),
                   jax.ShapeDtypeStruct((n, wbuf, D_KV), F32),
                   jax.ShapeDtypeStruct((n, wbuf, D_KV), F32)],
        compiler_params=_cparams("arbitrary"),
        name="attn_sample",
    )(sinks, q, kn, vn, ck, cv)


def _bdot3(a, b):
    dn = (((2,), (2,)), ((0,), (0,)))
    ah, al = _split_bf16(a)
    bh, bl = _split_bf16(b)
    d = lambda x, y: lax.dot_general(x, y, dn, preferred_element_type=F32)
    return d(ah, bh) + d(ah, bl) + d(al, bh)


def _ssm_prep_kernel(are_ref, aim_ref, ldt_ref, bre_ref, bim_ref, cre_ref, cim_ref,
                     r_ref, np_ref, mtp_ref, ns_ref, mts_ref, apw_ref, *, lc, dseq, powers):
    c = SSM_GROUP
    a_re = are_ref[...]
    a_im = aim_ref[...]
    dt = jnp.exp(ldt_ref[...])
    zr = a_re * dt
    zi = a_im * dt

    mag = jnp.exp(zr)
    abr, abi = mag * jnp.cos(zi), mag * jnp.sin(zi)
    pows = {0: (jnp.ones_like(abr), jnp.zeros_like(abi)), 1: (abr, abi)}

    def a_pow(j):
        if j not in pows:
            (pr, pi), (qr, qi) = (a_pow(j - 1), pows[1]) if j <= lc else (a_pow(j // 2),) * 2
            assert j <= lc or j % 2 == 0
            pows[j] = (pr * qr - pi * qi, pr * qi + pi * qr)
        return pows[j]

    xr = abr - 1.0
    den = a_re * a_re + a_im * a_im
    coef_r = (xr * a_re + abi * a_im) / den
    coef_i = (abi * a_re - xr * a_im) / den
    btr = bre_ref[...]
    bti = bim_ref[...]
    bbr = coef_r * btr - coef_i * bti
    bbi = coef_r * bti + coef_i * btr
    cre = cre_ref[...]
    cim = cim_ref[...]
    ns_ref[...] = jnp.zeros_like(ns_ref)
    mts_ref[...] = jnp.zeros_like(mts_ref)
    cars, cais = [], []
    for j in range(lc + 1):
        er, ei = a_pow(j)
        car = cre * er - cim * ei
        cai = cre * ei + cim * er
        if j < lc:
            cars.append(car)
            cais.append(cai)
            nbr = er * bbr - ei * bbi
            nbi = er * bbi + ei * bbr
            ncat = jnp.concatenate([nbr, nbi, nbi, nbr], axis=-1).astype(BF16)
            s = lc - 1 - j
            np_ref[:, s * c:(s + 1) * c, :] = ncat
            if j < dseq:
                s = dseq - 1 - j
                ns_ref[:, s * c:(s + 1) * c, :] = ncat
        if j >= 1:
            mcat = jnp.concatenate([car, -cai], axis=-1).astype(BF16)
            mtp_ref[:, (j - 1) * c:j * c, :] = mcat
            if j <= dseq:
                mts_ref[:, (j - 1) * c:j * c, :] = mcat
    ca_r = jnp.concatenate(cars, axis=1)
    ca_i = jnp.concatenate(cais, axis=1)
    r_ref[...] = _bdot3(bbr, ca_r) - _bdot3(bbi, ca_i)
    for idx, pw in enumerate(powers):
        er, ei = a_pow(pw)
        apw_ref[idx, 0] = jnp.concatenate([er, er], axis=-1)
        apw_ref[idx, 1] = jnp.concatenate([-ei, ei], axis=-1)
        apw_ref[idx, 2] = jnp.concatenate([ei, -ei], axis=-1)


def _ssm_prep_call(a_re, a_im, log_dt, b_re, b_im, c_re, c_im, *, lc, dseq, powers):
    depth = a_re.shape[0]
    g, p, c = N_SSM_GROUPS, SSM_STATE, SSM_GROUP
    npw = len(powers)
    a_spec = pl.BlockSpec((None, g, 1, p), lambda l: (l, 0, 0, 0))
    m_spec = pl.BlockSpec((None, g, c, p), lambda l: (l, 0, 0, 0))
    out4 = lambda a, b: pl.BlockSpec((None, g, a, b), lambda l: (l, 0, 0, 0))
    shape4 = lambda a, b, d: jax.ShapeDtypeStruct((depth, g, a, b), d)
    ws = LANES
    return pl.pallas_call(
        functools.partial(_ssm_prep_kernel, lc=lc, dseq=dseq, powers=powers),
        grid=(depth,),
        in_specs=[a_spec, a_spec, a_spec, m_spec, m_spec, m_spec, m_spec],
        out_specs=[out4(c, lc * c), out4(lc * c, 4 * p), out4(lc * c, 2 * p), out4(ws, 4 * p), out4(ws, 2 * p),
                   pl.BlockSpec((None, npw, 3, g, 1, 2 * p), lambda l: (l, 0, 0, 0, 0, 0))],
        out_shape=[shape4(c, lc * c, F32), shape4(lc * c, 4 * p, BF16), shape4(lc * c, 2 * p, BF16),
                   shape4(ws, 4 * p, BF16), shape4(ws, 2 * p, BF16),
                   jax.ShapeDtypeStruct((depth, npw, 3, g, 1, 2 * p), F32)],
        compiler_params=_cparams("arbitrary"),
        name="ssm_prep",
    )(a_re.reshape(depth, g, 1, p), a_im.reshape(depth, g, 1, p),
      jnp.broadcast_to(log_dt[:, :, None, None], (depth, g, 1, p)),
      jnp.swapaxes(b_re, -1, -2), jnp.swapaxes(b_im, -1, -2), c_re, c_im)


def _ssm_kernel(*refs, nsteps, first_power, nchunk, nseq, has_h0):
    if has_h0:
        u_ref, r_ref, n_ref, mt_ref, apw_ref, h0_ref, h0s_ref, y_ref, hend_ref, toe_scr = refs
    else:
        u_ref, r_ref, n_ref, mt_ref, apw_ref, y_ref, hend_ref, toe_scr = refs
    w = u_ref.shape[1]
    c = SSM_GROUP
    r = r_ref[...]
    lane = lax.broadcasted_iota(jnp.int32, r.shape, 1)
    toe_scr[0:c, :] = r.astype(BF16)
    for s in range(1, w // c):
        toe_scr[s * c:(s + 1) * c, :] = jnp.where(lane >= s * c, pltpu.roll(r, s * c, axis=1), 0.0).astype(BF16)
    uf = u_ref[...]
    y = _dot(uf, toe_scr[...])
    st = _dot(uf, n_ref[...])
    ws = 2 * SSM_STATE
    h = st[:, :ws]
    hs = st[:, ws:]
    kidx = lax.broadcasted_iota(jnp.int32, h.shape, 0) & (nchunk - 1)
    shift = lambda x, d: jnp.where(kidx >= d, pltpu.roll(x, d, axis=0), 0.0)
    for i in range(nsteps):
        d = 1 << i
        pw = first_power + i
        a1, a2, a3 = apw_ref[pw, 0], apw_ref[pw, 1], apw_ref[pw, 2]
        hd = shift(h, d)
        hsd = shift(hs, d)
        h, hs = h + a1 * hd + a2 * hsd, hs + a1 * hsd + a3 * hd
    if has_h0:
        h0 = h0_ref[...]
        hin = h0
        h = h + apw_ref[first_power, 0] * h0 + apw_ref[first_power, 1] * h0s_ref[...]
    else:
        hin = shift(h, 1)
    y_ref[...] = y + lax.dot_general(hin.astype(BF16), mt_ref[...], (((1,), (1,)), ((), ())),
                                     preferred_element_type=F32)
    if nchunk == 1:
        hend_ref[...] = h
    else:
        hend_ref[...] = jnp.concatenate([h[(b + 1) * nchunk - 1:(b + 1) * nchunk, :] for b in range(nseq)], axis=0)


def _ssm_call(uf, r, nmat, mt, apw, h0=None, h0s=None, *, layer, nchunk, nsteps, first_power):
    g, rows, w = uf.shape
    assert nchunk & (nchunk - 1) == 0, "chunks per sequence must be a power of two"
    nseq = rows // nchunk
    npw = apw.shape[1]
    gspec = lambda a, b: pl.BlockSpec((None, a, b), lambda i: (i, 0, 0))
    lgspec = lambda a, b: pl.BlockSpec((None, None, a, b), lambda i: (layer, i, 0, 0))
    in_specs = [gspec(rows, w), lgspec(SSM_GROUP, w), lgspec(w, 4 * SSM_STATE), lgspec(w, 2 * SSM_STATE),
                pl.BlockSpec((None, npw, 3, None, 1, 2 * SSM_STATE), lambda i: (layer, 0, 0, i, 0, 0))]
    args = [uf, r, nmat, mt, apw]
    if h0 is not None:
        in_specs += [gspec(rows, 2 * SSM_STATE)] * 2
        args += [h0, h0s]
    return pl.pallas_call(
        functools.partial(_ssm_kernel, nsteps=nsteps, first_power=first_power, nchunk=nchunk, nseq=nseq,
                          has_h0=h0 is not None),
        grid=(g,),
        in_specs=in_specs,
        out_specs=[gspec(rows, w), gspec(nseq, 2 * SSM_STATE)],
        out_shape=[jax.ShapeDtypeStruct((g, rows, w), F32),
                   jax.ShapeDtypeStruct((g, nseq, 2 * SSM_STATE), F32)],
        scratch_shapes=[pltpu.VMEM((w, w), BF16)],
        compiler_params=_cparams("arbitrary"),
        name="ssm",
    )(*args)


def _softplus(z):
    return jnp.maximum(z, 0.0) + jnp.log1p(jnp.exp(-jnp.abs(z)))


def _lru_gates(xc, wg_ref, bg_ref, lam_ref):
    gl = _dot(xc.astype(BF16), wg_ref[...]) + bg_ref[...]
    r = jax.nn.sigmoid(gl[:, :D_LRU])
    gi = jax.nn.sigmoid(gl[:, D_LRU:])
    log_a = -LRU_C * r * _softplus(-lam_ref[...])
    a = jnp.exp(log_a)
    mult = jnp.sqrt(1.0 - a * a)
    return a, mult, gi


def _lru_prompt_kernel(xr_ref, yg_ref, cw_ref, cb_ref, wg_ref, bg_ref, lam_ref, o_ref, hl_ref,
                       xp_scr, hc_scr):
    t = pl.program_id(1)
    tl = xr_ref.shape[0]
    halo = SUBLANES

    @pl.when(t == 0)
    def _():
        xp_scr[0:halo, :] = jnp.zeros((halo, D_LRU), F32)
        hc_scr[...] = jnp.zeros((1, D_LRU), F32)

    xp_scr[halo:halo + tl, :] = xr_ref[...]
    xc = cb_ref[...]
    for j in range(LRU_CONV):
        off = halo - (LRU_CONV - 1) + j
        xc = xc + cw_ref[j:j + 1, :] * xp_scr[off:off + tl, :]
    a, mult, gi = _lru_gates(xc, wg_ref, bg_ref, lam_ref)
    row = lax.broadcasted_iota(jnp.int32, (tl, D_LRU), 0)
    mult = jnp.where((row == 0) & (t == 0), 1.0, mult)
    b = mult * gi * xc
    nblk = tl // SUBLANES
    a = a.reshape(nblk, SUBLANES, D_LRU)
    b = b.reshape(nblk, SUBLANES, D_LRU)
    sub = lax.broadcasted_iota(jnp.int32, a.shape, 1)
    d = 1
    while d < SUBLANES:
        keep = sub >= d
        a_sh = jnp.where(keep, pltpu.roll(a, d, axis=1), 1.0)
        b_sh = jnp.where(keep, pltpu.roll(b, d, axis=1), 0.0)
        b = a * b_sh + b
        a = a * a_sh
        d *= 2
    carry = hc_scr[...]
    blocks = []
    for blk in range(nblk):
        hb = a[blk] * carry + b[blk]
        blocks.append(hb)
        carry = hb[SUBLANES - 1:SUBLANES, :]
    h = jnp.concatenate(blocks, axis=0)
    o_ref[...] = h * jax.nn.gelu(yg_ref[...])
    last = carry
    hc_scr[...] = last
    hl_ref[...] = jnp.broadcast_to(last, (SUBLANES, D_LRU))
    xp_scr[0:halo, :] = xp_scr[tl:tl + halo, :]


def _lru_weight_specs(layer):
    return [_layer_spec((LRU_CONV, D_LRU), layer), _layer_spec((1, D_LRU), layer),
            _layer_spec((D_LRU, 2 * D_LRU), layer), _layer_spec((1, 2 * D_LRU), layer),
            _layer_spec((1, D_LRU), layer)]


def _lru_weight_args(w):
    return (w["lru_conv_w"], w["lru_conv_b"], w["lru_wg"], w["lru_bg"], w["lru_lambda"])


def _lru_prompt_call(xr, yg, w, *, layer, bsz, seq_len):
    tl = min(LRU_TILE, seq_len)
    nt = seq_len // tl
    row = pl.BlockSpec((tl, D_LRU), lambda b, t: (b * nt + t, 0))
    return pl.pallas_call(
        _lru_prompt_kernel,
        grid=(bsz, nt),
        in_specs=[row, row] + _lru_weight_specs(layer),
        out_specs=[row, pl.BlockSpec((None, SUBLANES, D_LRU), lambda b, t: (b, 0, 0))],
        out_shape=[jax.ShapeDtypeStruct((bsz * seq_len, D_LRU), F32),
                   jax.ShapeDtypeStruct((bsz, SUBLANES, D_LRU), F32)],
        scratch_shapes=[pltpu.VMEM((tl + 2 * SUBLANES, D_LRU), F32), pltpu.VMEM((1, D_LRU), F32)],
        compiler_params=_cparams("arbitrary", "arbitrary"),
        name="lru_prompt",
    )(xr, yg, *_lru_weight_args(w))


def _lru_sample_kernel(xr_ref, yg_ref, buf_ref, h0_ref, cw_ref, cb_ref, wg_ref, bg_ref, lam_ref,
                       o_ref, hl_ref, *, dseq):
    n = h0_ref.shape[0]
    xp = [buf_ref[j] for j in range(LRU_CONV - 1)] + [xr_ref[pl.ds(t * n, n), :] for t in range(dseq)]
    xcs = []
    for t in range(dseq):
        xc = cb_ref[...]
        for j in range(LRU_CONV):
            xc = xc + cw_ref[j:j + 1, :] * xp[t + j]
        xcs.append(xc)
    xc = jnp.concatenate(xcs, axis=0)
    a, mult, gi = _lru_gates(xc, wg_ref, bg_ref, lam_ref)
    b = mult * gi * xc
    h = h0_ref[...]
    for t in range(dseq):
        h = a[t * n:(t + 1) * n] * h + b[t * n:(t + 1) * n]
        o_ref[pl.ds(t * n, n), :] = h * jax.nn.gelu(yg_ref[pl.ds(t * n, n), :])
    hl_ref[...] = h


def _lru_sample_call(xr, yg, buf_tm, h0, w, *, layer, dseq):
    n = h0.shape[0]
    full = lambda a: pl.BlockSpec(a.shape, lambda i: (0,) * a.ndim)
    return pl.pallas_call(
        functools.partial(_lru_sample_kernel, dseq=dseq),
        grid=(1,),
        in_specs=[full(xr), full(yg), full(buf_tm), full(h0)] + _lru_weight_specs(layer),
        out_specs=[pl.BlockSpec((dseq * n, D_LRU), lambda i: (0, 0)), pl.BlockSpec((n, D_LRU), lambda i: (0, 0))],
        out_shape=[jax.ShapeDtypeStruct((dseq * n, D_LRU), F32), jax.ShapeDtypeStruct((n, D_LRU), F32)],
        compiler_params=_cparams("arbitrary"),
        name="lru_sample",
    )(xr, yg, buf_tm, h0, *_lru_weight_args(w))


def _post_kernel(x_ref, oa_ref, ys_ref, u_ref, ol_ref, g1_ref, sh2_ref, sc2_ref, g2_ref,
                 d_ref, wglu_ref, bglu_ref, on_ref, wo_ref, n2_ref, wup_ref, cw_ref, cb_ref, wdn_ref,
                 halo_in_ref, xo_ref, halo_out_ref, h2_scr, acc_scr, halo_scr, ys_scr,
                 *, per_token, row_shift, group_major):
    t = pl.program_id(1)
    tm = x_ref.shape[0]
    hrows = halo_in_ref.shape[0]

    @pl.when(t == 0)
    def _():
        halo_scr[...] = halo_in_ref[...]

    if group_major:
        _from_group_major(ys_ref, ys_scr, ys_ref.shape[1], SSM_CHUNK)
        ys = jnp.concatenate([ys_scr[h] for h in range(SSM_HALVES)], axis=1)
    else:
        ys = ys_ref[...]
    ys = ys + d_ref[...] * u_ref[...]
    gs = jax.nn.gelu(ys)
    o_ssm = gs * jax.nn.sigmoid(_dot(gs.astype(BF16), wglu_ref[...]) + bglu_ref[...])
    on = on_ref[...]
    c1 = D_ATTN
    c2 = c1 + D_SSM
    o = jnp.concatenate([_rms(oa_ref[...], on[:, :c1]), _rms(o_ssm, on[:, c1:c2]),
                         _rms(ol_ref[...], on[:, c2:])], axis=-1)
    x1 = x_ref[...] + _mod_rows(g1_ref, per_token, tm) * _dot(o.astype(BF16), wo_ref[...])
    h2 = _rms(x1, n2_ref[...]) * (1.0 + _mod_rows(sc2_ref, per_token, tm)) + _mod_rows(sh2_ref, per_token, tm)
    h2_scr[...] = h2.astype(BF16)
    rowh = lax.broadcasted_iota(jnp.int32, (hrows, FF_CHUNK), 0)
    msub = tm // FF_ROW_SPLIT

    def ff_cols(c, gv):
        return slice(gv * D_FF + c * FF_CHUNK, gv * D_FF + (c + 1) * FF_CHUNK)

    def up_proj(c):
        return [jnp.concatenate([_dot(h2_scr[r * msub:(r + 1) * msub, :], wup_ref[:, ff_cols(c, gv)])
                                 for r in range(FF_ROW_SPLIT)], axis=0) for gv in range(2)]

    ups = up_proj(0)
    for c in range(N_FF_CHUNKS):
        cur = ups
        if c + 1 < N_FF_CHUNKS:
            ups = up_proj(c + 1)
        halves = []
        for gv in range(2):
            cols = ff_cols(c, gv)
            up = cur[gv]
            halo = halo_scr[:, cols]
            y = cb_ref[:, cols] + cw_ref[FFN_CONV - 1:FFN_CONV, cols] * up
            for back in range(1, FFN_CONV):
                sh = back * row_shift
                r = pltpu.roll(up, sh, axis=0)
                hr = pltpu.roll(halo, sh, axis=0) if sh % hrows else halo
                head = jnp.where(rowh < sh, hr, r[:hrows])
                shifted = jnp.concatenate([head, r[hrows:]], axis=0)
                j = FFN_CONV - 1 - back
                y = y + cw_ref[j:j + 1, cols] * shifted
            halo_scr[:, cols] = up[tm - hrows:, :]
            halves.append(y)
        act = (jax.nn.gelu(halves[0]) * halves[1]).astype(BF16)
        contrib = _dot(act, wdn_ref[c * FF_CHUNK:(c + 1) * FF_CHUNK, :])
        if c == 0:
            acc_scr[...] = contrib
        else:
            acc_scr[...] += contrib
    xo_ref[...] = x1 + _mod_rows(g2_ref, per_token, tm) * acc_scr[...]
    halo_out_ref[...] = halo_scr[...]


def _post_call(x, oa, ys, u, ol, mod, w, halo_in, *, layer, per_token, seq_len, row_shift, mod_seqs=None):
    nt = x.shape[0]
    tm = nt if per_token else min(POST_TILE, nt)
    tiles_per_seq = seq_len // tm
    nseq = nt // seq_len
    hrows = halo_in.shape[1]
    row = lambda wd: pl.BlockSpec((tm, wd), lambda s, t: (s * tiles_per_seq + t, 0))
    mspec = lambda piece: _mod_spec(layer, piece, per_token, lambda s, t: s)(mod_seqs)
    halo_spec = pl.BlockSpec((None, hrows, 2 * D_FF), lambda s, t: (s, 0, 0))
    group_major = ys.ndim == 3
    if group_major:
        ys_spec = pl.BlockSpec((N_SSM_GROUPS, tm // SSM_CHUNK, SSM_CHUNK * SSM_GROUP),
                               lambda s, t: (0, s * tiles_per_seq + t, 0))
    else:
        ys_spec = row(D_SSM)
    return pl.pallas_call(
        functools.partial(_post_kernel, per_token=per_token, row_shift=row_shift, group_major=group_major),
        grid=(nseq, tiles_per_seq),
        in_specs=[
            row(D_MODEL), row(D_ATTN), ys_spec, row(D_SSM), row(D_LRU),
            mspec(2), mspec(3), mspec(4), mspec(5),
            _layer_spec((1, D_SSM), layer), _layer_spec((D_SSM, D_SSM), layer), _layer_spec((1, D_SSM), layer),
            _layer_spec((1, D_MODEL), layer), _layer_spec((D_MODEL, D_MODEL), layer, True),
            _layer_spec((1, D_MODEL), layer),
            _layer_spec((D_MODEL, 2 * D_FF), layer, True),
            _layer_spec((FFN_CONV, 2 * D_FF), layer),
            _layer_spec((1, 2 * D_FF), layer),
            _layer_spec((D_FF, D_MODEL), layer, True),
            halo_spec,
        ],
        out_specs=[row(D_MODEL), halo_spec],
        out_shape=[jax.ShapeDtypeStruct((nt, D_MODEL), F32),
                   jax.ShapeDtypeStruct((nseq, hrows, 2 * D_FF), F32)],
        scratch_shapes=[pltpu.VMEM((tm, D_MODEL), BF16), pltpu.VMEM((tm, D_MODEL), F32),
                        pltpu.VMEM((hrows, 2 * D_FF), F32),
                        pltpu.VMEM((SSM_HALVES, tm, LANES), F32)],
        compiler_params=_cparams("arbitrary", "arbitrary"),
        name="post",
    )(x, oa, ys, u, ol, mod, mod, mod, mod, w["ssm_d"], w["ssm_w_glu"], w["ssm_b_glu"], w["out_norm"],
      w["w_o"], w["norm2"], w["w_up"], w["ffn_conv_w"], w["ffn_conv_b"], w["w_down"], halo_in)


def _block_diag(w):
    depth, nb, bs, _ = w.shape
    eye = jnp.eye(nb, dtype=w.dtype)
    return jnp.einsum("lhij,hk->lhikj", w, eye).reshape(depth, nb * bs, nb * bs)


def _prepare_weights(p):
    depth = p["w_in"].shape[0]
    row = lambda a: a.reshape(depth, 1, -1)
    return dict(
        norm1=row(p["norm1"]), norm2=row(p["norm2"]), out_norm=row(p["out_norm"]),
        w_in=p["w_in"].astype(BF16), w_o=p["w_o"].astype(BF16),
        q_norm=row(jnp.tile(p["q_norm"], (1, N_HEADS))), k_norm=row(jnp.tile(p["k_norm"], (1, N_KV_HEADS))),
        ssm_d=row(p["ssm_d"]), ssm_w_glu=p["ssm_w_glu"].astype(BF16), ssm_b_glu=row(p["ssm_b_glu"]),
        lru_conv_w=p["lru_conv_w"], lru_conv_b=row(p["lru_conv_b"]),
        lru_wg=jnp.concatenate([_block_diag(p["lru_w_a"]), _block_diag(p["lru_w_i"])], axis=-1).astype(BF16),
        lru_bg=row(jnp.concatenate([p["lru_b_a"], p["lru_b_i"]], axis=-1)),
        lru_lambda=row(p["lru_lambda"]),
        w_up=p["ffn_w_up"].astype(BF16), ffn_conv_w=p["ffn_conv_w"], ffn_conv_b=row(p["ffn_conv_b"]),
        w_down=p["ffn_w_down"].astype(BF16),
    )


def _prompt_layer(x, mod, w, seg, sinks, ssm_tabs, cos, sin, *, layer, bsz, seq_len, nsteps):
    nchunk = seq_len // SSM_CHUNK
    q, k, v, u, xr, yg, uf = _pre_call(x, mod, w, seg, cos, sin, layer=layer, per_token=False, seq_len=seq_len)
    oa = _attn_prompt_call(sinks, q, k, v, layer=layer, bsz=bsz, seq_len=seq_len)
    r_all, np_all, mtp_all, _, _, apw = ssm_tabs
    yf, hend = _ssm_call(uf, r_all, np_all, mtp_all, apw, layer=layer, nchunk=nchunk, nsteps=nsteps,
                         first_power=1)
    ol, hl = _lru_prompt_call(xr, yg, w, layer=layer, bsz=bsz, seq_len=seq_len)
    halo0 = jnp.zeros((bsz, SUBLANES, 2 * D_FF), F32)
    x_new, halo = _post_call(x, oa, yf, u, ol, mod, w, halo0, layer=layer, per_token=False, seq_len=seq_len,
                             row_shift=1)
    keep = min(WINDOW, seq_len)
    last = lambda a, nrows: a.reshape(bsz, seq_len, a.shape[-1])[:, seq_len - nrows:]
    hend_b = jnp.transpose(hend, (1, 0, 2))
    states = (last(k, keep).reshape(bsz, keep, N_KV_HEADS, HEAD_DIM),
              last(v, keep).reshape(bsz, keep, N_KV_HEADS, HEAD_DIM),
              hend_b[..., :SSM_STATE], hend_b[..., SSM_STATE:],
              hl[:, 0, :],
              last(xr, LRU_CONV - 1),
              halo[:, SUBLANES - (FFN_CONV - 1):, :])
    return x_new, states


def _sample_layer(x, mod, w, seg, sinks, ssm_tabs, cos, sin, st, *, layer, n, dseq):
    g = N_SSM_GROUPS
    ck, cv, s_re, s_im, lru_h, lru_conv, ffn_conv = st
    nt = n * dseq
    q, k, v, u, xr, yg = _pre_call(x, mod, w, seg, cos, sin, layer=layer, per_token=True, seq_len=nt, nseq=n)
    q_sm = jnp.transpose(q.reshape(dseq, n, N_KV_HEADS, KV_GROUP, HEAD_DIM), (1, 2, 0, 3, 4))
    q_sm = q_sm.reshape(n, N_KV_HEADS, dseq * KV_GROUP, HEAD_DIM)
    kn = jnp.transpose(k.reshape(dseq, n, D_KV), (1, 0, 2))
    vn = jnp.transpose(v.reshape(dseq, n, D_KV), (1, 0, 2))
    wbuf = ck.shape[1]
    o_sm, sk, sv = _attn_sample_call(sinks, q_sm, kn, vn, ck.reshape(n, wbuf, D_KV), cv.reshape(n, wbuf, D_KV),
                                     layer=layer, dseq=dseq)
    oa = jnp.transpose(o_sm.reshape(n, N_KV_HEADS, dseq, KV_GROUP, HEAD_DIM), (2, 0, 1, 3, 4)).reshape(nt, D_ATTN)
    r_all, _, _, ns_all, mts_all, apw = ssm_tabs
    wpad = LANES - dseq * SSM_GROUP
    uf = jnp.transpose(u.reshape(dseq, n, g, SSM_GROUP), (2, 1, 0, 3)).reshape(g, n, dseq * SSM_GROUP)
    uf = jnp.pad(uf, ((0, 0), (0, 0), (0, wpad))).astype(BF16)
    h_re = jnp.transpose(s_re, (1, 0, 2))
    h_im = jnp.transpose(s_im, (1, 0, 2))
    h0 = jnp.concatenate([h_re, h_im], axis=-1)
    h0s = jnp.concatenate([h_im, h_re], axis=-1)
    yf, hend = _ssm_call(uf, r_all, ns_all, mts_all, apw, h0, h0s, layer=layer, nchunk=1, nsteps=0, first_power=0)
    yf = yf[:, :, :dseq * SSM_GROUP]
    ys = jnp.transpose(yf.reshape(g, n, dseq, SSM_GROUP), (2, 1, 0, 3)).reshape(nt, D_SSM)
    ol, hl = _lru_sample_call(xr, yg, jnp.transpose(lru_conv, (1, 0, 2)), lru_h, w, layer=layer, dseq=dseq)
    nconv = FFN_CONV - 1
    halo0 = jnp.transpose(ffn_conv, (1, 0, 2)).reshape(1, nconv * n, 2 * D_FF)
    x_new, halo = _post_call(x, oa, ys, u, ol, mod, w, halo0, layer=layer, per_token=True, seq_len=nt,
                             row_shift=n, mod_seqs=n)
    hend_b = jnp.transpose(hend, (1, 0, 2))
    xr_tm = xr.reshape(dseq, n, D_LRU)
    lru_conv_all = jnp.concatenate([jnp.transpose(lru_conv, (1, 0, 2)), xr_tm], axis=0)
    states = (sk.reshape(n, wbuf, N_KV_HEADS, HEAD_DIM), sv.reshape(n, wbuf, N_KV_HEADS, HEAD_DIM),
              hend_b[..., :SSM_STATE], hend_b[..., SSM_STATE:],
              hl,
              jnp.transpose(lru_conv_all[dseq:], (1, 0, 2)),
              jnp.transpose(halo.reshape(nconv, n, 2 * D_FF), (1, 0, 2)))
    return x_new, states


def kernel(x_prompt, x_sample, cache_k, cache_v, state_ssm_re, state_ssm_im, state_lru_h, state_lru_conv,
           state_ffn_conv, c_prompt, c_sample, w_ada, b_ada, norm1, w_in, q_norm, k_norm, sinks, ssm_a_re,
           ssm_a_im, ssm_b_re, ssm_b_im, ssm_c_re, ssm_c_im, ssm_d, ssm_log_dt, ssm_w_glu, ssm_b_glu,
           lru_conv_w, lru_conv_b, lru_w_a, lru_b_a, lru_w_i, lru_b_i, lru_lambda, out_norm, w_o, norm2,
           ffn_w_up, ffn_conv_w, ffn_conv_b, ffn_w_down):
    bsz, seq_len = x_prompt.shape[:2]
    n, dseq = x_sample.shape[:2]
    depth = w_in.shape[0]
    params = dict(norm1=norm1, w_in=w_in, q_norm=q_norm, k_norm=k_norm, ssm_d=ssm_d,
                  ssm_w_glu=ssm_w_glu, ssm_b_glu=ssm_b_glu, lru_conv_w=lru_conv_w, lru_conv_b=lru_conv_b,
                  lru_w_a=lru_w_a, lru_b_a=lru_b_a, lru_w_i=lru_w_i, lru_b_i=lru_b_i, lru_lambda=lru_lambda,
                  out_norm=out_norm, w_o=w_o, norm2=norm2, ffn_w_up=ffn_w_up, ffn_conv_w=ffn_conv_w,
                  ffn_conv_b=ffn_conv_b, ffn_w_down=ffn_w_down)
    weights = _prepare_weights(params)
    seg = (jnp.arange(D_ATTN)[:, None] // HEAD_DIM == jnp.arange(D_ATTN)[None, :] // HEAD_DIM).astype(BF16)

    rows = n + bsz
    pad = (-rows) % SUBLANES
    c_all = jnp.concatenate([c_sample, c_prompt, jnp.zeros((pad, D_MODEL), F32)], axis=0)
    mod_all = _ada_call(c_all, w_ada, b_ada)
    mod_p = jnp.broadcast_to(mod_all[:, :, n:n + bsz, None, :], (depth, N_MOD, bsz, SUBLANES, D_MODEL))

    cos_p, sin_p = _rope_call(jnp.arange(seq_len, dtype=F32))
    cos_s, sin_s = _rope_call(jnp.repeat(PAST_LEN + jnp.arange(dseq, dtype=F32), n))

    nchunk = seq_len // SSM_CHUNK
    nsteps = max(nchunk - 1, 0).bit_length()
    powers = (dseq,) + tuple(SSM_CHUNK * (1 << i) for i in range(nsteps))
    assert dseq * SSM_GROUP <= LANES
    ssm_tabs = _ssm_prep_call(ssm_a_re, ssm_a_im, ssm_log_dt, ssm_b_re, ssm_b_im, ssm_c_re, ssm_c_im,
                              lc=SSM_CHUNK, dseq=dseq, powers=powers)

    xp = x_prompt.reshape(bsz * seq_len, D_MODEL)
    xs = jnp.transpose(x_sample, (1, 0, 2)).reshape(dseq * n, D_MODEL)
    new_p, new_s = [], []
    for i in range(depth):
        xp, st_p = _prompt_layer(xp, mod_p, weights, seg, sinks, ssm_tabs, cos_p, sin_p, layer=i, bsz=bsz,
                                 seq_len=seq_len, nsteps=nsteps)
        st_in = (cache_k[i], cache_v[i], state_ssm_re[i], state_ssm_im[i], state_lru_h[i], state_lru_conv[i],
                 state_ffn_conv[i])
        xs, st_s = _sample_layer(xs, mod_all, weights, seg, sinks, ssm_tabs, cos_s, sin_s, st_in, layer=i, n=n,
                                 dseq=dseq)
        new_p.append(st_p)
        new_s.append(st_s)
    pk, pv, p_re, p_im, p_lh, p_lc, p_fc = [jnp.stack(s) for s in zip(*new_p)]
    sk, sv, s_re, s_im, s_lh, s_lc, s_fc = [jnp.stack(s) for s in zip(*new_s)]
    y_p = xp.reshape(bsz, seq_len, D_MODEL)
    y_s = jnp.transpose(xs.reshape(dseq, n, D_MODEL), (1, 0, 2))
    return (y_p, y_s, pk, pv, p_re, p_im, p_lh, p_lc, p_fc, sk, sv, s_re, s_im, s_lh, s_lc, s_fc)
```

```python
import functools

import jax
import jax.numpy as jnp
from jax import lax
from jax.experimental import pallas as pl
from jax.experimental.pallas import tpu as pltpu

F32 = jnp.float32
BF16 = jnp.bfloat16

D_MODEL = 1024
HEAD_DIM = 64
N_HEADS = 8
N_KV_HEADS = 2
KV_GROUP = N_HEADS // N_KV_HEADS
D_ATTN = N_HEADS * HEAD_DIM
D_KV = N_KV_HEADS * HEAD_DIM
WINDOW = 128
ROPE_THETA = 10000.0
PAST_LEN = 8192
D_SSM = 256
SSM_GROUP = 16
N_SSM_GROUPS = 16
SSM_STATE = 64
D_LRU = 256
N_LRU_BLOCKS = 4
LRU_BLOCK = 64
LRU_CONV = 4
LRU_C = 8.0
D_FF = 2816
FFN_CONV = 3
D_IN = D_ATTN + 2 * D_KV + D_SSM + 2 * D_LRU
N_MOD = 6
EPS = 1e-6

SUBLANES = 8
LANES = 128
VMEM_LIMIT_BYTES = 56 * 1024 * 1024

TOKEN_TILE = 512
ATTN_BLOCK = WINDOW
ATTN_SUBBLOCKS = 4
SSM_CHUNK = 32
LRU_TILE = 512
FF_CHUNK = 256
N_FF_CHUNKS = D_FF // FF_CHUNK
PRE_ROW_SPLIT = 2
POST_TILE = 512
POST_PART_ROWS = 256
SAMPLE_ATTN_SEQS = 16
SSM_HALVES = D_SSM // LANES
GROUPS_PER_HALF = LANES // SSM_GROUP


def _cparams(*sem):
    return pltpu.CompilerParams(dimension_semantics=sem, vmem_limit_bytes=VMEM_LIMIT_BYTES)


def _dot(a, b):
    return jnp.dot(a, b, preferred_element_type=F32)


def _split_bf16(x):
    hi = x.astype(BF16)
    lo = (x - hi.astype(F32)).astype(BF16)
    return hi, lo


def _layer_spec(shape, layer, single=False):
    nd = len(shape)
    idx = lambda *_: (layer,) + (0,) * nd
    if single:
        return pl.BlockSpec((None,) + tuple(shape), idx, pipeline_mode=pl.Buffered(1))
    return pl.BlockSpec((None,) + tuple(shape), idx)


def _ada_kernel(c_ref, w_ref, b_ref, o_ref):
    c = c_ref[...]
    s = jax.nn.silu(c).astype(BF16)
    o_ref[...] = _dot(s, w_ref[...].astype(BF16)) + b_ref[...]


def _ada_call(c_all, w_ada, b_ada):
    depth = w_ada.shape[0]
    rows = c_all.shape[0]
    return pl.pallas_call(
        _ada_kernel,
        grid=(depth, N_MOD),
        in_specs=[
            pl.BlockSpec((rows, D_MODEL), lambda l, j: (0, 0)),
            pl.BlockSpec((None, D_MODEL, D_MODEL), lambda l, j: (l, 0, j)),
            pl.BlockSpec((None, 1, D_MODEL), lambda l, j: (l, 0, j)),
        ],
        out_specs=pl.BlockSpec((None, None, rows, D_MODEL), lambda l, j: (l, j, 0, 0)),
        out_shape=jax.ShapeDtypeStruct((depth, N_MOD, rows, D_MODEL), F32),
        compiler_params=_cparams("arbitrary", "arbitrary"),
        name="ada",
    )(c_all, w_ada, b_ada.reshape(depth, 1, N_MOD * D_MODEL))


def _rope_kernel(pos_ref, cos_ref, sin_ref):
    pos = pos_ref[...]
    lane = lax.broadcasted_iota(jnp.int32, pos.shape, 1)
    half = HEAD_DIM // 2
    j = (lane & (half - 1)).astype(F32)
    inv = ROPE_THETA ** (-j / half)
    ang = pos * inv
    cos_ref[...] = jnp.cos(ang)
    s = jnp.sin(ang)
    sin_ref[...] = jnp.where((lane & (HEAD_DIM - 1)) < half, -s, s)


def _rope_call(pos_rows):
    t = pos_rows.shape[0]
    pos_b = jnp.broadcast_to(pos_rows[:, None], (t, LANES))
    return pl.pallas_call(
        _rope_kernel,
        out_shape=(jax.ShapeDtypeStruct((t, LANES), F32), jax.ShapeDtypeStruct((t, LANES), F32)),
        name="rope",
    )(pos_b)


def _mod_rows(ref, per_token, tm):
    if not per_token:
        return ref[0:1, :]
    m = ref[...]
    reps = tm // m.shape[0]
    return jnp.concatenate([m] * reps, axis=0) if reps > 1 else m


def _mod_spec(layer, piece, per_token, seq_of_step):
    if per_token:
        return lambda n: pl.BlockSpec((None, None, n, D_MODEL), lambda *g: (layer, piece, 0, 0))
    return lambda n: pl.BlockSpec((None, None, None, SUBLANES, D_MODEL),
                                  lambda *g: (layer, piece, seq_of_step(*g), 0, 0))


def _rms(x, gain):
    return x * lax.rsqrt(jnp.mean(x * x, axis=-1, keepdims=True) + EPS) * gain


def _head_rms(t, seg, gain):
    ss = _dot((t * t).astype(BF16), seg)
    return t * lax.rsqrt(ss * (1.0 / HEAD_DIM) + EPS) * gain


def _rope(t, cos, sin):
    width = t.shape[1]
    reps = width // LANES
    if reps > 1:
        cos = jnp.concatenate([cos] * reps, axis=1)
        sin = jnp.concatenate([sin] * reps, axis=1)
    lane = lax.broadcasted_iota(jnp.int32, t.shape, 1)
    half = HEAD_DIM // 2
    up = pltpu.roll(t, width - half, axis=1)
    dn = pltpu.roll(t, half, axis=1)
    rot = jnp.where((lane & (HEAD_DIM - 1)) < half, up, dn)
    return t * cos + rot * sin


def _unit_transpose8(vs):
    lane = lax.broadcasted_iota(jnp.int32, vs[0].shape, 1)
    unit = lane >> 4
    for b in range(3):
        d = 1 << b
        bit = (unit >> b) & 1
        new = list(vs)
        for i in range(8):
            if (i >> b) & 1 == 0:
                lo, hi = vs[i], vs[i + d]
                new[i] = jnp.where(bit == 0, lo, pltpu.roll(hi, d * SSM_GROUP, axis=1))
                new[i + d] = jnp.where(bit == 1, hi, pltpu.roll(lo, LANES - d * SSM_GROUP, axis=1))
        vs = new
    return vs


def _to_group_major(u_ref, nk, lc):
    outs = [[None] * (lc // 8) for _ in range(N_SSM_GROUPS)]
    for h in range(SSM_HALVES):
        for tb in range(lc // 8):
            vs = [u_ref[h, pl.ds(tb * 8 + tp, nk, stride=lc), :] for tp in range(8)]
            ws = _unit_transpose8(vs)
            for gp in range(GROUPS_PER_HALF):
                outs[h * GROUPS_PER_HALF + gp][tb] = ws[gp]
    return [jnp.concatenate(o, axis=1) for o in outs]


def _from_group_major(yf_ref, ys_ref, nk, lc):
    for h in range(SSM_HALVES):
        for tb in range(lc // 8):
            ws = [yf_ref[h * GROUPS_PER_HALF + gp, :, tb * LANES:(tb + 1) * LANES] for gp in range(GROUPS_PER_HALF)]
            vs = _unit_transpose8(ws)
            for tp in range(8):
                ys_ref[h, pl.ds(tb * 8 + tp, nk, stride=lc), :] = vs[tp]


def _pre_kernel(x_ref, sh_ref, sc_ref, n1_ref, w_ref, qn_ref, kn_ref, seg_ref, cos_ref, sin_ref,
                q_ref, k_ref, v_ref, u_ref, xr_ref, yg_ref, *maybe_uf_ref, per_token):
    tm = x_ref.shape[0]
    parts = PRE_ROW_SPLIT if tm % (PRE_ROW_SPLIT * SSM_CHUNK) == 0 else 1
    tp = tm // parts
    c1 = D_ATTN
    c2 = c1 + D_KV
    c3 = c2 + D_KV
    c4 = c3 + D_SSM
    c5 = c4 + D_LRU
    sc = _mod_rows(sc_ref, per_token, tm)
    sh = _mod_rows(sh_ref, per_token, tm)
    seg = seg_ref[...]

    def project(p):
        rows = slice(p * tp, (p + 1) * tp)
        scp, shp = (sc[rows], sh[rows]) if per_token else (sc, sh)
        hb = (_rms(x_ref[rows, :], n1_ref[...]) * (1.0 + scp) + shp).astype(BF16)
        return _dot(hb, w_ref[...])

    def finish(p, proj):
        rows = slice(p * tp, (p + 1) * tp)
        cos = cos_ref[rows, :]
        sin = sin_ref[rows, :]
        qn = _head_rms(proj[:, :c1], seg, qn_ref[...])
        kn = _head_rms(proj[:, c1:c2], seg[:D_KV, :D_KV], kn_ref[...])
        v_ref[rows, :] = proj[:, c2:c3]
        u_ref[rows, :] = proj[:, c3:c4]
        xr_ref[rows, :] = proj[:, c4:c5]
        yg_ref[rows, :] = proj[:, c5:]
        if maybe_uf_ref:
            u_scr = maybe_uf_ref[1]
            for hh in range(SSM_HALVES):
                u_scr[hh, rows, :] = proj[:, c3 + hh * LANES:c3 + (hh + 1) * LANES]
        q_ref[rows, :] = (_rope(qn, cos, sin) * (HEAD_DIM ** -0.5)).astype(BF16)
        k_ref[rows, :] = _rope(kn, cos, sin)

    nxt = project(0)
    for p in range(parts):
        cur = nxt
        if p + 1 < parts:
            nxt = project(p + 1)
        finish(p, cur)
    if maybe_uf_ref:
        uf_ref, u_scr = maybe_uf_ref
        groups = _to_group_major(u_scr, uf_ref.shape[1], SSM_CHUNK)
        for g in range(N_SSM_GROUPS):
            uf_ref[g] = groups[g].astype(BF16)


def _pre_call(x, mod, w, seg, cos, sin, *, layer, per_token, seq_len, nseq=None):
    nt = x.shape[0]
    tm = min(TOKEN_TILE, nt)
    tiles_per_seq = seq_len // tm
    if per_token:
        tab_spec = pl.BlockSpec((tm, LANES), lambda i: (i, 0))
    else:
        tab_spec = pl.BlockSpec((tm, LANES), lambda i: (i % tiles_per_seq, 0))
    mspec = lambda piece: _mod_spec(layer, piece, per_token, lambda i: i // tiles_per_seq)(nseq)
    row = lambda wd: pl.BlockSpec((tm, wd), lambda i: (i, 0))
    widths = (D_ATTN, D_KV, D_KV, D_SSM, D_LRU, D_LRU)
    dtypes = (BF16, F32, F32, F32, F32, F32)
    out_specs = [row(wd) for wd in widths]
    out_shape = [jax.ShapeDtypeStruct((nt, wd), d) for wd, d in zip(widths, dtypes)]
    scratch = []
    if not per_token:
        nk = tm // SSM_CHUNK
        wf = SSM_CHUNK * SSM_GROUP
        out_specs.append(pl.BlockSpec((N_SSM_GROUPS, nk, wf), lambda i: (0, i, 0)))
        out_shape.append(jax.ShapeDtypeStruct((N_SSM_GROUPS, nt // SSM_CHUNK, wf), BF16))
        scratch.append(pltpu.VMEM((SSM_HALVES, tm, LANES), F32))
    return pl.pallas_call(
        functools.partial(_pre_kernel, per_token=per_token),
        grid=(nt // tm,),
        in_specs=[
            row(D_MODEL), mspec(0), mspec(1),
            _layer_spec((1, D_MODEL), layer),
            _layer_spec((D_MODEL, D_IN), layer, True),
            _layer_spec((1, D_ATTN), layer),
            _layer_spec((1, D_KV), layer),
            pl.BlockSpec((D_ATTN, D_ATTN), lambda i: (0, 0)),
            tab_spec, tab_spec,
        ],
        out_specs=out_specs,
        out_shape=out_shape,
        scratch_shapes=scratch,
        compiler_params=_cparams("arbitrary"),
        name="pre",
    )(x, mod, mod, w["norm1"], w["w_in"], w["q_norm"], w["k_norm"], seg, cos, sin)


def _attn_prompt_kernel(sink_ref, q_ref, kc_ref, kp_ref, vc_ref, vp_ref, o_ref, *, layer, nsub):
    assert (KV_GROUP, N_KV_HEADS, 2 * HEAD_DIM) == (4, 2, LANES)
    i = pl.program_id(1)
    bq = ATTN_BLOCK
    kcat = jnp.concatenate([kp_ref[...], kc_ref[...]], axis=0).astype(BF16)
    vcat = jnp.concatenate([vp_ref[...], vc_ref[...]], axis=0).astype(BF16)
    low = lax.broadcasted_iota(jnp.int32, kcat.shape, 1) < HEAD_DIM

    def half_placed(x):
        zero = jnp.zeros_like(x)
        h0_lo = jnp.where(low, x, zero)
        h1_hi = jnp.where(low, zero, x)
        return [[h0_lo, pltpu.roll(h0_lo, HEAD_DIM, axis=1)], [pltpu.roll(h1_hi, HEAD_DIM, axis=1), h1_hi]]

    kz = half_placed(kcat)
    vz = half_placed(vcat)
    qi = lax.broadcasted_iota(jnp.int32, (bq, 2 * bq), 0)
    si = lax.broadcasted_iota(jnp.int32, (bq, 2 * bq), 1)
    diff = qi + bq - si
    in_window = (diff >= 0) & (diff < WINDOW)
    first_pair = lax.broadcasted_iota(jnp.int32, (2 * bq, 1), 0) < bq
    heads = [(h, e) for h in range(N_KV_HEADS) for e in range(2)]

    def score_block(j):
        band = slice(j * bq, (j + 2) * bq)
        qrows = slice(j * bq, (j + 1) * bq)
        scores = []
        for h, e in heads:
            qh = jnp.concatenate([q_ref[qrows, (2 * h + p) * LANES:(2 * h + p + 1) * LANES] for p in range(2)],
                                 axis=0)
            s = lax.dot_general(qh, kz[h][e][band], (((1,), (1,)), ((), ())), preferred_element_type=F32)
            scores.append(s)
        return scores

    nxt = score_block(0)
    for j in range(nsub):
        scores = nxt
        if j + 1 < nsub:
            nxt = score_block(j + 1)
        if j == 0:
            allowed = in_window & ((si >= bq) | (i > 0))
        else:
            allowed = in_window
        allowed = jnp.concatenate([allowed, allowed], axis=0)
        band = slice(j * bq, (j + 2) * bq)
        qrows = slice(j * bq, (j + 1) * bq)
        probs = []
        for (h, e), s in zip(heads, scores):
            s = jnp.where(allowed, s, -jnp.inf)
            sink = jnp.where(first_pair, sink_ref[layer, KV_GROUP * h + e], sink_ref[layer, KV_GROUP * h + 2 + e])
            m = jnp.maximum(jnp.max(s, axis=-1, keepdims=True), sink)
            p = jnp.exp(s - m)
            denom = jnp.sum(p, axis=-1, keepdims=True) + jnp.exp(sink - m)
            probs.append((p.astype(BF16), denom))
        outs = [_dot(p, vz[h][e][band]) / denom for (h, e), (p, denom) in zip(heads, probs)]
        for h in range(N_KV_HEADS):
            o_pair = outs[2 * h] + outs[2 * h + 1]
            o_ref[qrows, (2 * h) * LANES:(2 * h + 1) * LANES] = o_pair[:bq]
            o_ref[qrows, (2 * h + 1) * LANES:(2 * h + 2) * LANES] = o_pair[bq:]


def _attn_prompt_call(sinks, q, k, v, *, layer, bsz, seq_len):
    nsub = min(ATTN_SUBBLOCKS, seq_len // ATTN_BLOCK)
    bq = nsub * ATTN_BLOCK
    nb = seq_len // bq
    nb_small = seq_len // ATTN_BLOCK
    cur = lambda w: pl.BlockSpec((bq, w), lambda b, i: (b * nb + i, 0))
    prev = lambda w: pl.BlockSpec((ATTN_BLOCK, w), lambda b, i: (b * nb_small + jnp.maximum(i * nsub - 1, 0), 0))
    return pl.pallas_call(
        functools.partial(_attn_prompt_kernel, layer=layer, nsub=nsub),
        grid=(bsz, nb),
        in_specs=[pl.BlockSpec(memory_space=pltpu.SMEM), cur(D_ATTN), cur(D_KV), prev(D_KV), cur(D_KV),
                  prev(D_KV)],
        out_specs=cur(D_ATTN),
        out_shape=jax.ShapeDtypeStruct((bsz * seq_len, D_ATTN), F32),
        compiler_params=_cparams("arbitrary", "arbitrary"),
        name="attn_prompt",
    )(sinks, q, k, k, v, v)


def _attn_sample_kernel(sink_ref, q_ref, kn_ref, vn_ref, ck_ref, cv_ref, o_ref, sk_ref, sv_ref, *, layer, dseq):
    assert N_KV_HEADS == 2 and KV_GROUP == 4
    kk = jnp.concatenate([ck_ref[...], kn_ref[...]], axis=1)
    vv = jnp.concatenate([cv_ref[...], vn_ref[...]], axis=1)
    wbuf = ck_ref.shape[1] // N_KV_HEADS
    sk_ref[...] = kk[:, N_KV_HEADS * dseq:, :]
    sv_ref[...] = vv[:, N_KV_HEADS * dseq:, :]
    nq = dseq * KV_GROUP
    nk = N_KV_HEADS * (wbuf + dseq)
    qrow = lax.broadcasted_iota(jnp.int32, (nq, nk), 0)
    krow = lax.broadcasted_iota(jnp.int32, (nq, nk), 1)
    diff = (qrow >> 2) + wbuf - (krow >> 1)
    in_window = (diff >= 0) & (diff < WINDOW)
    g_of_row = lax.broadcasted_iota(jnp.int32, (nq, 1), 0) & (KV_GROUP - 1)
    kkb = kk.astype(BF16)
    vvb = vv.astype(BF16)
    for h in range(N_KV_HEADS):
        kh = kkb
        vh = vvb
        allowed = (in_window & ((krow & 1) == h))[None]
        qh = q_ref[:, h]
        s = jnp.einsum("sqd,skd->sqk", qh, kh, preferred_element_type=F32)
        s = jnp.where(allowed, s, -jnp.inf)
        sink = jnp.full((nq, 1), sink_ref[layer, h * KV_GROUP], F32)
        for g in range(1, KV_GROUP):
            sink = jnp.where(g_of_row == g, sink_ref[layer, h * KV_GROUP + g], sink)
        sink = sink[None]
        m = jnp.maximum(jnp.max(s, axis=-1, keepdims=True), sink)
        p = jnp.exp(s - m)
        denom = jnp.sum(p, axis=-1, keepdims=True) + jnp.exp(sink - m)
        o = jnp.einsum("sqk,skd->sqd", p.astype(BF16), vh, preferred_element_type=F32) / denom
        o_ref[:, h] = o


def _attn_sample_call(sinks, q, kn, vn, ck_all, cv_all, *, layer, dseq):
    n, rows = ck_all.shape[1], ck_all.shape[2]
    sb = min(SAMPLE_ATTN_SEQS, n)
    nq = dseq * KV_GROUP
    qspec = pl.BlockSpec((sb, N_KV_HEADS, nq, HEAD_DIM), lambda i: (i, 0, 0, 0))
    nspec = pl.BlockSpec((sb, N_KV_HEADS * dseq, HEAD_DIM), lambda i: (i, 0, 0))
    cspec = pl.BlockSpec((None, sb, rows, HEAD_DIM), lambda i: (layer, i, 0, 0))
    return pl.pallas_call(
        functools.partial(_attn_sample_kernel, layer=layer, dseq=dseq),
        grid=(n // sb,),
        in_specs=[pl.BlockSpec(memory_space=pltpu.SMEM), qspec, nspec, nspec, cspec, cspec],
        out_specs=[qspec, cspec, cspec],
        out_shape=[jax.ShapeDtypeStruct((n, N_KV_HEADS, nq, HEAD_DIM), F32),
                   jax.ShapeDtypeStruct(ck_all.shape, F32),
                   jax.ShapeDtypeStruct(cv_all.shape, F32)],
        input_output_aliases={4: 1, 5: 2},
        compiler_params=_cparams("arbitrary"),
        name="attn_sample",
    )(sinks, q, kn, vn, ck_all, cv_all)


def _bdot3(a, b):
    dn = (((2,), (2,)), ((0,), (0,)))
    ah, al = _split_bf16(a)
    bh, bl = _split_bf16(b)
    d = lambda x, y: lax.dot_general(x, y, dn, preferred_element_type=F32)
    return d(ah, bh) + d(ah, bl) + d(al, bh)


def _ssm_prep_kernel(are_ref, aim_ref, ldt_ref, bre_ref, bim_ref, cre_ref, cim_ref,
                     r_ref, np_ref, mtp_ref, ns_ref, mts_ref, apw_ref, *, lc, dseq, powers):
    c = SSM_GROUP
    a_re = are_ref[...]
    a_im = aim_ref[...]
    dt = jnp.exp(ldt_ref[...])
    zr = a_re * dt
    zi = a_im * dt

    mag = jnp.exp(zr)
    abr, abi = mag * jnp.cos(zi), mag * jnp.sin(zi)
    pows = {0: (jnp.ones_like(abr), jnp.zeros_like(abi)), 1: (abr, abi)}

    def a_pow(j):
        if j not in pows:
            (pr, pi), (qr, qi) = (a_pow(j - 1), pows[1]) if j <= lc else (a_pow(j // 2),) * 2
            assert j <= lc or j % 2 == 0
            pows[j] = (pr * qr - pi * qi, pr * qi + pi * qr)
        return pows[j]

    xr = abr - 1.0
    den = a_re * a_re + a_im * a_im
    coef_r = (xr * a_re + abi * a_im) / den
    coef_i = (abi * a_re - xr * a_im) / den
    btr = bre_ref[...]
    bti = bim_ref[...]
    bbr = coef_r * btr - coef_i * bti
    bbi = coef_r * bti + coef_i * btr
    cre = cre_ref[...]
    cim = cim_ref[...]
    ns_ref[...] = jnp.zeros_like(ns_ref)
    mts_ref[...] = jnp.zeros_like(mts_ref)
    cars, cais = [], []
    for j in range(lc + 1):
        er, ei = a_pow(j)
        car = cre * er - cim * ei
        cai = cre * ei + cim * er
        if j < lc:
            cars.append(car)
            cais.append(cai)
            nbr = er * bbr - ei * bbi
            nbi = er * bbi + ei * bbr
            ncat = jnp.concatenate([nbr, nbi, nbi, nbr], axis=-1).astype(BF16)
            s = lc - 1 - j
            np_ref[:, s * c:(s + 1) * c, :] = ncat
            if j < dseq:
                s = dseq - 1 - j
                ns_ref[:, s * c:(s + 1) * c, :] = ncat
        if j >= 1:
            mcat = jnp.concatenate([car, -cai], axis=-1).astype(BF16)
            mtp_ref[:, (j - 1) * c:j * c, :] = mcat
            if j <= dseq:
                mts_ref[:, (j - 1) * c:j * c, :] = mcat
    ca_r = jnp.concatenate(cars, axis=1)
    ca_i = jnp.concatenate(cais, axis=1)
    r_ref[...] = _bdot3(bbr, ca_r) - _bdot3(bbi, ca_i)
    for idx, pw in enumerate(powers):
        er, ei = a_pow(pw)
        apw_ref[idx, 0] = jnp.concatenate([er, er], axis=-1)
        apw_ref[idx, 1] = jnp.concatenate([-ei, ei], axis=-1)
        apw_ref[idx, 2] = jnp.concatenate([ei, -ei], axis=-1)


def _ssm_prep_call(a_re, a_im, log_dt, b_re, b_im, c_re, c_im, *, lc, dseq, powers):
    depth = a_re.shape[0]
    g, p, c = N_SSM_GROUPS, SSM_STATE, SSM_GROUP
    npw = len(powers)
    a_spec = pl.BlockSpec((None, g, 1, p), lambda l: (l, 0, 0, 0))
    m_spec = pl.BlockSpec((None, g, c, p), lambda l: (l, 0, 0, 0))
    out4 = lambda a, b: pl.BlockSpec((None, g, a, b), lambda l: (l, 0, 0, 0))
    shape4 = lambda a, b, d: jax.ShapeDtypeStruct((depth, g, a, b), d)
    ws = LANES
    return pl.pallas_call(
        functools.partial(_ssm_prep_kernel, lc=lc, dseq=dseq, powers=powers),
        grid=(depth,),
        in_specs=[a_spec, a_spec, a_spec, m_spec, m_spec, m_spec, m_spec],
        out_specs=[out4(c, lc * c), out4(lc * c, 4 * p), out4(lc * c, 2 * p), out4(ws, 4 * p), out4(ws, 2 * p),
                   pl.BlockSpec((None, npw, 3, g, 1, 2 * p), lambda l: (l, 0, 0, 0, 0, 0))],
        out_shape=[shape4(c, lc * c, F32), shape4(lc * c, 4 * p, BF16), shape4(lc * c, 2 * p, BF16),
                   shape4(ws, 4 * p, BF16), shape4(ws, 2 * p, BF16),
                   jax.ShapeDtypeStruct((depth, npw, 3, g, 1, 2 * p), F32)],
        compiler_params=_cparams("arbitrary"),
        name="ssm_prep",
    )(a_re.reshape(depth, g, 1, p), a_im.reshape(depth, g, 1, p),
      jnp.broadcast_to(log_dt[:, :, None, None], (depth, g, 1, p)),
      jnp.swapaxes(b_re, -1, -2), jnp.swapaxes(b_im, -1, -2), c_re, c_im)


def _ssm_kernel(*refs, nsteps, first_power, nchunk, nseq, has_h0):
    if has_h0:
        u_ref, r_ref, n_ref, mt_ref, apw_ref, h0_ref, h0s_ref, y_ref, hend_ref, toe_scr = refs
    else:
        u_ref, r_ref, n_ref, mt_ref, apw_ref, y_ref, hend_ref, toe_scr = refs
    w = u_ref.shape[1]
    c = SSM_GROUP
    r = r_ref[...]
    lane = lax.broadcasted_iota(jnp.int32, r.shape, 1)
    toe_scr[0:c, :] = r.astype(BF16)
    for s in range(1, w // c):
        toe_scr[s * c:(s + 1) * c, :] = jnp.where(lane >= s * c, pltpu.roll(r, s * c, axis=1), 0.0).astype(BF16)
    uf = u_ref[...]
    y = _dot(uf, toe_scr[...])
    st = _dot(uf, n_ref[...])
    ws = 2 * SSM_STATE
    h = st[:, :ws]
    hs = st[:, ws:]
    kidx = lax.broadcasted_iota(jnp.int32, h.shape, 0) & (nchunk - 1)
    shift = lambda x, d: jnp.where(kidx >= d, pltpu.roll(x, d, axis=0), 0.0)
    for i in range(nsteps):
        d = 1 << i
        pw = first_power + i
        a1, a2, a3 = apw_ref[pw, 0], apw_ref[pw, 1], apw_ref[pw, 2]
        hd = shift(h, d)
        hsd = shift(hs, d)
        h, hs = h + a1 * hd + a2 * hsd, hs + a1 * hsd + a3 * hd
    if has_h0:
        h0 = h0_ref[...]
        hin = h0
        h = h + apw_ref[first_power, 0] * h0 + apw_ref[first_power, 1] * h0s_ref[...]
    else:
        hin = shift(h, 1)
    y_ref[...] = y + lax.dot_general(hin.astype(BF16), mt_ref[...], (((1,), (1,)), ((), ())),
                                     preferred_element_type=F32)
    if nchunk == 1:
        hend_ref[...] = h
    else:
        hend_ref[...] = jnp.concatenate([h[(b + 1) * nchunk - 1:(b + 1) * nchunk, :] for b in range(nseq)], axis=0)


def _ssm_call(uf, r, nmat, mt, apw, h0=None, h0s=None, *, layer, nchunk, nsteps, first_power):
    g, rows, w = uf.shape
    assert nchunk & (nchunk - 1) == 0, "chunks per sequence must be a power of two"
    nseq = rows // nchunk
    npw = apw.shape[1]
    gspec = lambda a, b: pl.BlockSpec((None, a, b), lambda i: (i, 0, 0))
    lgspec = lambda a, b: pl.BlockSpec((None, None, a, b), lambda i: (layer, i, 0, 0))
    in_specs = [gspec(rows, w), lgspec(SSM_GROUP, w), lgspec(w, 4 * SSM_STATE), lgspec(w, 2 * SSM_STATE),
                pl.BlockSpec((None, npw, 3, None, 1, 2 * SSM_STATE), lambda i: (layer, 0, 0, i, 0, 0))]
    args = [uf, r, nmat, mt, apw]
    if h0 is not None:
        in_specs += [gspec(rows, 2 * SSM_STATE)] * 2
        args += [h0, h0s]
    return pl.pallas_call(
        functools.partial(_ssm_kernel, nsteps=nsteps, first_power=first_power, nchunk=nchunk, nseq=nseq,
                          has_h0=h0 is not None),
        grid=(g,),
        in_specs=in_specs,
        out_specs=[gspec(rows, w), gspec(nseq, 2 * SSM_STATE)],
        out_shape=[jax.ShapeDtypeStruct((g, rows, w), F32),
                   jax.ShapeDtypeStruct((g, nseq, 2 * SSM_STATE), F32)],
        scratch_shapes=[pltpu.VMEM((w, w), BF16)],
        compiler_params=_cparams("arbitrary"),
        name="ssm",
    )(*args)


def _softplus(z):
    return jnp.maximum(z, 0.0) + jnp.log1p(jnp.exp(-jnp.abs(z)))


def _lru_gates(xc, wg_ref, bg_ref, lam_ref):
    gl = _dot(xc.astype(BF16), wg_ref[...]) + bg_ref[...]
    r = jax.nn.sigmoid(gl[:, :D_LRU])
    gi = jax.nn.sigmoid(gl[:, D_LRU:])
    log_a = -LRU_C * r * _softplus(-lam_ref[...])
    a = jnp.exp(log_a)
    mult = jnp.sqrt(1.0 - a * a)
    return a, mult, gi


def _lru_prompt_kernel(xr_ref, yg_ref, cw_ref, cb_ref, wg_ref, bg_ref, lam_ref, o_ref, hl_ref,
                       xp_scr, hc_scr):
    t = pl.program_id(1)
    tl = xr_ref.shape[0]
    halo = SUBLANES

    @pl.when(t == 0)
    def _():
        xp_scr[0:halo, :] = jnp.zeros((halo, D_LRU), F32)
        hc_scr[...] = jnp.zeros((1, D_LRU), F32)

    xp_scr[halo:halo + tl, :] = xr_ref[...]
    xc = cb_ref[...]
    for j in range(LRU_CONV):
        off = halo - (LRU_CONV - 1) + j
        xc = xc + cw_ref[j:j + 1, :] * xp_scr[off:off + tl, :]
    a, mult, gi = _lru_gates(xc, wg_ref, bg_ref, lam_ref)
    row = lax.broadcasted_iota(jnp.int32, (tl, D_LRU), 0)
    mult = jnp.where((row == 0) & (t == 0), 1.0, mult)
    b = mult * gi * xc
    nblk = tl // SUBLANES
    a = a.reshape(nblk, SUBLANES, D_LRU)
    b = b.reshape(nblk, SUBLANES, D_LRU)
    sub = lax.broadcasted_iota(jnp.int32, a.shape, 1)
    d = 1
    while d < SUBLANES:
        keep = sub >= d
        a_sh = jnp.where(keep, pltpu.roll(a, d, axis=1), 1.0)
        b_sh = jnp.where(keep, pltpu.roll(b, d, axis=1), 0.0)
        b = a * b_sh + b
        a = a * a_sh
        d *= 2
    carry = hc_scr[...]
    blocks = []
    for blk in range(nblk):
        hb = a[blk] * carry + b[blk]
        blocks.append(hb)
        carry = hb[SUBLANES - 1:SUBLANES, :]
    h = jnp.concatenate(blocks, axis=0)
    o_ref[...] = h * jax.nn.gelu(yg_ref[...])
    last = carry
    hc_scr[...] = last
    hl_ref[...] = jnp.broadcast_to(last, (SUBLANES, D_LRU))
    xp_scr[0:halo, :] = xp_scr[tl:tl + halo, :]


def _lru_weight_specs(layer):
    return [_layer_spec((LRU_CONV, D_LRU), layer), _layer_spec((1, D_LRU), layer),
            _layer_spec((D_LRU, 2 * D_LRU), layer), _layer_spec((1, 2 * D_LRU), layer),
            _layer_spec((1, D_LRU), layer)]


def _lru_weight_args(w):
    return (w["lru_conv_w"], w["lru_conv_b"], w["lru_wg"], w["lru_bg"], w["lru_lambda"])


def _lru_prompt_call(xr, yg, w, *, layer, bsz, seq_len):
    tl = min(LRU_TILE, seq_len)
    nt = seq_len // tl
    row = pl.BlockSpec((tl, D_LRU), lambda b, t: (b * nt + t, 0))
    return pl.pallas_call(
        _lru_prompt_kernel,
        grid=(bsz, nt),
        in_specs=[row, row] + _lru_weight_specs(layer),
        out_specs=[row, pl.BlockSpec((None, SUBLANES, D_LRU), lambda b, t: (b, 0, 0))],
        out_shape=[jax.ShapeDtypeStruct((bsz * seq_len, D_LRU), F32),
                   jax.ShapeDtypeStruct((bsz, SUBLANES, D_LRU), F32)],
        scratch_shapes=[pltpu.VMEM((tl + 2 * SUBLANES, D_LRU), F32), pltpu.VMEM((1, D_LRU), F32)],
        compiler_params=_cparams("arbitrary", "arbitrary"),
        name="lru_prompt",
    )(xr, yg, *_lru_weight_args(w))


def _lru_sample_kernel(xr_ref, yg_ref, buf_ref, h0_ref, cw_ref, cb_ref, wg_ref, bg_ref, lam_ref,
                       o_ref, hl_ref, *, dseq):
    n = h0_ref.shape[0]
    xp = [buf_ref[j] for j in range(LRU_CONV - 1)] + [xr_ref[pl.ds(t * n, n), :] for t in range(dseq)]
    xcs = []
    for t in range(dseq):
        xc = cb_ref[...]
        for j in range(LRU_CONV):
            xc = xc + cw_ref[j:j + 1, :] * xp[t + j]
        xcs.append(xc)
    xc = jnp.concatenate(xcs, axis=0)
    a, mult, gi = _lru_gates(xc, wg_ref, bg_ref, lam_ref)
    b = mult * gi * xc
    h = h0_ref[...]
    for t in range(dseq):
        h = a[t * n:(t + 1) * n] * h + b[t * n:(t + 1) * n]
        o_ref[pl.ds(t * n, n), :] = h * jax.nn.gelu(yg_ref[pl.ds(t * n, n), :])
    hl_ref[...] = h


def _lru_sample_call(xr, yg, buf_tm, h0, w, *, layer, dseq):
    n = h0.shape[0]
    full = lambda a: pl.BlockSpec(a.shape, lambda i: (0,) * a.ndim)
    return pl.pallas_call(
        functools.partial(_lru_sample_kernel, dseq=dseq),
        grid=(1,),
        in_specs=[full(xr), full(yg), full(buf_tm), full(h0)] + _lru_weight_specs(layer),
        out_specs=[pl.BlockSpec((dseq * n, D_LRU), lambda i: (0, 0)), pl.BlockSpec((n, D_LRU), lambda i: (0, 0))],
        out_shape=[jax.ShapeDtypeStruct((dseq * n, D_LRU), F32), jax.ShapeDtypeStruct((n, D_LRU), F32)],
        compiler_params=_cparams("arbitrary"),
        name="lru_sample",
    )(xr, yg, buf_tm, h0, *_lru_weight_args(w))


def _post_kernel(x_ref, oa_ref, ys_ref, u_ref, ol_ref, g1_ref, sh2_ref, sc2_ref, g2_ref,
                 d_ref, wglu_ref, bglu_ref, on_ref, wo_ref, n2_ref, wup_ref, cw_ref, cb_ref, wdn_ref,
                 halo_in_ref, xo_ref, halo_out_ref, h2_scr, acc_scr, halo_scr, ys_scr,
                 *, per_token, row_shift, group_major):
    t = pl.program_id(1)
    tm = x_ref.shape[0]
    hrows = halo_in_ref.shape[0]

    @pl.when(t == 0)
    def _():
        halo_scr[...] = halo_in_ref[...]

    parts = 1 if per_token else max(1, tm // POST_PART_ROWS)
    tp = tm // parts
    if group_major:
        _from_group_major(ys_ref, ys_scr, ys_ref.shape[1], SSM_CHUNK)
    on = on_ref[...]
    c1 = D_ATTN
    c2 = c1 + D_SSM
    g1 = _mod_rows(g1_ref, per_token, tm)
    g2 = _mod_rows(g2_ref, per_token, tm)
    sc2 = _mod_rows(sc2_ref, per_token, tm)
    sh2 = _mod_rows(sh2_ref, per_token, tm)

    def prologue(p):
        rows = slice(p * tp, (p + 1) * tp)
        if group_major:
            ys = jnp.concatenate([ys_scr[h, rows, :] for h in range(SSM_HALVES)], axis=1)
        else:
            ys = ys_ref[rows, :]
        ys = ys + d_ref[...] * u_ref[rows, :]
        gs = jax.nn.gelu(ys)
        o_ssm = gs * jax.nn.sigmoid(_dot(gs.astype(BF16), wglu_ref[...]) + bglu_ref[...])
        o = jnp.concatenate([_rms(oa_ref[rows, :], on[:, :c1]), _rms(o_ssm, on[:, c1:c2]),
                             _rms(ol_ref[rows, :], on[:, c2:])], axis=-1).astype(BF16)
        mix = jnp.concatenate([_dot(o, wo_ref[:, j * FF_CHUNK:(j + 1) * FF_CHUNK])
                               for j in range(D_MODEL // FF_CHUNK)], axis=1)
        x1 = x_ref[rows, :] + g1 * mix
        h2 = _rms(x1, n2_ref[...]) * (1.0 + sc2) + sh2
        h2_scr[rows, :] = h2.astype(BF16)
        return x1

    rowh = lax.broadcasted_iota(jnp.int32, (hrows, FF_CHUNK), 0)

    def ff_cols(c, gv):
        return slice(gv * D_FF + c * FF_CHUNK, gv * D_FF + (c + 1) * FF_CHUNK)

    def up_proj(p, c):
        return [_dot(h2_scr[p * tp:(p + 1) * tp, :], wup_ref[:, ff_cols(c, gv)]) for gv in range(2)]

    x1s = {0: prologue(0)}
    ups = up_proj(0, 0)
    tails = None
    for p in range(parts):
        rows = slice(p * tp, (p + 1) * tp)
        new_tails = []
        for c in range(N_FF_CHUNKS):
            cur = ups
            if c + 1 < N_FF_CHUNKS:
                ups = up_proj(p, c + 1)
            elif p + 1 < parts:
                ups = up_proj(p + 1, 0)
            if c == 0 and p + 1 < parts:
                x1s[p + 1] = prologue(p + 1)
            halves = []
            for gv in range(2):
                cols = ff_cols(c, gv)
                up = cur[gv]
                halo = halo_scr[:, cols] if p == 0 else tails[c][gv]
                y = cb_ref[:, cols] + cw_ref[FFN_CONV - 1:FFN_CONV, cols] * up
                for back in range(1, FFN_CONV):
                    sh = back * row_shift
                    r = pltpu.roll(up, sh, axis=0)
                    hr = pltpu.roll(halo, sh, axis=0) if sh % hrows else halo
                    head = jnp.where(rowh < sh, hr, r[:hrows])
                    shifted = jnp.concatenate([head, r[hrows:]], axis=0)
                    j = FFN_CONV - 1 - back
                    y = y + cw_ref[j:j + 1, cols] * shifted
                tail = up[tp - hrows:, :]
                if p + 1 == parts:
                    halo_scr[:, cols] = tail
                halves.append((y, tail))
            new_tails.append([halves[0][1], halves[1][1]])
            act = (jax.nn.gelu(halves[0][0]) * halves[1][0]).astype(BF16)
            contrib = _dot(act, wdn_ref[c * FF_CHUNK:(c + 1) * FF_CHUNK, :])
            if c == 0:
                acc_scr[rows, :] = contrib
            else:
                acc_scr[rows, :] += contrib
        tails = new_tails
        xo_ref[rows, :] = x1s[p] + g2 * acc_scr[rows, :]
    halo_out_ref[...] = halo_scr[...]


def _post_call(x, oa, ys, u, ol, mod, w, halo_in, *, layer, per_token, seq_len, row_shift, mod_seqs=None):
    nt = x.shape[0]
    tm = nt if per_token else min(POST_TILE, nt)
    tiles_per_seq = seq_len // tm
    nseq = nt // seq_len
    hrows = halo_in.shape[1]
    row = lambda wd: pl.BlockSpec((tm, wd), lambda s, t: (s * tiles_per_seq + t, 0))
    mspec = lambda piece: _mod_spec(layer, piece, per_token, lambda s, t: s)(mod_seqs)
    halo_spec = pl.BlockSpec((None, hrows, 2 * D_FF), lambda s, t: (s, 0, 0))
    group_major = ys.ndim == 3
    if group_major:
        ys_spec = pl.BlockSpec((N_SSM_GROUPS, tm // SSM_CHUNK, SSM_CHUNK * SSM_GROUP),
                               lambda s, t: (0, s * tiles_per_seq + t, 0))
    else:
        ys_spec = row(D_SSM)
    return pl.pallas_call(
        functools.partial(_post_kernel, per_token=per_token, row_shift=row_shift, group_major=group_major),
        grid=(nseq, tiles_per_seq),
        in_specs=[
            row(D_MODEL), row(D_ATTN), ys_spec, row(D_SSM), row(D_LRU),
            mspec(2), mspec(3), mspec(4), mspec(5),
            _layer_spec((1, D_SSM), layer), _layer_spec((D_SSM, D_SSM), layer), _layer_spec((1, D_SSM), layer),
            _layer_spec((1, D_MODEL), layer), _layer_spec((D_MODEL, D_MODEL), layer, True),
            _layer_spec((1, D_MODEL), layer),
            _layer_spec((D_MODEL, 2 * D_FF), layer, True),
            _layer_spec((FFN_CONV, 2 * D_FF), layer),
            _layer_spec((1, 2 * D_FF), layer),
            _layer_spec((D_FF, D_MODEL), layer, True),
            halo_spec,
        ],
        out_specs=[row(D_MODEL), halo_spec],
        out_shape=[jax.ShapeDtypeStruct((nt, D_MODEL), F32),
                   jax.ShapeDtypeStruct((nseq, hrows, 2 * D_FF), F32)],
        scratch_shapes=[pltpu.VMEM((tm, D_MODEL), BF16), pltpu.VMEM((tm, D_MODEL), F32),
                        pltpu.VMEM((hrows, 2 * D_FF), F32),
                        pltpu.VMEM((SSM_HALVES, tm, LANES), F32)],
        compiler_params=_cparams("arbitrary", "arbitrary"),
        name="post",
    )(x, oa, ys, u, ol, mod, mod, mod, mod, w["ssm_d"], w["ssm_w_glu"], w["ssm_b_glu"], w["out_norm"],
      w["w_o"], w["norm2"], w["w_up"], w["ffn_conv_w"], w["ffn_conv_b"], w["w_down"], halo_in)


def _block_diag(w):
    depth, nb, bs, _ = w.shape
    eye = jnp.eye(nb, dtype=w.dtype)
    return jnp.einsum("lhij,hk->lhikj", w, eye).reshape(depth, nb * bs, nb * bs)


def _prepare_weights(p):
    depth = p["w_in"].shape[0]
    row = lambda a: a.reshape(depth, 1, -1)
    return dict(
        norm1=row(p["norm1"]), norm2=row(p["norm2"]), out_norm=row(p["out_norm"]),
        w_in=p["w_in"].astype(BF16), w_o=p["w_o"].astype(BF16),
        q_norm=row(jnp.tile(p["q_norm"], (1, N_HEADS))), k_norm=row(jnp.tile(p["k_norm"], (1, N_KV_HEADS))),
        ssm_d=row(p["ssm_d"]), ssm_w_glu=p["ssm_w_glu"].astype(BF16), ssm_b_glu=row(p["ssm_b_glu"]),
        lru_conv_w=p["lru_conv_w"], lru_conv_b=row(p["lru_conv_b"]),
        lru_wg=jnp.concatenate([_block_diag(p["lru_w_a"]), _block_diag(p["lru_w_i"])], axis=-1).astype(BF16),
        lru_bg=row(jnp.concatenate([p["lru_b_a"], p["lru_b_i"]], axis=-1)),
        lru_lambda=row(p["lru_lambda"]),
        w_up=p["ffn_w_up"].astype(BF16), ffn_conv_w=p["ffn_conv_w"], ffn_conv_b=row(p["ffn_conv_b"]),
        w_down=p["ffn_w_down"].astype(BF16),
    )


def _prompt_layer(x, mod, w, seg, sinks, ssm_tabs, cos, sin, *, layer, bsz, seq_len, nsteps):
    nchunk = seq_len // SSM_CHUNK
    q, k, v, u, xr, yg, uf = _pre_call(x, mod, w, seg, cos, sin, layer=layer, per_token=False, seq_len=seq_len)
    oa = _attn_prompt_call(sinks, q, k, v, layer=layer, bsz=bsz, seq_len=seq_len)
    r_all, np_all, mtp_all, _, _, apw = ssm_tabs
    yf, hend = _ssm_call(uf, r_all, np_all, mtp_all, apw, layer=layer, nchunk=nchunk, nsteps=nsteps,
                         first_power=1)
    ol, hl = _lru_prompt_call(xr, yg, w, layer=layer, bsz=bsz, seq_len=seq_len)
    halo0 = jnp.zeros((bsz, SUBLANES, 2 * D_FF), F32)
    x_new, halo = _post_call(x, oa, yf, u, ol, mod, w, halo0, layer=layer, per_token=False, seq_len=seq_len,
                             row_shift=1)
    keep = min(WINDOW, seq_len)
    last = lambda a, nrows: a.reshape(bsz, seq_len, a.shape[-1])[:, seq_len - nrows:]
    hend_b = jnp.transpose(hend, (1, 0, 2))
    states = (last(k, keep).reshape(bsz, keep, N_KV_HEADS, HEAD_DIM),
              last(v, keep).reshape(bsz, keep, N_KV_HEADS, HEAD_DIM),
              hend_b[..., :SSM_STATE], hend_b[..., SSM_STATE:],
              hl[:, 0, :],
              last(xr, LRU_CONV - 1),
              halo[:, SUBLANES - (FFN_CONV - 1):, :])
    return x_new, states


def _sample_layer(x, mod, w, seg, sinks, ssm_tabs, cos, sin, st, *, layer, n, dseq):
    g = N_SSM_GROUPS
    ck, cv, s_re, s_im, lru_h, lru_conv, ffn_conv = st
    nt = n * dseq
    q, k, v, u, xr, yg = _pre_call(x, mod, w, seg, cos, sin, layer=layer, per_token=True, seq_len=nt, nseq=n)
    q_sm = jnp.transpose(q.reshape(dseq, n, N_KV_HEADS, KV_GROUP, HEAD_DIM), (1, 2, 0, 3, 4))
    q_sm = q_sm.reshape(n, N_KV_HEADS, dseq * KV_GROUP, HEAD_DIM)
    new_rows = lambda a: jnp.transpose(a.reshape(dseq, n, N_KV_HEADS, HEAD_DIM), (1, 0, 2, 3)).reshape(
        n, dseq * N_KV_HEADS, HEAD_DIM)
    o_sm, ck, cv = _attn_sample_call(sinks, q_sm, new_rows(k), new_rows(v), ck, cv, layer=layer, dseq=dseq)
    oa = jnp.transpose(o_sm.reshape(n, N_KV_HEADS, dseq, KV_GROUP, HEAD_DIM), (2, 0, 1, 3, 4)).reshape(nt, D_ATTN)
    r_all, _, _, ns_all, mts_all, apw = ssm_tabs
    wpad = LANES - dseq * SSM_GROUP
    uf = jnp.transpose(u.reshape(dseq, n, g, SSM_GROUP), (2, 1, 0, 3)).reshape(g, n, dseq * SSM_GROUP)
    uf = jnp.pad(uf, ((0, 0), (0, 0), (0, wpad))).astype(BF16)
    h_re = jnp.transpose(s_re, (1, 0, 2))
    h_im = jnp.transpose(s_im, (1, 0, 2))
    h0 = jnp.concatenate([h_re, h_im], axis=-1)
    h0s = jnp.concatenate([h_im, h_re], axis=-1)
    yf, hend = _ssm_call(uf, r_all, ns_all, mts_all, apw, h0, h0s, layer=layer, nchunk=1, nsteps=0, first_power=0)
    yf = yf[:, :, :dseq * SSM_GROUP]
    ys = jnp.transpose(yf.reshape(g, n, dseq, SSM_GROUP), (2, 1, 0, 3)).reshape(nt, D_SSM)
    ol, hl = _lru_sample_call(xr, yg, jnp.transpose(lru_conv, (1, 0, 2)), lru_h, w, layer=layer, dseq=dseq)
    nconv = FFN_CONV - 1
    halo0 = jnp.transpose(ffn_conv, (1, 0, 2)).reshape(1, nconv * n, 2 * D_FF)
    x_new, halo = _post_call(x, oa, ys, u, ol, mod, w, halo0, layer=layer, per_token=True, seq_len=nt,
                             row_shift=n, mod_seqs=n)
    hend_b = jnp.transpose(hend, (1, 0, 2))
    xr_tm = xr.reshape(dseq, n, D_LRU)
    lru_conv_all = jnp.concatenate([jnp.transpose(lru_conv, (1, 0, 2)), xr_tm], axis=0)
    states = (hend_b[..., :SSM_STATE], hend_b[..., SSM_STATE:],
              hl,
              jnp.transpose(lru_conv_all[dseq:], (1, 0, 2)),
              jnp.transpose(halo.reshape(nconv, n, 2 * D_FF), (1, 0, 2)))
    return x_new, ck, cv, states


def kernel(x_prompt, x_sample, cache_k, cache_v, state_ssm_re, state_ssm_im, state_lru_h, state_lru_conv,
           state_ffn_conv, c_prompt, c_sample, w_ada, b_ada, norm1, w_in, q_norm, k_norm, sinks, ssm_a_re,
           ssm_a_im, ssm_b_re, ssm_b_im, ssm_c_re, ssm_c_im, ssm_d, ssm_log_dt, ssm_w_glu, ssm_b_glu,
           lru_conv_w, lru_conv_b, lru_w_a, lru_b_a, lru_w_i, lru_b_i, lru_lambda, out_norm, w_o, norm2,
           ffn_w_up, ffn_conv_w, ffn_conv_b, ffn_w_down):
    bsz, seq_len = x_prompt.shape[:2]
    n, dseq = x_sample.shape[:2]
    depth = w_in.shape[0]
    params = dict(norm1=norm1, w_in=w_in, q_norm=q_norm, k_norm=k_norm, ssm_d=ssm_d,
                  ssm_w_glu=ssm_w_glu, ssm_b_glu=ssm_b_glu, lru_conv_w=lru_conv_w, lru_conv_b=lru_conv_b,
                  lru_w_a=lru_w_a, lru_b_a=lru_b_a, lru_w_i=lru_w_i, lru_b_i=lru_b_i, lru_lambda=lru_lambda,
                  out_norm=out_norm, w_o=w_o, norm2=norm2, ffn_w_up=ffn_w_up, ffn_conv_w=ffn_conv_w,
                  ffn_conv_b=ffn_conv_b, ffn_w_down=ffn_w_down)
    weights = _prepare_weights(params)
    seg = (jnp.arange(D_ATTN)[:, None] // HEAD_DIM == jnp.arange(D_ATTN)[None, :] // HEAD_DIM).astype(BF16)

    rows = n + bsz
    pad = (-rows) % SUBLANES
    c_all = jnp.concatenate([c_sample, c_prompt, jnp.zeros((pad, D_MODEL), F32)], axis=0)
    mod_all = _ada_call(c_all, w_ada, b_ada)
    mod_p = jnp.broadcast_to(mod_all[:, :, n:n + bsz, None, :], (depth, N_MOD, bsz, SUBLANES, D_MODEL))

    cos_p, sin_p = _rope_call(jnp.arange(seq_len, dtype=F32))
    cos_s, sin_s = _rope_call(jnp.repeat(PAST_LEN + jnp.arange(dseq, dtype=F32), n))

    nchunk = seq_len // SSM_CHUNK
    nsteps = max(nchunk - 1, 0).bit_length()
    powers = (dseq,) + tuple(SSM_CHUNK * (1 << i) for i in range(nsteps))
    assert dseq * SSM_GROUP <= LANES
    ssm_tabs = _ssm_prep_call(ssm_a_re, ssm_a_im, ssm_log_dt, ssm_b_re, ssm_b_im, ssm_c_re, ssm_c_im,
                              lc=SSM_CHUNK, dseq=dseq, powers=powers)

    xp = x_prompt.reshape(bsz * seq_len, D_MODEL)
    xs = jnp.transpose(x_sample, (1, 0, 2)).reshape(dseq * n, D_MODEL)
    wbuf = cache_k.shape[2]
    ck = cache_k.reshape(depth, n, wbuf * N_KV_HEADS, HEAD_DIM)
    cv = cache_v.reshape(depth, n, wbuf * N_KV_HEADS, HEAD_DIM)
    new_p, new_s = [], []
    for i in range(depth):
        xp, st_p = _prompt_layer(xp, mod_p, weights, seg, sinks, ssm_tabs, cos_p, sin_p, layer=i, bsz=bsz,
                                 seq_len=seq_len, nsteps=nsteps)
        st_in = (ck, cv, state_ssm_re[i], state_ssm_im[i], state_lru_h[i], state_lru_conv[i], state_ffn_conv[i])
        xs, ck, cv, st_s = _sample_layer(xs, mod_all, weights, seg, sinks, ssm_tabs, cos_s, sin_s, st_in, layer=i,
                                         n=n, dseq=dseq)
        new_p.append(st_p)
        new_s.append(st_s)
    pk, pv, p_re, p_im, p_lh, p_lc, p_fc = [jnp.stack(s) for s in zip(*new_p)]
    s_re, s_im, s_lh, s_lc, s_fc = [jnp.stack(s) for s in zip(*new_s)]
    sk = ck.reshape(depth, n, wbuf, N_KV_HEADS, HEAD_DIM)
    sv = cv.reshape(depth, n, wbuf, N_KV_HEADS, HEAD_DIM)
    y_p = xp.reshape(bsz, seq_len, D_MODEL)
    y_s = jnp.transpose(xs.reshape(dseq, n, D_MODEL), (1, 0, 2))
    return (y_p, y_s, pk, pv, p_re, p_im, p_lh, p_lc, p_fc, sk, sv, s_re, s_im, s_lh, s_lc, s_fc)
```

```python
import functools

import jax
import jax.numpy as jnp
from jax import lax
from jax.experimental import pallas as pl
from jax.experimental.pallas import tpu as pltpu

F32 = jnp.float32
BF16 = jnp.bfloat16

D_MODEL = 1024
HEAD_DIM = 64
N_HEADS = 8
N_KV_HEADS = 2
KV_GROUP = N_HEADS // N_KV_HEADS
D_ATTN = N_HEADS * HEAD_DIM
D_KV = N_KV_HEADS * HEAD_DIM
WINDOW = 128
ROPE_THETA = 10000.0
PAST_LEN = 8192
D_SSM = 256
SSM_GROUP = 16
N_SSM_GROUPS = 16
SSM_STATE = 64
D_LRU = 256
N_LRU_BLOCKS = 4
LRU_BLOCK = 64
LRU_CONV = 4
LRU_C = 8.0
D_FF = 2816
FFN_CONV = 3
D_IN = D_ATTN + 2 * D_KV + D_SSM + 2 * D_LRU
N_MOD = 6
EPS = 1e-6

SUBLANES = 8
LANES = 128
VMEM_LIMIT_BYTES = 56 * 1024 * 1024

TOKEN_TILE = 512
ATTN_BLOCK = WINDOW
ATTN_SUBBLOCKS = 4
SSM_CHUNK = 32
LRU_TILE = 512
FF_CHUNK = 256
N_FF_CHUNKS = D_FF // FF_CHUNK
PRE_ROW_SPLIT = 2
POST_TILE = 256
SAMPLE_ATTN_SEQS = 16
ADA_PIECES = 2
SSM_HALVES = D_SSM // LANES
GROUPS_PER_HALF = LANES // SSM_GROUP


def _cparams(*sem):
    return pltpu.CompilerParams(dimension_semantics=sem, vmem_limit_bytes=VMEM_LIMIT_BYTES)


def _dot(a, b):
    return jnp.dot(a, b, preferred_element_type=F32)


def _split_bf16(x):
    hi = x.astype(BF16)
    lo = (x - hi.astype(F32)).astype(BF16)
    return hi, lo


def _layer_spec(shape, layer, single=False):
    nd = len(shape)
    idx = lambda *_: (layer,) + (0,) * nd
    if single:
        return pl.BlockSpec((None,) + tuple(shape), idx, pipeline_mode=pl.Buffered(1))
    return pl.BlockSpec((None,) + tuple(shape), idx)


def _ada_kernel(c_ref, w_ref, b_ref, o_ref):
    c = c_ref[...]
    s = jax.nn.silu(c).astype(BF16)
    res = _dot(s, w_ref[...].astype(BF16)) + b_ref[...]
    for j in range(ADA_PIECES):
        o_ref[j] = res[:, j * D_MODEL:(j + 1) * D_MODEL]


def _ada_call(c_all, w_ada, b_ada):
    depth = w_ada.shape[0]
    rows = c_all.shape[0]
    cols = ADA_PIECES * D_MODEL
    return pl.pallas_call(
        _ada_kernel,
        grid=(depth, N_MOD // ADA_PIECES),
        in_specs=[
            pl.BlockSpec((rows, D_MODEL), lambda l, j: (0, 0)),
            pl.BlockSpec((None, D_MODEL, cols), lambda l, j: (l, 0, j)),
            pl.BlockSpec((None, 1, cols), lambda l, j: (l, 0, j)),
        ],
        out_specs=pl.BlockSpec((None, ADA_PIECES, rows, D_MODEL), lambda l, j: (l, j, 0, 0)),
        out_shape=jax.ShapeDtypeStruct((depth, N_MOD, rows, D_MODEL), F32),
        compiler_params=_cparams("arbitrary", "arbitrary"),
        name="ada",
    )(c_all, w_ada, b_ada.reshape(depth, 1, N_MOD * D_MODEL))


def _rope_kernel(pos_ref, cos_ref, sin_ref):
    pos = pos_ref[...]
    lane = lax.broadcasted_iota(jnp.int32, pos.shape, 1)
    half = HEAD_DIM // 2
    j = (lane & (half - 1)).astype(F32)
    inv = ROPE_THETA ** (-j / half)
    ang = pos * inv
    cos_ref[...] = jnp.cos(ang)
    s = jnp.sin(ang)
    sin_ref[...] = jnp.where((lane & (HEAD_DIM - 1)) < half, -s, s)


def _rope_call(pos_rows):
    t = pos_rows.shape[0]
    pos_b = jnp.broadcast_to(pos_rows[:, None], (t, LANES))
    return pl.pallas_call(
        _rope_kernel,
        out_shape=(jax.ShapeDtypeStruct((t, LANES), F32), jax.ShapeDtypeStruct((t, LANES), F32)),
        name="rope",
    )(pos_b)


def _mod_rows(ref, per_token, tm):
    if not per_token:
        return ref[0:1, :]
    m = ref[...]
    reps = tm // m.shape[0]
    return jnp.concatenate([m] * reps, axis=0) if reps > 1 else m


def _mod_spec(layer, piece, per_token, seq_of_step):
    if per_token:
        return lambda n: pl.BlockSpec((None, None, n, D_MODEL), lambda *g: (layer, piece, 0, 0))
    return lambda n: pl.BlockSpec((None, None, None, SUBLANES, D_MODEL),
                                  lambda *g: (layer, piece, seq_of_step(*g), 0, 0))


def _rms(x, gain):
    return x * lax.rsqrt(jnp.mean(x * x, axis=-1, keepdims=True) + EPS) * gain


def _head_rms(t, seg, gain):
    ss = _dot((t * t).astype(BF16), seg)
    return t * lax.rsqrt(ss * (1.0 / HEAD_DIM) + EPS) * gain


def _rope(t, cos, sin):
    width = t.shape[1]
    reps = width // LANES
    if reps > 1:
        cos = jnp.concatenate([cos] * reps, axis=1)
        sin = jnp.concatenate([sin] * reps, axis=1)
    lane = lax.broadcasted_iota(jnp.int32, t.shape, 1)
    half = HEAD_DIM // 2
    up = pltpu.roll(t, width - half, axis=1)
    dn = pltpu.roll(t, half, axis=1)
    rot = jnp.where((lane & (HEAD_DIM - 1)) < half, up, dn)
    return t * cos + rot * sin


def _unit_transpose8(vs):
    lane = lax.broadcasted_iota(jnp.int32, vs[0].shape, 1)
    unit = lane >> 4
    for b in range(3):
        d = 1 << b
        bit = (unit >> b) & 1
        new = list(vs)
        for i in range(8):
            if (i >> b) & 1 == 0:
                lo, hi = vs[i], vs[i + d]
                new[i] = jnp.where(bit == 0, lo, pltpu.roll(hi, d * SSM_GROUP, axis=1))
                new[i + d] = jnp.where(bit == 1, hi, pltpu.roll(lo, LANES - d * SSM_GROUP, axis=1))
        vs = new
    return vs


def _to_group_major(u_ref, nk, lc):
    outs = [[None] * (lc // 8) for _ in range(N_SSM_GROUPS)]
    for h in range(SSM_HALVES):
        for tb in range(lc // 8):
            vs = [u_ref[h, pl.ds(tb * 8 + tp, nk, stride=lc), :] for tp in range(8)]
            ws = _unit_transpose8(vs)
            for gp in range(GROUPS_PER_HALF):
                outs[h * GROUPS_PER_HALF + gp][tb] = ws[gp]
    return [jnp.concatenate(o, axis=1) for o in outs]


def _from_group_major(yf_ref, ys_ref, nk, lc):
    for h in range(SSM_HALVES):
        for tb in range(lc // 8):
            ws = [yf_ref[h * GROUPS_PER_HALF + gp, :, tb * LANES:(tb + 1) * LANES] for gp in range(GROUPS_PER_HALF)]
            vs = _unit_transpose8(ws)
            for tp in range(8):
                ys_ref[h, pl.ds(tb * 8 + tp, nk, stride=lc), :] = vs[tp]


def _pre_kernel(x_ref, sh_ref, sc_ref, n1_ref, w_ref, qn_ref, kn_ref, seg_ref, cos_ref, sin_ref,
                q_ref, k_ref, v_ref, u_ref, xr_ref, yg_ref, *maybe_uf_ref, per_token):
    tm = x_ref.shape[0]
    parts = PRE_ROW_SPLIT if tm % (PRE_ROW_SPLIT * SSM_CHUNK) == 0 else 1
    tp = tm // parts
    c1 = D_ATTN
    c2 = c1 + D_KV
    c3 = c2 + D_KV
    c4 = c3 + D_SSM
    c5 = c4 + D_LRU
    sc = _mod_rows(sc_ref, per_token, tm)
    sh = _mod_rows(sh_ref, per_token, tm)
    seg = seg_ref[...]

    def project(p):
        rows = slice(p * tp, (p + 1) * tp)
        scp, shp = (sc[rows], sh[rows]) if per_token else (sc, sh)
        hb = (_rms(x_ref[rows, :], n1_ref[...]) * (1.0 + scp) + shp).astype(BF16)
        return _dot(hb, w_ref[...])

    def finish(p, proj):
        rows = slice(p * tp, (p + 1) * tp)
        cos = cos_ref[rows, :]
        sin = sin_ref[rows, :]
        qn = _head_rms(proj[:, :c1], seg, qn_ref[...])
        kn = _head_rms(proj[:, c1:c2], seg[:D_KV, :D_KV], kn_ref[...])
        v_ref[rows, :] = proj[:, c2:c3]
        u_ref[rows, :] = proj[:, c3:c4]
        xr_ref[rows, :] = proj[:, c4:c5]
        yg_ref[rows, :] = proj[:, c5:]
        if maybe_uf_ref:
            u_scr = maybe_uf_ref[1]
            for hh in range(SSM_HALVES):
                u_scr[hh, rows, :] = proj[:, c3 + hh * LANES:c3 + (hh + 1) * LANES]
        q_ref[rows, :] = (_rope(qn, cos, sin) * (HEAD_DIM ** -0.5)).astype(BF16)
        k_ref[rows, :] = _rope(kn, cos, sin)

    nxt = project(0)
    for p in range(parts):
        cur = nxt
        if p + 1 < parts:
            nxt = project(p + 1)
        finish(p, cur)
    if maybe_uf_ref:
        uf_ref, u_scr = maybe_uf_ref
        groups = _to_group_major(u_scr, uf_ref.shape[1], SSM_CHUNK)
        for g in range(N_SSM_GROUPS):
            uf_ref[g] = groups[g].astype(BF16)


def _pre_call(x, mod, w, seg, cos, sin, *, layer, per_token, seq_len, nseq=None):
    nt = x.shape[0]
    tm = min(TOKEN_TILE, nt)
    tiles_per_seq = seq_len // tm
    if per_token:
        tab_spec = pl.BlockSpec((tm, LANES), lambda i: (i, 0))
    else:
        tab_spec = pl.BlockSpec((tm, LANES), lambda i: (i % tiles_per_seq, 0))
    mspec = lambda piece: _mod_spec(layer, piece, per_token, lambda i: i // tiles_per_seq)(nseq)
    row = lambda wd: pl.BlockSpec((tm, wd), lambda i: (i, 0))
    widths = (D_ATTN, D_KV, D_KV, D_SSM, D_LRU, D_LRU)
    dtypes = (BF16, F32, F32, F32, F32, F32)
    out_specs = [row(wd) for wd in widths]
    out_shape = [jax.ShapeDtypeStruct((nt, wd), d) for wd, d in zip(widths, dtypes)]
    scratch = []
    if not per_token:
        nk = tm // SSM_CHUNK
        wf = SSM_CHUNK * SSM_GROUP
        out_specs.append(pl.BlockSpec((N_SSM_GROUPS, nk, wf), lambda i: (0, i, 0)))
        out_shape.append(jax.ShapeDtypeStruct((N_SSM_GROUPS, nt // SSM_CHUNK, wf), BF16))
        scratch.append(pltpu.VMEM((SSM_HALVES, tm, LANES), F32))
    return pl.pallas_call(
        functools.partial(_pre_kernel, per_token=per_token),
        grid=(nt // tm,),
        in_specs=[
            row(D_MODEL), mspec(0), mspec(1),
            _layer_spec((1, D_MODEL), layer),
            _layer_spec((D_MODEL, D_IN), layer, True),
            _layer_spec((1, D_ATTN), layer),
            _layer_spec((1, D_KV), layer),
            pl.BlockSpec((D_ATTN, D_ATTN), lambda i: (0, 0)),
            tab_spec, tab_spec,
        ],
        out_specs=out_specs,
        out_shape=out_shape,
        scratch_shapes=scratch,
        compiler_params=_cparams("arbitrary"),
        name="pre",
    )(x, mod, mod, w["norm1"], w["w_in"], w["q_norm"], w["k_norm"], seg, cos, sin)


def _attn_prompt_kernel(sink_ref, q_ref, kc_ref, kp_ref, vc_ref, vp_ref, o_ref, *, layer, nsub):
    assert (KV_GROUP, N_KV_HEADS, 2 * HEAD_DIM) == (4, 2, LANES)
    i = pl.program_id(1)
    bq = ATTN_BLOCK
    kcat = jnp.concatenate([kp_ref[...], kc_ref[...]], axis=0).astype(BF16)
    vcat = jnp.concatenate([vp_ref[...], vc_ref[...]], axis=0).astype(BF16)
    low = lax.broadcasted_iota(jnp.int32, kcat.shape, 1) < HEAD_DIM

    def half_placed(x):
        zero = jnp.zeros_like(x)
        h0_lo = jnp.where(low, x, zero)
        h1_hi = jnp.where(low, zero, x)
        return [[h0_lo, pltpu.roll(h0_lo, HEAD_DIM, axis=1)], [pltpu.roll(h1_hi, HEAD_DIM, axis=1), h1_hi]]

    kz = half_placed(kcat)
    vz = half_placed(vcat)
    qi = lax.broadcasted_iota(jnp.int32, (bq, 2 * bq), 0)
    si = lax.broadcasted_iota(jnp.int32, (bq, 2 * bq), 1)
    diff = qi + bq - si
    in_window = (diff >= 0) & (diff < WINDOW)
    first_pair = lax.broadcasted_iota(jnp.int32, (2 * bq, 1), 0) < bq
    heads = [(h, e) for h in range(N_KV_HEADS) for e in range(2)]

    def score_block(j):
        band = slice(j * bq, (j + 2) * bq)
        qrows = slice(j * bq, (j + 1) * bq)
        scores = []
        for h, e in heads:
            qh = jnp.concatenate([q_ref[qrows, (2 * h + p) * LANES:(2 * h + p + 1) * LANES] for p in range(2)],
                                 axis=0)
            s = lax.dot_general(qh, kz[h][e][band], (((1,), (1,)), ((), ())), preferred_element_type=F32)
            scores.append(s)
        return scores

    nxt = score_block(0)
    for j in range(nsub):
        scores = nxt
        if j + 1 < nsub:
            nxt = score_block(j + 1)
        if j == 0:
            allowed = in_window & ((si >= bq) | (i > 0))
        else:
            allowed = in_window
        allowed = jnp.concatenate([allowed, allowed], axis=0)
        band = slice(j * bq, (j + 2) * bq)
        qrows = slice(j * bq, (j + 1) * bq)
        probs = []
        for (h, e), s in zip(heads, scores):
            s = jnp.where(allowed, s, -jnp.inf)
            sink = jnp.where(first_pair, sink_ref[layer, KV_GROUP * h + e], sink_ref[layer, KV_GROUP * h + 2 + e])
            m = jnp.maximum(jnp.max(s, axis=-1, keepdims=True), sink)
            p = jnp.exp(s - m)
            denom = jnp.sum(p, axis=-1, keepdims=True) + jnp.exp(sink - m)
            probs.append((p.astype(BF16), denom))
        outs = [_dot(p, vz[h][e][band]) / denom for (h, e), (p, denom) in zip(heads, probs)]
        for h in range(N_KV_HEADS):
            o_pair = outs[2 * h] + outs[2 * h + 1]
            o_ref[qrows, (2 * h) * LANES:(2 * h + 1) * LANES] = o_pair[:bq]
            o_ref[qrows, (2 * h + 1) * LANES:(2 * h + 2) * LANES] = o_pair[bq:]


def _attn_prompt_call(sinks, q, k, v, *, layer, bsz, seq_len):
    nsub = min(ATTN_SUBBLOCKS, seq_len // ATTN_BLOCK)
    bq = nsub * ATTN_BLOCK
    nb = seq_len // bq
    nb_small = seq_len // ATTN_BLOCK
    cur = lambda w: pl.BlockSpec((bq, w), lambda b, i: (b * nb + i, 0))
    prev = lambda w: pl.BlockSpec((ATTN_BLOCK, w), lambda b, i: (b * nb_small + jnp.maximum(i * nsub - 1, 0), 0))
    return pl.pallas_call(
        functools.partial(_attn_prompt_kernel, layer=layer, nsub=nsub),
        grid=(bsz, nb),
        in_specs=[pl.BlockSpec(memory_space=pltpu.SMEM), cur(D_ATTN), cur(D_KV), prev(D_KV), cur(D_KV),
                  prev(D_KV)],
        out_specs=cur(D_ATTN),
        out_shape=jax.ShapeDtypeStruct((bsz * seq_len, D_ATTN), F32),
        compiler_params=_cparams("arbitrary", "arbitrary"),
        name="attn_prompt",
    )(sinks, q, k, k, v, v)


def _attn_sample_kernel(sink_ref, q_ref, kn_ref, vn_ref, ck_ref, cv_ref, o_ref, sk_ref, sv_ref, *, layer, dseq):
    kk = jnp.concatenate([ck_ref[...], kn_ref[...]], axis=1)
    vv = jnp.concatenate([cv_ref[...], vn_ref[...]], axis=1)
    wbuf = ck_ref.shape[1]
    sk_ref[...] = kk[:, dseq:, :]
    sv_ref[...] = vv[:, dseq:, :]
    nq = dseq * KV_GROUP
    nk = wbuf + dseq
    qrow = lax.broadcasted_iota(jnp.int32, (nq, nk), 0)
    j = lax.broadcasted_iota(jnp.int32, (nq, nk), 1)
    diff = (qrow >> 2) + wbuf - j
    allowed = ((diff >= 0) & (diff < WINDOW))[None]
    g_of_row = lax.broadcasted_iota(jnp.int32, (nq, 1), 0) & (KV_GROUP - 1)
    kkb = kk.astype(BF16)
    vvb = vv.astype(BF16)
    for h in range(N_KV_HEADS):
        kh = kkb[:, :, h * HEAD_DIM:(h + 1) * HEAD_DIM]
        vh = vvb[:, :, h * HEAD_DIM:(h + 1) * HEAD_DIM]
        qh = q_ref[:, h]
        s = jnp.einsum("sqd,skd->sqk", qh, kh, preferred_element_type=F32)
        s = jnp.where(allowed, s, -jnp.inf)
        sink = jnp.full((nq, 1), sink_ref[layer, h * KV_GROUP], F32)
        for g in range(1, KV_GROUP):
            sink = jnp.where(g_of_row == g, sink_ref[layer, h * KV_GROUP + g], sink)
        sink = sink[None]
        m = jnp.maximum(jnp.max(s, axis=-1, keepdims=True), sink)
        p = jnp.exp(s - m)
        denom = jnp.sum(p, axis=-1, keepdims=True) + jnp.exp(sink - m)
        o = jnp.einsum("sqk,skd->sqd", p.astype(BF16), vh, preferred_element_type=F32) / denom
        o_ref[:, h] = o


def _attn_sample_call(sinks, q, kn, vn, ck, cv, *, layer, dseq):
    n, wbuf = ck.shape[0], ck.shape[1]
    sb = min(SAMPLE_ATTN_SEQS, n)
    nq = dseq * KV_GROUP
    qspec = pl.BlockSpec((sb, N_KV_HEADS, nq, HEAD_DIM), lambda i: (i, 0, 0, 0))
    nspec = pl.BlockSpec((sb, dseq, D_KV), lambda i: (i, 0, 0))
    cspec = pl.BlockSpec((sb, wbuf, D_KV), lambda i: (i, 0, 0))
    return pl.pallas_call(
        functools.partial(_attn_sample_kernel, layer=layer, dseq=dseq),
        grid=(n // sb,),
        in_specs=[pl.BlockSpec(memory_space=pltpu.SMEM), qspec, nspec, nspec, cspec, cspec],
        out_specs=[qspec, cspec, cspec],
        out_shape=[jax.ShapeDtypeStruct((n, N_KV_HEADS, nq, HEAD_DIM), F32),
                   jax.ShapeDtypeStruct((n, wbuf, D_KV), F32),
                   jax.ShapeDtypeStruct((n, wbuf, D_KV), F32)],
        compiler_params=_cparams("arbitrary"),
        name="attn_sample",
    )(sinks, q, kn, vn, ck, cv)


def _bdot3(a, b):
    dn = (((2,), (2,)), ((0,), (0,)))
    ah, al = _split_bf16(a)
    bh, bl = _split_bf16(b)
    d = lambda x, y: lax.dot_general(x, y, dn, preferred_element_type=F32)
    return d(ah, bh) + d(ah, bl) + d(al, bh)


def _ssm_prep_kernel(are_ref, aim_ref, ldt_ref, bre_ref, bim_ref, cre_ref, cim_ref,
                     r_ref, np_ref, mtp_ref, ns_ref, mts_ref, apw_ref, *, lc, dseq, powers):
    c = SSM_GROUP
    a_re = are_ref[...]
    a_im = aim_ref[...]
    dt = jnp.exp(ldt_ref[...])
    zr = a_re * dt
    zi = a_im * dt

    mag = jnp.exp(zr)
    abr, abi = mag * jnp.cos(zi), mag * jnp.sin(zi)
    pows = {0: (jnp.ones_like(abr), jnp.zeros_like(abi)), 1: (abr, abi)}

    def a_pow(j):
        if j not in pows:
            (pr, pi), (qr, qi) = (a_pow(j - 1), pows[1]) if j <= lc else (a_pow(j // 2),) * 2
            assert j <= lc or j % 2 == 0
            pows[j] = (pr * qr - pi * qi, pr * qi + pi * qr)
        return pows[j]

    xr = abr - 1.0
    den = a_re * a_re + a_im * a_im
    coef_r = (xr * a_re + abi * a_im) / den
    coef_i = (abi * a_re - xr * a_im) / den
    btr = bre_ref[...]
    bti = bim_ref[...]
    bbr = coef_r * btr - coef_i * bti
    bbi = coef_r * bti + coef_i * btr
    cre = cre_ref[...]
    cim = cim_ref[...]
    ns_ref[...] = jnp.zeros_like(ns_ref)
    mts_ref[...] = jnp.zeros_like(mts_ref)
    cars, cais = [], []
    for j in range(lc + 1):
        er, ei = a_pow(j)
        car = cre * er - cim * ei
        cai = cre * ei + cim * er
        if j < lc:
            cars.append(car)
            cais.append(cai)
            nbr = er * bbr - ei * bbi
            nbi = er * bbi + ei * bbr
            ncat = jnp.concatenate([nbr, nbi, nbi, nbr], axis=-1).astype(BF16)
            s = lc - 1 - j
            np_ref[:, s * c:(s + 1) * c, :] = ncat
            if j < dseq:
                s = dseq - 1 - j
                ns_ref[:, s * c:(s + 1) * c, :] = ncat
        if j >= 1:
            mcat = jnp.concatenate([car, -cai], axis=-1).astype(BF16)
            mtp_ref[:, (j - 1) * c:j * c, :] = mcat
            if j <= dseq:
                mts_ref[:, (j - 1) * c:j * c, :] = mcat
    ca_r = jnp.concatenate(cars, axis=1)
    ca_i = jnp.concatenate(cais, axis=1)
    r_ref[...] = _bdot3(bbr, ca_r) - _bdot3(bbi, ca_i)
    for idx, pw in enumerate(powers):
        er, ei = a_pow(pw)
        apw_ref[idx, 0] = jnp.concatenate([er, er], axis=-1)
        apw_ref[idx, 1] = jnp.concatenate([-ei, ei], axis=-1)
        apw_ref[idx, 2] = jnp.concatenate([ei, -ei], axis=-1)


def _ssm_prep_call(a_re, a_im, log_dt, b_re, b_im, c_re, c_im, *, lc, dseq, powers):
    depth = a_re.shape[0]
    g, p, c = N_SSM_GROUPS, SSM_STATE, SSM_GROUP
    npw = len(powers)
    a_spec = pl.BlockSpec((None, g, 1, p), lambda l: (l, 0, 0, 0))
    m_spec = pl.BlockSpec((None, g, c, p), lambda l: (l, 0, 0, 0))
    out4 = lambda a, b: pl.BlockSpec((None, g, a, b), lambda l: (l, 0, 0, 0))
    shape4 = lambda a, b, d: jax.ShapeDtypeStruct((depth, g, a, b), d)
    ws = LANES
    return pl.pallas_call(
        functools.partial(_ssm_prep_kernel, lc=lc, dseq=dseq, powers=powers),
        grid=(depth,),
        in_specs=[a_spec, a_spec, a_spec, m_spec, m_spec, m_spec, m_spec],
        out_specs=[out4(c, lc * c), out4(lc * c, 4 * p), out4(lc * c, 2 * p), out4(ws, 4 * p), out4(ws, 2 * p),
                   pl.BlockSpec((None, npw, 3, g, 1, 2 * p), lambda l: (l, 0, 0, 0, 0, 0))],
        out_shape=[shape4(c, lc * c, F32), shape4(lc * c, 4 * p, BF16), shape4(lc * c, 2 * p, BF16),
                   shape4(ws, 4 * p, BF16), shape4(ws, 2 * p, BF16),
                   jax.ShapeDtypeStruct((depth, npw, 3, g, 1, 2 * p), F32)],
        compiler_params=_cparams("arbitrary"),
        name="ssm_prep",
    )(a_re.reshape(depth, g, 1, p), a_im.reshape(depth, g, 1, p),
      jnp.broadcast_to(log_dt[:, :, None, None], (depth, g, 1, p)),
      jnp.swapaxes(b_re, -1, -2), jnp.swapaxes(b_im, -1, -2), c_re, c_im)


def _ssm_kernel(*refs, nsteps, first_power, nchunk, nseq, has_h0):
    if has_h0:
        u_ref, r_ref, n_ref, mt_ref, apw_ref, h0_ref, h0s_ref, y_ref, hend_ref, toe_scr = refs
    else:
        u_ref, r_ref, n_ref, mt_ref, apw_ref, y_ref, hend_ref, toe_scr = refs
    w = u_ref.shape[1]
    c = SSM_GROUP
    r = r_ref[...]
    lane = lax.broadcasted_iota(jnp.int32, r.shape, 1)
    toe_scr[0:c, :] = r.astype(BF16)
    for s in range(1, w // c):
        toe_scr[s * c:(s + 1) * c, :] = jnp.where(lane >= s * c, pltpu.roll(r, s * c, axis=1), 0.0).astype(BF16)
    uf = u_ref[...]
    y = _dot(uf, toe_scr[...])
    st = _dot(uf, n_ref[...])
    ws = 2 * SSM_STATE
    h = st[:, :ws]
    hs = st[:, ws:]
    kidx = lax.broadcasted_iota(jnp.int32, h.shape, 0) & (nchunk - 1)
    shift = lambda x, d: jnp.where(kidx >= d, pltpu.roll(x, d, axis=0), 0.0)
    for i in range(nsteps):
        d = 1 << i
        pw = first_power + i
        a1, a2, a3 = apw_ref[pw, 0], apw_ref[pw, 1], apw_ref[pw, 2]
        hd = shift(h, d)
        hsd = shift(hs, d)
        h, hs = h + a1 * hd + a2 * hsd, hs + a1 * hsd + a3 * hd
    if has_h0:
        h0 = h0_ref[...]
        hin = h0
        h = h + apw_ref[first_power, 0] * h0 + apw_ref[first_power, 1] * h0s_ref[...]
    else:
        hin = shift(h, 1)
    y_ref[...] = y + lax.dot_general(hin.astype(BF16), mt_ref[...], (((1,), (1,)), ((), ())),
                                     preferred_element_type=F32)
    if nchunk == 1:
        hend_ref[...] = h
    else:
        hend_ref[...] = jnp.concatenate([h[(b + 1) * nchunk - 1:(b + 1) * nchunk, :] for b in range(nseq)], axis=0)


def _ssm_call(uf, r, nmat, mt, apw, h0=None, h0s=None, *, layer, nchunk, nsteps, first_power):
    g, rows, w = uf.shape
    assert nchunk & (nchunk - 1) == 0, "chunks per sequence must be a power of two"
    nseq = rows // nchunk
    npw = apw.shape[1]
    gspec = lambda a, b: pl.BlockSpec((None, a, b), lambda i: (i, 0, 0))
    lgspec = lambda a, b: pl.BlockSpec((None, None, a, b), lambda i: (layer, i, 0, 0))
    in_specs = [gspec(rows, w), lgspec(SSM_GROUP, w), lgspec(w, 4 * SSM_STATE), lgspec(w, 2 * SSM_STATE),
                pl.BlockSpec((None, npw, 3, None, 1, 2 * SSM_STATE), lambda i: (layer, 0, 0, i, 0, 0))]
    args = [uf, r, nmat, mt, apw]
    if h0 is not None:
        in_specs += [gspec(rows, 2 * SSM_STATE)] * 2
        args += [h0, h0s]
    return pl.pallas_call(
        functools.partial(_ssm_kernel, nsteps=nsteps, first_power=first_power, nchunk=nchunk, nseq=nseq,
                          has_h0=h0 is not None),
        grid=(g,),
        in_specs=in_specs,
        out_specs=[gspec(rows, w), gspec(nseq, 2 * SSM_STATE)],
        out_shape=[jax.ShapeDtypeStruct((g, rows, w), F32),
                   jax.ShapeDtypeStruct((g, nseq, 2 * SSM_STATE), F32)],
        scratch_shapes=[pltpu.VMEM((w, w), BF16)],
        compiler_params=_cparams("arbitrary"),
        name="ssm",
    )(*args)


def _softplus(z):
    return jnp.maximum(z, 0.0) + jnp.log1p(jnp.exp(-jnp.abs(z)))


def _lru_gates(xc, wg_ref, bg_ref, lam_ref):
    gl = _dot(xc.astype(BF16), wg_ref[...]) + bg_ref[...]
    r = jax.nn.sigmoid(gl[:, :D_LRU])
    gi = jax.nn.sigmoid(gl[:, D_LRU:])
    log_a = -LRU_C * r * _softplus(-lam_ref[...])
    a = jnp.exp(log_a)
    mult = jnp.sqrt(1.0 - a * a)
    return a, mult, gi


def _lru_prompt_kernel(xr_ref, yg_ref, cw_ref, cb_ref, wg_ref, bg_ref, lam_ref, o_ref, hl_ref,
                       xp_scr, hc_scr):
    t = pl.program_id(1)
    tl = xr_ref.shape[0]
    halo = SUBLANES

    @pl.when(t == 0)
    def _():
        xp_scr[0:halo, :] = jnp.zeros((halo, D_LRU), F32)
        hc_scr[...] = jnp.zeros((1, D_LRU), F32)

    xp_scr[halo:halo + tl, :] = xr_ref[...]
    xc = cb_ref[...]
    for j in range(LRU_CONV):
        off = halo - (LRU_CONV - 1) + j
        xc = xc + cw_ref[j:j + 1, :] * xp_scr[off:off + tl, :]
    a, mult, gi = _lru_gates(xc, wg_ref, bg_ref, lam_ref)
    row = lax.broadcasted_iota(jnp.int32, (tl, D_LRU), 0)
    mult = jnp.where((row == 0) & (t == 0), 1.0, mult)
    b = mult * gi * xc
    nblk = tl // SUBLANES
    a = a.reshape(nblk, SUBLANES, D_LRU)
    b = b.reshape(nblk, SUBLANES, D_LRU)
    sub = lax.broadcasted_iota(jnp.int32, a.shape, 1)
    d = 1
    while d < SUBLANES:
        keep = sub >= d
        a_sh = jnp.where(keep, pltpu.roll(a, d, axis=1), 1.0)
        b_sh = jnp.where(keep, pltpu.roll(b, d, axis=1), 0.0)
        b = a * b_sh + b
        a = a * a_sh
        d *= 2
    carry = hc_scr[...]
    blocks = []
    for blk in range(nblk):
        hb = a[blk] * carry + b[blk]
        blocks.append(hb)
        carry = hb[SUBLANES - 1:SUBLANES, :]
    h = jnp.concatenate(blocks, axis=0)
    o_ref[...] = h * jax.nn.gelu(yg_ref[...])
    last = carry
    hc_scr[...] = last
    hl_ref[...] = jnp.broadcast_to(last, (SUBLANES, D_LRU))
    xp_scr[0:halo, :] = xp_scr[tl:tl + halo, :]


def _lru_weight_specs(layer):
    return [_layer_spec((LRU_CONV, D_LRU), layer), _layer_spec((1, D_LRU), layer),
            _layer_spec((D_LRU, 2 * D_LRU), layer), _layer_spec((1, 2 * D_LRU), layer),
            _layer_spec((1, D_LRU), layer)]


def _lru_weight_args(w):
    return (w["lru_conv_w"], w["lru_conv_b"], w["lru_wg"], w["lru_bg"], w["lru_lambda"])


def _lru_prompt_call(xr, yg, w, *, layer, bsz, seq_len):
    tl = min(LRU_TILE, seq_len)
    nt = seq_len // tl
    row = pl.BlockSpec((tl, D_LRU), lambda b, t: (b * nt + t, 0))
    return pl.pallas_call(
        _lru_prompt_kernel,
        grid=(bsz, nt),
        in_specs=[row, row] + _lru_weight_specs(layer),
        out_specs=[row, pl.BlockSpec((None, SUBLANES, D_LRU), lambda b, t: (b, 0, 0))],
        out_shape=[jax.ShapeDtypeStruct((bsz * seq_len, D_LRU), F32),
                   jax.ShapeDtypeStruct((bsz, SUBLANES, D_LRU), F32)],
        scratch_shapes=[pltpu.VMEM((tl + 2 * SUBLANES, D_LRU), F32), pltpu.VMEM((1, D_LRU), F32)],
        compiler_params=_cparams("arbitrary", "arbitrary"),
        name="lru_prompt",
    )(xr, yg, *_lru_weight_args(w))


def _lru_sample_kernel(xr_ref, yg_ref, buf_ref, h0_ref, cw_ref, cb_ref, wg_ref, bg_ref, lam_ref,
                       o_ref, hl_ref, *, dseq):
    n = h0_ref.shape[0]
    xp = [buf_ref[j] for j in range(LRU_CONV - 1)] + [xr_ref[pl.ds(t * n, n), :] for t in range(dseq)]
    xcs = []
    for t in range(dseq):
        xc = cb_ref[...]
        for j in range(LRU_CONV):
            xc = xc + cw_ref[j:j + 1, :] * xp[t + j]
        xcs.append(xc)
    xc = jnp.concatenate(xcs, axis=0)
    a, mult, gi = _lru_gates(xc, wg_ref, bg_ref, lam_ref)
    b = mult * gi * xc
    h = h0_ref[...]
    for t in range(dseq):
        h = a[t * n:(t + 1) * n] * h + b[t * n:(t + 1) * n]
        o_ref[pl.ds(t * n, n), :] = h * jax.nn.gelu(yg_ref[pl.ds(t * n, n), :])
    hl_ref[...] = h


def _lru_sample_call(xr, yg, buf_tm, h0, w, *, layer, dseq):
    n = h0.shape[0]
    full = lambda a: pl.BlockSpec(a.shape, lambda i: (0,) * a.ndim)
    return pl.pallas_call(
        functools.partial(_lru_sample_kernel, dseq=dseq),
        grid=(1,),
        in_specs=[full(xr), full(yg), full(buf_tm), full(h0)] + _lru_weight_specs(layer),
        out_specs=[pl.BlockSpec((dseq * n, D_LRU), lambda i: (0, 0)), pl.BlockSpec((n, D_LRU), lambda i: (0, 0))],
        out_shape=[jax.ShapeDtypeStruct((dseq * n, D_LRU), F32), jax.ShapeDtypeStruct((n, D_LRU), F32)],
        compiler_params=_cparams("arbitrary"),
        name="lru_sample",
    )(xr, yg, buf_tm, h0, *_lru_weight_args(w))


def _post_kernel(x_ref, oa_ref, ys_ref, u_ref, ol_ref, g1_ref, sh2_ref, sc2_ref, g2_ref,
                 d_ref, wglu_ref, bglu_ref, on_ref, wo_ref, n2_ref, wup_ref, cw_ref, cb_ref, wdn_ref,
                 halo_in_ref, xo_ref, halo_out_ref, h2_scr, acc_scr, halo_scr, ys_scr,
                 *, per_token, row_shift, group_major):
    t = pl.program_id(1)
    tm = x_ref.shape[0]
    hrows = halo_in_ref.shape[0]

    @pl.when(t == 0)
    def _():
        halo_scr[...] = halo_in_ref[...]

    if group_major:
        _from_group_major(ys_ref, ys_scr, ys_ref.shape[1], SSM_CHUNK)
        ys = jnp.concatenate([ys_scr[h] for h in range(SSM_HALVES)], axis=1)
    else:
        ys = ys_ref[...]
    ys = ys + d_ref[...] * u_ref[...]
    gs = jax.nn.gelu(ys)
    o_ssm = gs * jax.nn.sigmoid(_dot(gs.astype(BF16), wglu_ref[...]) + bglu_ref[...])
    on = on_ref[...]
    c1 = D_ATTN
    c2 = c1 + D_SSM
    o = jnp.concatenate([_rms(oa_ref[...], on[:, :c1]), _rms(o_ssm, on[:, c1:c2]),
                         _rms(ol_ref[...], on[:, c2:])], axis=-1)
    x1 = x_ref[...] + _mod_rows(g1_ref, per_token, tm) * _dot(o.astype(BF16), wo_ref[...])
    h2 = _rms(x1, n2_ref[...]) * (1.0 + _mod_rows(sc2_ref, per_token, tm)) + _mod_rows(sh2_ref, per_token, tm)
    h2_scr[...] = h2.astype(BF16)
    rowh = lax.broadcasted_iota(jnp.int32, (hrows, FF_CHUNK), 0)

    def ff_cols(c, gv):
        return slice(gv * D_FF + c * FF_CHUNK, gv * D_FF + (c + 1) * FF_CHUNK)

    def up_proj(c):
        return [_dot(h2_scr[...], wup_ref[:, ff_cols(c, gv)]) for gv in range(2)]

    ups = up_proj(0)
    for c in range(N_FF_CHUNKS):
        cur = ups
        if c + 1 < N_FF_CHUNKS:
            ups = up_proj(c + 1)
        halves = []
        for gv in range(2):
            cols = ff_cols(c, gv)
            up = cur[gv]
            halo = halo_scr[:, cols]
            y = cb_ref[:, cols] + cw_ref[FFN_CONV - 1:FFN_CONV, cols] * up
            for back in range(1, FFN_CONV):
                sh = back * row_shift
                r = pltpu.roll(up, sh, axis=0)
                hr = pltpu.roll(halo, sh, axis=0) if sh % hrows else halo
                head = jnp.where(rowh < sh, hr, r[:hrows])
                shifted = jnp.concatenate([head, r[hrows:]], axis=0)
                j = FFN_CONV - 1 - back
                y = y + cw_ref[j:j + 1, cols] * shifted
            halo_scr[:, cols] = up[tm - hrows:, :]
            halves.append(y)
        act = (jax.nn.gelu(halves[0]) * halves[1]).astype(BF16)
        contrib = _dot(act, wdn_ref[c * FF_CHUNK:(c + 1) * FF_CHUNK, :])
        if c == 0:
            acc_scr[...] = contrib
        else:
            acc_scr[...] += contrib
    xo_ref[...] = x1 + _mod_rows(g2_ref, per_token, tm) * acc_scr[...]
    halo_out_ref[...] = halo_scr[...]


def _post_call(x, oa, ys, u, ol, mod, w, halo_in, *, layer, per_token, seq_len, row_shift, mod_seqs=None):
    nt = x.shape[0]
    tm = nt if per_token else min(POST_TILE, nt)
    tiles_per_seq = seq_len // tm
    nseq = nt // seq_len
    hrows = halo_in.shape[1]
    row = lambda wd: pl.BlockSpec((tm, wd), lambda s, t: (s * tiles_per_seq + t, 0))
    mspec = lambda piece: _mod_spec(layer, piece, per_token, lambda s, t: s)(mod_seqs)
    halo_spec = pl.BlockSpec((None, hrows, 2 * D_FF), lambda s, t: (s, 0, 0))
    group_major = ys.ndim == 3
    if group_major:
        ys_spec = pl.BlockSpec((N_SSM_GROUPS, tm // SSM_CHUNK, SSM_CHUNK * SSM_GROUP),
                               lambda s, t: (0, s * tiles_per_seq + t, 0))
    else:
        ys_spec = row(D_SSM)
    return pl.pallas_call(
        functools.partial(_post_kernel, per_token=per_token, row_shift=row_shift, group_major=group_major),
        grid=(nseq, tiles_per_seq),
        in_specs=[
            row(D_MODEL), row(D_ATTN), ys_spec, row(D_SSM), row(D_LRU),
            mspec(2), mspec(3), mspec(4), mspec(5),
            _layer_spec((1, D_SSM), layer), _layer_spec((D_SSM, D_SSM), layer), _layer_spec((1, D_SSM), layer),
            _layer_spec((1, D_MODEL), layer), _layer_spec((D_MODEL, D_MODEL), layer, True),
            _layer_spec((1, D_MODEL), layer),
            _layer_spec((D_MODEL, 2 * D_FF), layer, True),
            _layer_spec((FFN_CONV, 2 * D_FF), layer),
            _layer_spec((1, 2 * D_FF), layer),
            _layer_spec((D_FF, D_MODEL), layer, True),
            halo_spec,
        ],
        out_specs=[row(D_MODEL), halo_spec],
        out_shape=[jax.ShapeDtypeStruct((nt, D_MODEL), F32),
                   jax.ShapeDtypeStruct((nseq, hrows, 2 * D_FF), F32)],
        scratch_shapes=[pltpu.VMEM((tm, D_MODEL), BF16), pltpu.VMEM((tm, D_MODEL), F32),
                        pltpu.VMEM((hrows, 2 * D_FF), F32),
                        pltpu.VMEM((SSM_HALVES, tm, LANES), F32)],
        compiler_params=_cparams("arbitrary", "arbitrary"),
        name="post",
    )(x, oa, ys, u, ol, mod, mod, mod, mod, w["ssm_d"], w["ssm_w_glu"], w["ssm_b_glu"], w["out_norm"],
      w["w_o"], w["norm2"], w["w_up"], w["ffn_conv_w"], w["ffn_conv_b"], w["w_down"], halo_in)


def _block_diag(w):
    depth, nb, bs, _ = w.shape
    eye = jnp.eye(nb, dtype=w.dtype)
    return jnp.einsum("lhij,hk->lhikj", w, eye).reshape(depth, nb * bs, nb * bs)


def _prepare_weights(p):
    depth = p["w_in"].shape[0]
    row = lambda a: a.reshape(depth, 1, -1)
    return dict(
        norm1=row(p["norm1"]), norm2=row(p["norm2"]), out_norm=row(p["out_norm"]),
        w_in=p["w_in"].astype(BF16), w_o=p["w_o"].astype(BF16),
        q_norm=row(jnp.tile(p["q_norm"], (1, N_HEADS))), k_norm=row(jnp.tile(p["k_norm"], (1, N_KV_HEADS))),
        ssm_d=row(p["ssm_d"]), ssm_w_glu=p["ssm_w_glu"].astype(BF16), ssm_b_glu=row(p["ssm_b_glu"]),
        lru_conv_w=p["lru_conv_w"], lru_conv_b=row(p["lru_conv_b"]),
        lru_wg=jnp.concatenate([_block_diag(p["lru_w_a"]), _block_diag(p["lru_w_i"])], axis=-1).astype(BF16),
        lru_bg=row(jnp.concatenate([p["lru_b_a"], p["lru_b_i"]], axis=-1)),
        lru_lambda=row(p["lru_lambda"]),
        w_up=p["ffn_w_up"].astype(BF16), ffn_conv_w=p["ffn_conv_w"], ffn_conv_b=row(p["ffn_conv_b"]),
        w_down=p["ffn_w_down"].astype(BF16),
    )


def _prompt_layer(x, mod, w, seg, sinks, ssm_tabs, cos, sin, *, layer, bsz, seq_len, nsteps):
    nchunk = seq_len // SSM_CHUNK
    q, k, v, u, xr, yg, uf = _pre_call(x, mod, w, seg, cos, sin, layer=layer, per_token=False, seq_len=seq_len)
    oa = _attn_prompt_call(sinks, q, k, v, layer=layer, bsz=bsz, seq_len=seq_len)
    r_all, np_all, mtp_all, _, _, apw = ssm_tabs
    yf, hend = _ssm_call(uf, r_all, np_all, mtp_all, apw, layer=layer, nchunk=nchunk, nsteps=nsteps,
                         first_power=1)
    ol, hl = _lru_prompt_call(xr, yg, w, layer=layer, bsz=bsz, seq_len=seq_len)
    halo0 = jnp.zeros((bsz, SUBLANES, 2 * D_FF), F32)
    x_new, halo = _post_call(x, oa, yf, u, ol, mod, w, halo0, layer=layer, per_token=False, seq_len=seq_len,
                             row_shift=1)
    keep = min(WINDOW, seq_len)
    last = lambda a, nrows: a.reshape(bsz, seq_len, a.shape[-1])[:, seq_len - nrows:]
    hend_b = jnp.transpose(hend, (1, 0, 2))
    states = (last(k, keep).reshape(bsz, keep, N_KV_HEADS, HEAD_DIM),
              last(v, keep).reshape(bsz, keep, N_KV_HEADS, HEAD_DIM),
              hend_b[..., :SSM_STATE], hend_b[..., SSM_STATE:],
              hl[:, 0, :],
              last(xr, LRU_CONV - 1),
              halo[:, SUBLANES - (FFN_CONV - 1):, :])
    return x_new, states


def _sample_layer(x, mod, w, seg, sinks, ssm_tabs, cos, sin, st, *, layer, n, dseq):
    g = N_SSM_GROUPS
    ck, cv, s_re, s_im, lru_h, lru_conv, ffn_conv = st
    nt = n * dseq
    q, k, v, u, xr, yg = _pre_call(x, mod, w, seg, cos, sin, layer=layer, per_token=True, seq_len=nt, nseq=n)
    q_sm = jnp.transpose(q.reshape(dseq, n, N_KV_HEADS, KV_GROUP, HEAD_DIM), (1, 2, 0, 3, 4))
    q_sm = q_sm.reshape(n, N_KV_HEADS, dseq * KV_GROUP, HEAD_DIM)
    kn = jnp.transpose(k.reshape(dseq, n, D_KV), (1, 0, 2))
    vn = jnp.transpose(v.reshape(dseq, n, D_KV), (1, 0, 2))
    wbuf = ck.shape[1]
    o_sm, sk, sv = _attn_sample_call(sinks, q_sm, kn, vn, ck.reshape(n, wbuf, D_KV), cv.reshape(n, wbuf, D_KV),
                                     layer=layer, dseq=dseq)
    oa = jnp.transpose(o_sm.reshape(n, N_KV_HEADS, dseq, KV_GROUP, HEAD_DIM), (2, 0, 1, 3, 4)).reshape(nt, D_ATTN)
    r_all, _, _, ns_all, mts_all, apw = ssm_tabs
    wpad = LANES - dseq * SSM_GROUP
    uf = jnp.transpose(u.reshape(dseq, n, g, SSM_GROUP), (2, 1, 0, 3)).reshape(g, n, dseq * SSM_GROUP)
    uf = jnp.pad(uf, ((0, 0), (0, 0), (0, wpad))).astype(BF16)
    h_re = jnp.transpose(s_re, (1, 0, 2))
    h_im = jnp.transpose(s_im, (1, 0, 2))
    h0 = jnp.concatenate([h_re, h_im], axis=-1)
    h0s = jnp.concatenate([h_im, h_re], axis=-1)
    yf, hend = _ssm_call(uf, r_all, ns_all, mts_all, apw, h0, h0s, layer=layer, nchunk=1, nsteps=0, first_power=0)
    yf = yf[:, :, :dseq * SSM_GROUP]
    ys = jnp.transpose(yf.reshape(g, n, dseq, SSM_GROUP), (2, 1, 0, 3)).reshape(nt, D_SSM)
    ol, hl = _lru_sample_call(xr, yg, jnp.transpose(lru_conv, (1, 0, 2)), lru_h, w, layer=layer, dseq=dseq)
    nconv = FFN_CONV - 1
    halo0 = jnp.transpose(ffn_conv, (1, 0, 2)).reshape(1, nconv * n, 2 * D_FF)
    x_new, halo = _post_call(x, oa, ys, u, ol, mod, w, halo0, layer=layer, per_token=True, seq_len=nt,
                             row_shift=n, mod_seqs=n)
    hend_b = jnp.transpose(hend, (1, 0, 2))
    xr_tm = xr.reshape(dseq, n, D_LRU)
    lru_conv_all = jnp.concatenate([jnp.transpose(lru_conv, (1, 0, 2)), xr_tm], axis=0)
    states = (sk.reshape(n, wbuf, N_KV_HEADS, HEAD_DIM), sv.reshape(n, wbuf, N_KV_HEADS, HEAD_DIM),
              hend_b[..., :SSM_STATE], hend_b[..., SSM_STATE:],
              hl,
              jnp.transpose(lru_conv_all[dseq:], (1, 0, 2)),
              jnp.transpose(halo.reshape(nconv, n, 2 * D_FF), (1, 0, 2)))
    return x_new, states


def kernel(x_prompt, x_sample, cache_k, cache_v, state_ssm_re, state_ssm_im, state_lru_h, state_lru_conv,
           state_ffn_conv, c_prompt, c_sample, w_ada, b_ada, norm1, w_in, q_norm, k_norm, sinks, ssm_a_re,
           ssm_a_im, ssm_b_re, ssm_b_im, ssm_c_re, ssm_c_im, ssm_d, ssm_log_dt, ssm_w_glu, ssm_b_glu,
           lru_conv_w, lru_conv_b, lru_w_a, lru_b_a, lru_w_i, lru_b_i, lru_lambda, out_norm, w_o, norm2,
           ffn_w_up, ffn_conv_w, ffn_conv_b, ffn_w_down):
    bsz, seq_len = x_prompt.shape[:2]
    n, dseq = x_sample.shape[:2]
    depth = w_in.shape[0]
    params = dict(norm1=norm1, w_in=w_in, q_norm=q_norm, k_norm=k_norm, ssm_d=ssm_d,
                  ssm_w_glu=ssm_w_glu, ssm_b_glu=ssm_b_glu, lru_conv_w=lru_conv_w, lru_conv_b=lru_conv_b,
                  lru_w_a=lru_w_a, lru_b_a=lru_b_a, lru_w_i=lru_w_i, lru_b_i=lru_b_i, lru_lambda=lru_lambda,
                  out_norm=out_norm, w_o=w_o, norm2=norm2, ffn_w_up=ffn_w_up, ffn_conv_w=ffn_conv_w,
                  ffn_conv_b=ffn_conv_b, ffn_w_down=ffn_w_down)
    weights = _prepare_weights(params)
    seg = (jnp.arange(D_ATTN)[:, None] // HEAD_DIM == jnp.arange(D_ATTN)[None, :] // HEAD_DIM).astype(BF16)

    rows = n + bsz
    pad = (-rows) % SUBLANES
    c_all = jnp.concatenate([c_sample, c_prompt, jnp.zeros((pad, D_MODEL), F32)], axis=0)
    mod_all = _ada_call(c_all, w_ada, b_ada)
    mod_p = jnp.broadcast_to(mod_all[:, :, n:n + bsz, None, :], (depth, N_MOD, bsz, SUBLANES, D_MODEL))

    cos_p, sin_p = _rope_call(jnp.arange(seq_len, dtype=F32))
    cos_s, sin_s = _rope_call(jnp.repeat(PAST_LEN + jnp.arange(dseq, dtype=F32), n))

    nchunk = seq_len // SSM_CHUNK
    nsteps = max(nchunk - 1, 0).bit_length()
    powers = (dseq,) + tuple(SSM_CHUNK * (1 << i) for i in range(nsteps))
    assert dseq * SSM_GROUP <= LANES
    ssm_tabs = _ssm_prep_call(ssm_a_re, ssm_a_im, ssm_log_dt, ssm_b_re, ssm_b_im, ssm_c_re, ssm_c_im,
                              lc=SSM_CHUNK, dseq=dseq, powers=powers)

    xp = x_prompt.reshape(bsz * seq_len, D_MODEL)
    xs = jnp.transpose(x_sample, (1, 0, 2)).reshape(dseq * n, D_MODEL)
    new_p, new_s = [], []
    for i in range(depth):
        xp, st_p = _prompt_layer(xp, mod_p, weights, seg, sinks, ssm_tabs, cos_p, sin_p, layer=i, bsz=bsz,
                                 seq_len=seq_len, nsteps=nsteps)
        st_in = (cache_k[i], cache_v[i], state_ssm_re[i], state_ssm_im[i], state_lru_h[i], state_lru_conv[i],
                 state_ffn_conv[i])
        xs, st_s = _sample_layer(xs, mod_all, weights, seg, sinks, ssm_tabs, cos_s, sin_s, st_in, layer=i, n=n,
                                 dseq=dseq)
        new_p.append(st_p)
        new_s.append(st_s)
    pk, pv, p_re, p_im, p_lh, p_lc, p_fc = [jnp.stack(s) for s in zip(*new_p)]
    sk, sv, s_re, s_im, s_lh, s_lc, s_fc = [jnp.stack(s) for s in zip(*new_s)]
    y_p = xp.reshape(bsz, seq_len, D_MODEL)
    y_s = jnp.transpose(xs.reshape(dseq, n, D_MODEL), (1, 0, 2))
    return (y_p, y_s, pk, pv, p_re, p_im, p_lh, p_lc, p_fc, sk, sv, s_re, s_im, s_lh, s_lc, s_fc)
```

```python
import functools

import jax
import jax.numpy as jnp
from jax import lax
from jax.experimental import pallas as pl
from jax.experimental.pallas import tpu as pltpu

F32 = jnp.float32
BF16 = jnp.bfloat16

D_MODEL = 1024
HEAD_DIM = 64
N_HEADS = 8
N_KV_HEADS = 2
KV_GROUP = N_HEADS // N_KV_HEADS
D_ATTN = N_HEADS * HEAD_DIM
D_KV = N_KV_HEADS * HEAD_DIM
WINDOW = 128
ROPE_THETA = 10000.0
PAST_LEN = 8192
D_SSM = 256
SSM_GROUP = 16
N_SSM_GROUPS = 16
SSM_STATE = 64
D_LRU = 256
N_LRU_BLOCKS = 4
LRU_BLOCK = 64
LRU_CONV = 4
LRU_C = 8.0
D_FF = 2816
FFN_CONV = 3
D_IN = D_ATTN + 2 * D_KV + D_SSM + 2 * D_LRU
N_MOD = 6
EPS = 1e-6

SUBLANES = 8
LANES = 128
VMEM_LIMIT_BYTES = 56 * 1024 * 1024

TOKEN_TILE = 512
ATTN_BLOCK = WINDOW
ATTN_SUBBLOCKS = 4
SSM_CHUNK = 32
LRU_TILE = 1024
FF_CHUNK = 256
N_FF_CHUNKS = D_FF // FF_CHUNK
PRE_ROW_SPLIT = 2
POST_TILE = 512
POST_PART_ROWS = 256
SAMPLE_ATTN_SEQS = 16
ADA_PIECES = 2
SSM_HALVES = D_SSM // LANES
GROUPS_PER_HALF = LANES // SSM_GROUP


def _cparams(*sem):
    return pltpu.CompilerParams(dimension_semantics=sem, vmem_limit_bytes=VMEM_LIMIT_BYTES)


def _dot(a, b):
    return jnp.dot(a, b, preferred_element_type=F32)


def _split_bf16(x):
    hi = x.astype(BF16)
    lo = (x - hi.astype(F32)).astype(BF16)
    return hi, lo


def _layer_spec(shape, layer, single=False):
    nd = len(shape)
    idx = lambda *_: (layer,) + (0,) * nd
    if single:
        return pl.BlockSpec((None,) + tuple(shape), idx, pipeline_mode=pl.Buffered(1))
    return pl.BlockSpec((None,) + tuple(shape), idx)


def _ada_kernel(c_ref, w_ref, b_ref, o_ref):
    c = c_ref[...]
    s = jax.nn.silu(c).astype(BF16)
    res = _dot(s, w_ref[...].astype(BF16)) + b_ref[...]
    for j in range(ADA_PIECES):
        o_ref[j] = res[:, j * D_MODEL:(j + 1) * D_MODEL]


def _ada_call(c_all, w_ada, b_ada):
    depth = w_ada.shape[0]
    rows = c_all.shape[0]
    cols = ADA_PIECES * D_MODEL
    return pl.pallas_call(
        _ada_kernel,
        grid=(depth, N_MOD // ADA_PIECES),
        in_specs=[
            pl.BlockSpec((rows, D_MODEL), lambda l, j: (0, 0)),
            pl.BlockSpec((None, D_MODEL, cols), lambda l, j: (l, 0, j)),
            pl.BlockSpec((None, 1, cols), lambda l, j: (l, 0, j)),
        ],
        out_specs=pl.BlockSpec((None, ADA_PIECES, rows, D_MODEL), lambda l, j: (l, j, 0, 0)),
        out_shape=jax.ShapeDtypeStruct((depth, N_MOD, rows, D_MODEL), F32),
        compiler_params=_cparams("arbitrary", "arbitrary"),
        name="ada",
    )(c_all, w_ada, b_ada.reshape(depth, 1, N_MOD * D_MODEL))


def _rope_kernel(pos_ref, cos_ref, sin_ref):
    pos = pos_ref[...]
    lane = lax.broadcasted_iota(jnp.int32, pos.shape, 1)
    half = HEAD_DIM // 2
    j = (lane & (half - 1)).astype(F32)
    inv = ROPE_THETA ** (-j / half)
    ang = pos * inv
    cos_ref[...] = jnp.cos(ang)
    s = jnp.sin(ang)
    sin_ref[...] = jnp.where((lane & (HEAD_DIM - 1)) < half, -s, s)


def _rope_call(pos_rows):
    t = pos_rows.shape[0]
    pos_b = jnp.broadcast_to(pos_rows[:, None], (t, LANES))
    return pl.pallas_call(
        _rope_kernel,
        out_shape=(jax.ShapeDtypeStruct((t, LANES), F32), jax.ShapeDtypeStruct((t, LANES), F32)),
        name="rope",
    )(pos_b)


def _mod_rows(ref, per_token, tm):
    if not per_token:
        return ref[0:1, :]
    m = ref[...]
    reps = tm // m.shape[0]
    return jnp.concatenate([m] * reps, axis=0) if reps > 1 else m


def _mod_spec(layer, piece, per_token, seq_of_step):
    if per_token:
        return lambda n: pl.BlockSpec((None, None, n, D_MODEL), lambda *g: (layer, piece, 0, 0))
    return lambda n: pl.BlockSpec((None, None, None, SUBLANES, D_MODEL),
                                  lambda *g: (layer, piece, seq_of_step(*g), 0, 0))


def _rms(x, gain):
    return x * lax.rsqrt(jnp.mean(x * x, axis=-1, keepdims=True) + EPS) * gain


def _head_rms(t, seg, gain):
    ss = _dot((t * t).astype(BF16), seg)
    return t * lax.rsqrt(ss * (1.0 / HEAD_DIM) + EPS) * gain


def _rope(t, cos, sin):
    width = t.shape[1]
    reps = width // LANES
    if reps > 1:
        cos = jnp.concatenate([cos] * reps, axis=1)
        sin = jnp.concatenate([sin] * reps, axis=1)
    lane = lax.broadcasted_iota(jnp.int32, t.shape, 1)
    half = HEAD_DIM // 2
    up = pltpu.roll(t, width - half, axis=1)
    dn = pltpu.roll(t, half, axis=1)
    rot = jnp.where((lane & (HEAD_DIM - 1)) < half, up, dn)
    return t * cos + rot * sin


def _unit_transpose8(vs):
    lane = lax.broadcasted_iota(jnp.int32, vs[0].shape, 1)
    unit = lane >> 4
    for b in range(3):
        d = 1 << b
        bit = (unit >> b) & 1
        new = list(vs)
        for i in range(8):
            if (i >> b) & 1 == 0:
                lo, hi = vs[i], vs[i + d]
                new[i] = jnp.where(bit == 0, lo, pltpu.roll(hi, d * SSM_GROUP, axis=1))
                new[i + d] = jnp.where(bit == 1, hi, pltpu.roll(lo, LANES - d * SSM_GROUP, axis=1))
        vs = new
    return vs


def _to_group_major(u_ref, nk, lc):
    outs = [[None] * (lc // 8) for _ in range(N_SSM_GROUPS)]
    for h in range(SSM_HALVES):
        for tb in range(lc // 8):
            vs = [u_ref[h, pl.ds(tb * 8 + tp, nk, stride=lc), :] for tp in range(8)]
            ws = _unit_transpose8(vs)
            for gp in range(GROUPS_PER_HALF):
                outs[h * GROUPS_PER_HALF + gp][tb] = ws[gp]
    return [jnp.concatenate(o, axis=1) for o in outs]


def _from_group_major(yf_ref, ys_ref, nk, lc):
    for h in range(SSM_HALVES):
        for tb in range(lc // 8):
            ws = [yf_ref[h * GROUPS_PER_HALF + gp, :, tb * LANES:(tb + 1) * LANES] for gp in range(GROUPS_PER_HALF)]
            vs = _unit_transpose8(ws)
            for tp in range(8):
                ys_ref[h, pl.ds(tb * 8 + tp, nk, stride=lc), :] = vs[tp]


def _pre_kernel(x_ref, sh_ref, sc_ref, n1_ref, w_ref, qn_ref, kn_ref, seg_ref, cos_ref, sin_ref,
                q_ref, k_ref, v_ref, u_ref, xr_ref, yg_ref, *maybe_uf_ref, per_token):
    tm = x_ref.shape[0]
    parts = PRE_ROW_SPLIT if tm % (PRE_ROW_SPLIT * SSM_CHUNK) == 0 else 1
    tp = tm // parts
    c1 = D_ATTN
    c2 = c1 + D_KV
    c3 = c2 + D_KV
    c4 = c3 + D_SSM
    c5 = c4 + D_LRU
    sc = _mod_rows(sc_ref, per_token, tm)
    sh = _mod_rows(sh_ref, per_token, tm)
    seg = seg_ref[...]

    def project(p):
        rows = slice(p * tp, (p + 1) * tp)
        scp, shp = (sc[rows], sh[rows]) if per_token else (sc, sh)
        hb = (_rms(x_ref[rows, :], n1_ref[...]) * (1.0 + scp) + shp).astype(BF16)
        return _dot(hb, w_ref[...])

    def finish(p, proj):
        rows = slice(p * tp, (p + 1) * tp)
        cos = cos_ref[rows, :]
        sin = sin_ref[rows, :]
        qn = _head_rms(proj[:, :c1], seg, qn_ref[...])
        kn = _head_rms(proj[:, c1:c2], seg[:D_KV, :D_KV], kn_ref[...])
        v_ref[rows, :] = proj[:, c2:c3]
        u_ref[rows, :] = proj[:, c3:c4]
        xr_ref[rows, :] = proj[:, c4:c5]
        yg_ref[rows, :] = proj[:, c5:]
        if maybe_uf_ref:
            u_scr = maybe_uf_ref[1]
            for hh in range(SSM_HALVES):
                u_scr[hh, rows, :] = proj[:, c3 + hh * LANES:c3 + (hh + 1) * LANES]
        q_ref[rows, :] = (_rope(qn, cos, sin) * (HEAD_DIM ** -0.5)).astype(BF16)
        k_ref[rows, :] = _rope(kn, cos, sin)

    nxt = project(0)
    for p in range(parts):
        cur = nxt
        if p + 1 < parts:
            nxt = project(p + 1)
        finish(p, cur)
    if maybe_uf_ref:
        uf_ref, u_scr = maybe_uf_ref
        groups = _to_group_major(u_scr, uf_ref.shape[1], SSM_CHUNK)
        for g in range(N_SSM_GROUPS):
            uf_ref[g] = groups[g].astype(BF16)


def _pre_call(x, mod, w, seg, cos, sin, *, layer, per_token, seq_len, nseq=None):
    nt = x.shape[0]
    tm = min(TOKEN_TILE, nt)
    tiles_per_seq = seq_len // tm
    if per_token:
        tab_spec = pl.BlockSpec((tm, LANES), lambda i: (i, 0))
    else:
        tab_spec = pl.BlockSpec((tm, LANES), lambda i: (i % tiles_per_seq, 0))
    mspec = lambda piece: _mod_spec(layer, piece, per_token, lambda i: i // tiles_per_seq)(nseq)
    row = lambda wd: pl.BlockSpec((tm, wd), lambda i: (i, 0))
    widths = (D_ATTN, D_KV, D_KV, D_SSM, D_LRU, D_LRU)
    dtypes = (BF16, F32, F32, F32, F32, F32)
    out_specs = [row(wd) for wd in widths]
    out_shape = [jax.ShapeDtypeStruct((nt, wd), d) for wd, d in zip(widths, dtypes)]
    scratch = []
    if not per_token:
        nk = tm // SSM_CHUNK
        wf = SSM_CHUNK * SSM_GROUP
        out_specs.append(pl.BlockSpec((N_SSM_GROUPS, nk, wf), lambda i: (0, i, 0)))
        out_shape.append(jax.ShapeDtypeStruct((N_SSM_GROUPS, nt // SSM_CHUNK, wf), BF16))
        scratch.append(pltpu.VMEM((SSM_HALVES, tm, LANES), F32))
    return pl.pallas_call(
        functools.partial(_pre_kernel, per_token=per_token),
        grid=(nt // tm,),
        in_specs=[
            row(D_MODEL), mspec(0), mspec(1),
            _layer_spec((1, D_MODEL), layer),
            _layer_spec((D_MODEL, D_IN), layer, True),
            _layer_spec((1, D_ATTN), layer),
            _layer_spec((1, D_KV), layer),
            pl.BlockSpec((D_ATTN, D_ATTN), lambda i: (0, 0)),
            tab_spec, tab_spec,
        ],
        out_specs=out_specs,
        out_shape=out_shape,
        scratch_shapes=scratch,
        compiler_params=_cparams("arbitrary"),
        name="pre",
    )(x, mod, mod, w["norm1"], w["w_in"], w["q_norm"], w["k_norm"], seg, cos, sin)


def _attn_prompt_kernel(sink_ref, q_ref, kc_ref, kp_ref, vc_ref, vp_ref, o_ref, *, layer, nsub):
    assert (KV_GROUP, N_KV_HEADS, 2 * HEAD_DIM) == (4, 2, LANES)
    i = pl.program_id(1)
    bq = ATTN_BLOCK
    kcat = jnp.concatenate([kp_ref[...], kc_ref[...]], axis=0).astype(BF16)
    vcat = jnp.concatenate([vp_ref[...], vc_ref[...]], axis=0).astype(BF16)
    low = lax.broadcasted_iota(jnp.int32, kcat.shape, 1) < HEAD_DIM

    def half_placed(x):
        zero = jnp.zeros_like(x)
        h0_lo = jnp.where(low, x, zero)
        h1_hi = jnp.where(low, zero, x)
        return [[h0_lo, pltpu.roll(h0_lo, HEAD_DIM, axis=1)], [pltpu.roll(h1_hi, HEAD_DIM, axis=1), h1_hi]]

    kz = half_placed(kcat)
    vz = half_placed(vcat)
    qi = lax.broadcasted_iota(jnp.int32, (bq, 2 * bq), 0)
    si = lax.broadcasted_iota(jnp.int32, (bq, 2 * bq), 1)
    diff = qi + bq - si
    in_window = (diff >= 0) & (diff < WINDOW)
    first_pair = lax.broadcasted_iota(jnp.int32, (2 * bq, 1), 0) < bq
    heads = [(h, e) for h in range(N_KV_HEADS) for e in range(2)]

    def score_block(j):
        band = slice(j * bq, (j + 2) * bq)
        qrows = slice(j * bq, (j + 1) * bq)
        scores = []
        for h, e in heads:
            qh = jnp.concatenate([q_ref[qrows, (2 * h + p) * LANES:(2 * h + p + 1) * LANES] for p in range(2)],
                                 axis=0)
            s = lax.dot_general(qh, kz[h][e][band], (((1,), (1,)), ((), ())), preferred_element_type=F32)
            scores.append(s)
        return scores

    nxt = score_block(0)
    for j in range(nsub):
        scores = nxt
        if j + 1 < nsub:
            nxt = score_block(j + 1)
        if j == 0:
            allowed = in_window & ((si >= bq) | (i > 0))
        else:
            allowed = in_window
        allowed = jnp.concatenate([allowed, allowed], axis=0)
        band = slice(j * bq, (j + 2) * bq)
        qrows = slice(j * bq, (j + 1) * bq)
        probs = []
        for (h, e), s in zip(heads, scores):
            s = jnp.where(allowed, s, -jnp.inf)
            sink = jnp.where(first_pair, sink_ref[layer, KV_GROUP * h + e], sink_ref[layer, KV_GROUP * h + 2 + e])
            m = jnp.maximum(jnp.max(s, axis=-1, keepdims=True), sink)
            p = jnp.exp(s - m)
            denom = jnp.sum(p, axis=-1, keepdims=True) + jnp.exp(sink - m)
            probs.append((p.astype(BF16), denom))
        outs = [_dot(p, vz[h][e][band]) / denom for (h, e), (p, denom) in zip(heads, probs)]
        for h in range(N_KV_HEADS):
            o_pair = outs[2 * h] + outs[2 * h + 1]
            o_ref[qrows, (2 * h) * LANES:(2 * h + 1) * LANES] = o_pair[:bq]
            o_ref[qrows, (2 * h + 1) * LANES:(2 * h + 2) * LANES] = o_pair[bq:]


def _attn_prompt_call(sinks, q, k, v, *, layer, bsz, seq_len):
    nsub = min(ATTN_SUBBLOCKS, seq_len // ATTN_BLOCK)
    bq = nsub * ATTN_BLOCK
    nb = seq_len // bq
    nb_small = seq_len // ATTN_BLOCK
    cur = lambda w: pl.BlockSpec((bq, w), lambda b, i: (b * nb + i, 0))
    prev = lambda w: pl.BlockSpec((ATTN_BLOCK, w), lambda b, i: (b * nb_small + jnp.maximum(i * nsub - 1, 0), 0))
    return pl.pallas_call(
        functools.partial(_attn_prompt_kernel, layer=layer, nsub=nsub),
        grid=(bsz, nb),
        in_specs=[pl.BlockSpec(memory_space=pltpu.SMEM), cur(D_ATTN), cur(D_KV), prev(D_KV), cur(D_KV),
                  prev(D_KV)],
        out_specs=cur(D_ATTN),
        out_shape=jax.ShapeDtypeStruct((bsz * seq_len, D_ATTN), F32),
        compiler_params=_cparams("arbitrary", "arbitrary"),
        name="attn_prompt",
    )(sinks, q, k, k, v, v)


def _attn_sample_kernel(sink_ref, q_ref, kn_ref, vn_ref, ck_ref, cv_ref, o_ref, sk_ref, sv_ref, *, layer, dseq):
    kk = jnp.concatenate([ck_ref[...], kn_ref[...]], axis=1)
    vv = jnp.concatenate([cv_ref[...], vn_ref[...]], axis=1)
    wbuf = ck_ref.shape[1]
    sk_ref[...] = kk[:, dseq:, :]
    sv_ref[...] = vv[:, dseq:, :]
    nq = dseq * KV_GROUP
    nk = wbuf + dseq
    qrow = lax.broadcasted_iota(jnp.int32, (nq, nk), 0)
    j = lax.broadcasted_iota(jnp.int32, (nq, nk), 1)
    diff = (qrow >> 2) + wbuf - j
    allowed = ((diff >= 0) & (diff < WINDOW))[None]
    g_of_row = lax.broadcasted_iota(jnp.int32, (nq, 1), 0) & (KV_GROUP - 1)
    kkb = kk.astype(BF16)
    vvb = vv.astype(BF16)
    for h in range(N_KV_HEADS):
        kh = kkb[:, :, h * HEAD_DIM:(h + 1) * HEAD_DIM]
        vh = vvb[:, :, h * HEAD_DIM:(h + 1) * HEAD_DIM]
        qh = q_ref[:, h]
        s = jnp.einsum("sqd,skd->sqk", qh, kh, preferred_element_type=F32)
        s = jnp.where(allowed, s, -jnp.inf)
        sink = jnp.full((nq, 1), sink_ref[layer, h * KV_GROUP], F32)
        for g in range(1, KV_GROUP):
            sink = jnp.where(g_of_row == g, sink_ref[layer, h * KV_GROUP + g], sink)
        sink = sink[None]
        m = jnp.maximum(jnp.max(s, axis=-1, keepdims=True), sink)
        p = jnp.exp(s - m)
        denom = jnp.sum(p, axis=-1, keepdims=True) + jnp.exp(sink - m)
        o = jnp.einsum("sqk,skd->sqd", p.astype(BF16), vh, preferred_element_type=F32) / denom
        o_ref[:, h] = o


def _attn_sample_call(sinks, q, kn, vn, ck, cv, *, layer, dseq):
    n, wbuf = ck.shape[0], ck.shape[1]
    sb = min(SAMPLE_ATTN_SEQS, n)
    nq = dseq * KV_GROUP
    qspec = pl.BlockSpec((sb, N_KV_HEADS, nq, HEAD_DIM), lambda i: (i, 0, 0, 0))
    nspec = pl.BlockSpec((sb, dseq, D_KV), lambda i: (i, 0, 0))
    cspec = pl.BlockSpec((sb, wbuf, D_KV), lambda i: (i, 0, 0))
    return pl.pallas_call(
        functools.partial(_attn_sample_kernel, layer=layer, dseq=dseq),
        grid=(n // sb,),
        in_specs=[pl.BlockSpec(memory_space=pltpu.SMEM), qspec, nspec, nspec, cspec, cspec],
        out_specs=[qspec, cspec, cspec],
        out_shape=[jax.ShapeDtypeStruct((n, N_KV_HEADS, nq, HEAD_DIM), F32),
                   jax.ShapeDtypeStruct((n, wbuf, D_KV), F32),
                   jax.ShapeDtypeStruct((n, wbuf, D_KV), F32)],
        compiler_params=_cparams("arbitrary"),
        name="attn_sample",
    )(sinks, q, kn, vn, ck, cv)


def _bdot3(a, b):
    dn = (((2,), (2,)), ((0,), (0,)))
    ah, al = _split_bf16(a)
    bh, bl = _split_bf16(b)
    d = lambda x, y: lax.dot_general(x, y, dn, preferred_element_type=F32)
    return d(ah, bh) + d(ah, bl) + d(al, bh)


def _ssm_prep_kernel(are_ref, aim_ref, ldt_ref, bre_ref, bim_ref, cre_ref, cim_ref,
                     r_ref, np_ref, mtp_ref, ns_ref, mts_ref, apw_ref, *, lc, dseq, powers):
    c = SSM_GROUP
    a_re = are_ref[...]
    a_im = aim_ref[...]
    dt = jnp.exp(ldt_ref[...])
    zr = a_re * dt
    zi = a_im * dt

    mag = jnp.exp(zr)
    abr, abi = mag * jnp.cos(zi), mag * jnp.sin(zi)
    pows = {0: (jnp.ones_like(abr), jnp.zeros_like(abi)), 1: (abr, abi)}

    def a_pow(j):
        if j not in pows:
            (pr, pi), (qr, qi) = (a_pow(j - 1), pows[1]) if j <= lc else (a_pow(j // 2),) * 2
            assert j <= lc or j % 2 == 0
            pows[j] = (pr * qr - pi * qi, pr * qi + pi * qr)
        return pows[j]

    xr = abr - 1.0
    den = a_re * a_re + a_im * a_im
    coef_r = (xr * a_re + abi * a_im) / den
    coef_i = (abi * a_re - xr * a_im) / den
    btr = bre_ref[...]
    bti = bim_ref[...]
    bbr = coef_r * btr - coef_i * bti
    bbi = coef_r * bti + coef_i * btr
    cre = cre_ref[...]
    cim = cim_ref[...]
    ns_ref[...] = jnp.zeros_like(ns_ref)
    mts_ref[...] = jnp.zeros_like(mts_ref)
    cars, cais = [], []
    for j in range(lc + 1):
        er, ei = a_pow(j)
        car = cre * er - cim * ei
        cai = cre * ei + cim * er
        if j < lc:
            cars.append(car)
            cais.append(cai)
            nbr = er * bbr - ei * bbi
            nbi = er * bbi + ei * bbr
            ncat = jnp.concatenate([nbr, nbi, nbi, nbr], axis=-1).astype(BF16)
            s = lc - 1 - j
            np_ref[:, s * c:(s + 1) * c, :] = ncat
            if j < dseq:
                s = dseq - 1 - j
                ns_ref[:, s * c:(s + 1) * c, :] = ncat
        if j >= 1:
            mcat = jnp.concatenate([car, -cai], axis=-1).astype(BF16)
            mtp_ref[:, (j - 1) * c:j * c, :] = mcat
            if j <= dseq:
                mts_ref[:, (j - 1) * c:j * c, :] = mcat
    ca_r = jnp.concatenate(cars, axis=1)
    ca_i = jnp.concatenate(cais, axis=1)
    r_ref[...] = _bdot3(bbr, ca_r) - _bdot3(bbi, ca_i)
    for idx, pw in enumerate(powers):
        er, ei = a_pow(pw)
        apw_ref[idx, 0] = jnp.concatenate([er, er], axis=-1)
        apw_ref[idx, 1] = jnp.concatenate([-ei, ei], axis=-1)
        apw_ref[idx, 2] = jnp.concatenate([ei, -ei], axis=-1)


def _ssm_prep_call(a_re, a_im, log_dt, b_re, b_im, c_re, c_im, *, lc, dseq, powers):
    depth = a_re.shape[0]
    g, p, c = N_SSM_GROUPS, SSM_STATE, SSM_GROUP
    npw = len(powers)
    a_spec = pl.BlockSpec((None, g, 1, p), lambda l: (l, 0, 0, 0))
    m_spec = pl.BlockSpec((None, g, c, p), lambda l: (l, 0, 0, 0))
    out4 = lambda a, b: pl.BlockSpec((None, g, a, b), lambda l: (l, 0, 0, 0))
    shape4 = lambda a, b, d: jax.ShapeDtypeStruct((depth, g, a, b), d)
    ws = LANES
    return pl.pallas_call(
        functools.partial(_ssm_prep_kernel, lc=lc, dseq=dseq, powers=powers),
        grid=(depth,),
        in_specs=[a_spec, a_spec, a_spec, m_spec, m_spec, m_spec, m_spec],
        out_specs=[out4(c, lc * c), out4(lc * c, 4 * p), out4(lc * c, 2 * p), out4(ws, 4 * p), out4(ws, 2 * p),
                   pl.BlockSpec((None, npw, 3, g, 1, 2 * p), lambda l: (l, 0, 0, 0, 0, 0))],
        out_shape=[shape4(c, lc * c, F32), shape4(lc * c, 4 * p, BF16), shape4(lc * c, 2 * p, BF16),
                   shape4(ws, 4 * p, BF16), shape4(ws, 2 * p, BF16),
                   jax.ShapeDtypeStruct((depth, npw, 3, g, 1, 2 * p), F32)],
        compiler_params=_cparams("arbitrary"),
        name="ssm_prep",
    )(a_re.reshape(depth, g, 1, p), a_im.reshape(depth, g, 1, p),
      jnp.broadcast_to(log_dt[:, :, None, None], (depth, g, 1, p)),
      jnp.swapaxes(b_re, -1, -2), jnp.swapaxes(b_im, -1, -2), c_re, c_im)


def _ssm_kernel(*refs, nsteps, first_power, nchunk, nseq, has_h0):
    if has_h0:
        u_ref, r_ref, n_ref, mt_ref, apw_ref, h0_ref, h0s_ref, y_ref, hend_ref, toe_scr = refs
    else:
        u_ref, r_ref, n_ref, mt_ref, apw_ref, y_ref, hend_ref, toe_scr = refs
    w = u_ref.shape[1]
    c = SSM_GROUP
    r = r_ref[...]
    lane = lax.broadcasted_iota(jnp.int32, r.shape, 1)
    toe_scr[0:c, :] = r.astype(BF16)
    for s in range(1, w // c):
        toe_scr[s * c:(s + 1) * c, :] = jnp.where(lane >= s * c, pltpu.roll(r, s * c, axis=1), 0.0).astype(BF16)
    uf = u_ref[...]
    y = _dot(uf, toe_scr[...])
    st = _dot(uf, n_ref[...])
    ws = 2 * SSM_STATE
    h = st[:, :ws]
    hs = st[:, ws:]
    kidx = lax.broadcasted_iota(jnp.int32, h.shape, 0) & (nchunk - 1)
    shift = lambda x, d: jnp.where(kidx >= d, pltpu.roll(x, d, axis=0), 0.0)
    for i in range(nsteps):
        d = 1 << i
        pw = first_power + i
        a1, a2, a3 = apw_ref[pw, 0], apw_ref[pw, 1], apw_ref[pw, 2]
        hd = shift(h, d)
        hsd = shift(hs, d)
        h, hs = h + a1 * hd + a2 * hsd, hs + a1 * hsd + a3 * hd
    if has_h0:
        h0 = h0_ref[...]
        hin = h0
        h = h + apw_ref[first_power, 0] * h0 + apw_ref[first_power, 1] * h0s_ref[...]
    else:
        hin = shift(h, 1)
    y_ref[...] = y + lax.dot_general(hin.astype(BF16), mt_ref[...], (((1,), (1,)), ((), ())),
                                     preferred_element_type=F32)
    if nchunk == 1:
        hend_ref[...] = h
    else:
        hend_ref[...] = jnp.concatenate([h[(b + 1) * nchunk - 1:(b + 1) * nchunk, :] for b in range(nseq)], axis=0)


def _ssm_call(uf, r, nmat, mt, apw, h0=None, h0s=None, *, layer, nchunk, nsteps, first_power):
    g, rows, w = uf.shape
    assert nchunk & (nchunk - 1) == 0, "chunks per sequence must be a power of two"
    nseq = rows // nchunk
    npw = apw.shape[1]
    gspec = lambda a, b: pl.BlockSpec((None, a, b), lambda i: (i, 0, 0))
    lgspec = lambda a, b: pl.BlockSpec((None, None, a, b), lambda i: (layer, i, 0, 0))
    in_specs = [gspec(rows, w), lgspec(SSM_GROUP, w), lgspec(w, 4 * SSM_STATE), lgspec(w, 2 * SSM_STATE),
                pl.BlockSpec((None, npw, 3, None, 1, 2 * SSM_STATE), lambda i: (layer, 0, 0, i, 0, 0))]
    args = [uf, r, nmat, mt, apw]
    if h0 is not None:
        in_specs += [gspec(rows, 2 * SSM_STATE)] * 2
        args += [h0, h0s]
    return pl.pallas_call(
        functools.partial(_ssm_kernel, nsteps=nsteps, first_power=first_power, nchunk=nchunk, nseq=nseq,
                          has_h0=h0 is not None),
        grid=(g,),
        in_specs=in_specs,
        out_specs=[gspec(rows, w), gspec(nseq, 2 * SSM_STATE)],
        out_shape=[jax.ShapeDtypeStruct((g, rows, w), F32),
                   jax.ShapeDtypeStruct((g, nseq, 2 * SSM_STATE), F32)],
        scratch_shapes=[pltpu.VMEM((w, w), BF16)],
        compiler_params=_cparams("arbitrary"),
        name="ssm",
    )(*args)


def _softplus(z):
    return jnp.maximum(z, 0.0) + jnp.log1p(jnp.exp(-jnp.abs(z)))


def _lru_gates(xc, wg_ref, bg_ref, lam_ref):
    gl = _dot(xc.astype(BF16), wg_ref[...]) + bg_ref[...]
    r = jax.nn.sigmoid(gl[:, :D_LRU])
    gi = jax.nn.sigmoid(gl[:, D_LRU:])
    log_a = -LRU_C * r * _softplus(-lam_ref[...])
    a = jnp.exp(log_a)
    mult = jnp.sqrt(1.0 - a * a)
    return a, mult, gi


def _lru_prompt_kernel(xr_ref, yg_ref, cw_ref, cb_ref, wg_ref, bg_ref, lam_ref, o_ref, hl_ref,
                       xp_scr, hc_scr):
    t = pl.program_id(1)
    tl = xr_ref.shape[0]
    halo = SUBLANES

    @pl.when(t == 0)
    def _():
        xp_scr[0:halo, :] = jnp.zeros((halo, D_LRU), F32)
        hc_scr[...] = jnp.zeros((1, D_LRU), F32)

    xp_scr[halo:halo + tl, :] = xr_ref[...]
    xc = cb_ref[...]
    for j in range(LRU_CONV):
        off = halo - (LRU_CONV - 1) + j
        xc = xc + cw_ref[j:j + 1, :] * xp_scr[off:off + tl, :]
    a, mult, gi = _lru_gates(xc, wg_ref, bg_ref, lam_ref)
    row = lax.broadcasted_iota(jnp.int32, (tl, D_LRU), 0)
    mult = jnp.where((row == 0) & (t == 0), 1.0, mult)
    b = mult * gi * xc
    nblk = tl // SUBLANES
    a = a.reshape(nblk, SUBLANES, D_LRU)
    b = b.reshape(nblk, SUBLANES, D_LRU)
    sub = lax.broadcasted_iota(jnp.int32, a.shape, 1)
    d = 1
    while d < SUBLANES:
        keep = sub >= d
        a_sh = jnp.where(keep, pltpu.roll(a, d, axis=1), 1.0)
        b_sh = jnp.where(keep, pltpu.roll(b, d, axis=1), 0.0)
        b = a * b_sh + b
        a = a * a_sh
        d *= 2
    carry = hc_scr[...]
    blocks = []
    for blk in range(nblk):
        hb = a[blk] * carry + b[blk]
        blocks.append(hb)
        carry = hb[SUBLANES - 1:SUBLANES, :]
    h = jnp.concatenate(blocks, axis=0)
    o_ref[...] = h * jax.nn.gelu(yg_ref[...])
    last = carry
    hc_scr[...] = last
    hl_ref[...] = jnp.broadcast_to(last, (SUBLANES, D_LRU))
    xp_scr[0:halo, :] = xp_scr[tl:tl + halo, :]


def _lru_weight_specs(layer):
    return [_layer_spec((LRU_CONV, D_LRU), layer), _layer_spec((1, D_LRU), layer),
            _layer_spec((D_LRU, 2 * D_LRU), layer), _layer_spec((1, 2 * D_LRU), layer),
            _layer_spec((1, D_LRU), layer)]


def _lru_weight_args(w):
    return (w["lru_conv_w"], w["lru_conv_b"], w["lru_wg"], w["lru_bg"], w["lru_lambda"])


def _lru_prompt_call(xr, yg, w, *, layer, bsz, seq_len):
    tl = min(LRU_TILE, seq_len)
    nt = seq_len // tl
    row = pl.BlockSpec((tl, D_LRU), lambda b, t: (b * nt + t, 0))
    return pl.pallas_call(
        _lru_prompt_kernel,
        grid=(bsz, nt),
        in_specs=[row, row] + _lru_weight_specs(layer),
        out_specs=[row, pl.BlockSpec((None, SUBLANES, D_LRU), lambda b, t: (b, 0, 0))],
        out_shape=[jax.ShapeDtypeStruct((bsz * seq_len, D_LRU), F32),
                   jax.ShapeDtypeStruct((bsz, SUBLANES, D_LRU), F32)],
        scratch_shapes=[pltpu.VMEM((tl + 2 * SUBLANES, D_LRU), F32), pltpu.VMEM((1, D_LRU), F32)],
        compiler_params=_cparams("arbitrary", "arbitrary"),
        name="lru_prompt",
    )(xr, yg, *_lru_weight_args(w))


def _lru_sample_kernel(xr_ref, yg_ref, buf_ref, h0_ref, cw_ref, cb_ref, wg_ref, bg_ref, lam_ref,
                       o_ref, hl_ref, *, dseq):
    n = h0_ref.shape[0]
    xp = [buf_ref[j] for j in range(LRU_CONV - 1)] + [xr_ref[pl.ds(t * n, n), :] for t in range(dseq)]
    xcs = []
    for t in range(dseq):
        xc = cb_ref[...]
        for j in range(LRU_CONV):
            xc = xc + cw_ref[j:j + 1, :] * xp[t + j]
        xcs.append(xc)
    xc = jnp.concatenate(xcs, axis=0)
    a, mult, gi = _lru_gates(xc, wg_ref, bg_ref, lam_ref)
    b = mult * gi * xc
    h = h0_ref[...]
    for t in range(dseq):
        h = a[t * n:(t + 1) * n] * h + b[t * n:(t + 1) * n]
        o_ref[pl.ds(t * n, n), :] = h * jax.nn.gelu(yg_ref[pl.ds(t * n, n), :])
    hl_ref[...] = h


def _lru_sample_call(xr, yg, buf_tm, h0, w, *, layer, dseq):
    n = h0.shape[0]
    full = lambda a: pl.BlockSpec(a.shape, lambda i: (0,) * a.ndim)
    return pl.pallas_call(
        functools.partial(_lru_sample_kernel, dseq=dseq),
        grid=(1,),
        in_specs=[full(xr), full(yg), full(buf_tm), full(h0)] + _lru_weight_specs(layer),
        out_specs=[pl.BlockSpec((dseq * n, D_LRU), lambda i: (0, 0)), pl.BlockSpec((n, D_LRU), lambda i: (0, 0))],
        out_shape=[jax.ShapeDtypeStruct((dseq * n, D_LRU), F32), jax.ShapeDtypeStruct((n, D_LRU), F32)],
        compiler_params=_cparams("arbitrary"),
        name="lru_sample",
    )(xr, yg, buf_tm, h0, *_lru_weight_args(w))


def _post_kernel(x_ref, oa_ref, ys_ref, u_ref, ol_ref, g1_ref, sh2_ref, sc2_ref, g2_ref,
                 d_ref, wglu_ref, bglu_ref, on_ref, wo_ref, n2_ref, wup_ref, cw_ref, cb_ref, wdn_ref,
                 halo_in_ref, xo_ref, halo_out_ref, h2_scr, acc_scr, halo_scr, ys_scr,
                 *, per_token, row_shift, group_major):
    t = pl.program_id(1)
    tm = x_ref.shape[0]
    hrows = halo_in_ref.shape[0]

    @pl.when(t == 0)
    def _():
        halo_scr[...] = halo_in_ref[...]

    parts = 1 if per_token else max(1, tm // POST_PART_ROWS)
    tp = tm // parts
    if group_major:
        _from_group_major(ys_ref, ys_scr, ys_ref.shape[1], SSM_CHUNK)
    on = on_ref[...]
    c1 = D_ATTN
    c2 = c1 + D_SSM
    rowh = lax.broadcasted_iota(jnp.int32, (hrows, FF_CHUNK), 0)

    def ff_cols(c, gv):
        return slice(gv * D_FF + c * FF_CHUNK, gv * D_FF + (c + 1) * FF_CHUNK)

    for p in range(parts):
        rows = slice(p * tp, (p + 1) * tp)
        if group_major:
            ys = jnp.concatenate([ys_scr[h, rows, :] for h in range(SSM_HALVES)], axis=1)
        else:
            ys = ys_ref[rows, :]
        ys = ys + d_ref[...] * u_ref[rows, :]
        gs = jax.nn.gelu(ys)
        o_ssm = gs * jax.nn.sigmoid(_dot(gs.astype(BF16), wglu_ref[...]) + bglu_ref[...])
        o = jnp.concatenate([_rms(oa_ref[rows, :], on[:, :c1]), _rms(o_ssm, on[:, c1:c2]),
                             _rms(ol_ref[rows, :], on[:, c2:])], axis=-1)
        x1 = x_ref[rows, :] + _mod_rows(g1_ref, per_token, tm) * _dot(o.astype(BF16), wo_ref[...])
        h2 = _rms(x1, n2_ref[...]) * (1.0 + _mod_rows(sc2_ref, per_token, tm)) + _mod_rows(sh2_ref, per_token, tm)
        h2_scr[rows, :] = h2.astype(BF16)

        def up_proj(c):
            return [_dot(h2_scr[rows, :], wup_ref[:, ff_cols(c, gv)]) for gv in range(2)]

        ups = up_proj(0)
        for c in range(N_FF_CHUNKS):
            cur = ups
            if c + 1 < N_FF_CHUNKS:
                ups = up_proj(c + 1)
            halves = []
            for gv in range(2):
                cols = ff_cols(c, gv)
                up = cur[gv]
                halo = halo_scr[:, cols]
                y = cb_ref[:, cols] + cw_ref[FFN_CONV - 1:FFN_CONV, cols] * up
                for back in range(1, FFN_CONV):
                    sh = back * row_shift
                    r = pltpu.roll(up, sh, axis=0)
                    hr = pltpu.roll(halo, sh, axis=0) if sh % hrows else halo
                    head = jnp.where(rowh < sh, hr, r[:hrows])
                    shifted = jnp.concatenate([head, r[hrows:]], axis=0)
                    j = FFN_CONV - 1 - back
                    y = y + cw_ref[j:j + 1, cols] * shifted
                halo_scr[:, cols] = up[tp - hrows:, :]
                halves.append(y)
            act = (jax.nn.gelu(halves[0]) * halves[1]).astype(BF16)
            contrib = _dot(act, wdn_ref[c * FF_CHUNK:(c + 1) * FF_CHUNK, :])
            if c == 0:
                acc_scr[rows, :] = contrib
            else:
                acc_scr[rows, :] += contrib
        xo_ref[rows, :] = x1 + _mod_rows(g2_ref, per_token, tm) * acc_scr[rows, :]
    halo_out_ref[...] = halo_scr[...]


def _post_call(x, oa, ys, u, ol, mod, w, halo_in, *, layer, per_token, seq_len, row_shift, mod_seqs=None):
    nt = x.shape[0]
    tm = nt if per_token else min(POST_TILE, nt)
    tiles_per_seq = seq_len // tm
    nseq = nt // seq_len
    hrows = halo_in.shape[1]
    row = lambda wd: pl.BlockSpec((tm, wd), lambda s, t: (s * tiles_per_seq + t, 0))
    mspec = lambda piece: _mod_spec(layer, piece, per_token, lambda s, t: s)(mod_seqs)
    halo_spec = pl.BlockSpec((None, hrows, 2 * D_FF), lambda s, t: (s, 0, 0))
    group_major = ys.ndim == 3
    if group_major:
        ys_spec = pl.BlockSpec((N_SSM_GROUPS, tm // SSM_CHUNK, SSM_CHUNK * SSM_GROUP),
                               lambda s, t: (0, s * tiles_per_seq + t, 0))
    else:
        ys_spec = row(D_SSM)
    return pl.pallas_call(
        functools.partial(_post_kernel, per_token=per_token, row_shift=row_shift, group_major=group_major),
        grid=(nseq, tiles_per_seq),
        in_specs=[
            row(D_MODEL), row(D_ATTN), ys_spec, row(D_SSM), row(D_LRU),
            mspec(2), mspec(3), mspec(4), mspec(5),
            _layer_spec((1, D_SSM), layer), _layer_spec((D_SSM, D_SSM), layer), _layer_spec((1, D_SSM), layer),
            _layer_spec((1, D_MODEL), layer), _layer_spec((D_MODEL, D_MODEL), layer, True),
            _layer_spec((1, D_MODEL), layer),
            _layer_spec((D_MODEL, 2 * D_FF), layer, True),
            _layer_spec((FFN_CONV, 2 * D_FF), layer),
            _layer_spec((1, 2 * D_FF), layer),
            _layer_spec((D_FF, D_MODEL), layer, True),
            halo_spec,
        ],
        out_specs=[row(D_MODEL), halo_spec],
        out_shape=[jax.ShapeDtypeStruct((nt, D_MODEL), F32),
                   jax.ShapeDtypeStruct((nseq, hrows, 2 * D_FF), F32)],
        scratch_shapes=[pltpu.VMEM((tm, D_MODEL), BF16), pltpu.VMEM((tm, D_MODEL), F32),
                        pltpu.VMEM((hrows, 2 * D_FF), F32),
                        pltpu.VMEM((SSM_HALVES, tm, LANES), F32)],
        compiler_params=_cparams("arbitrary", "arbitrary"),
        name="post",
    )(x, oa, ys, u, ol, mod, mod, mod, mod, w["ssm_d"], w["ssm_w_glu"], w["ssm_b_glu"], w["out_norm"],
      w["w_o"], w["norm2"], w["w_up"], w["ffn_conv_w"], w["ffn_conv_b"], w["w_down"], halo_in)


def _block_diag(w):
    depth, nb, bs, _ = w.shape
    eye = jnp.eye(nb, dtype=w.dtype)
    return jnp.einsum("lhij,hk->lhikj", w, eye).reshape(depth, nb * bs, nb * bs)


def _prepare_weights(p):
    depth = p["w_in"].shape[0]
    row = lambda a: a.reshape(depth, 1, -1)
    return dict(
        norm1=row(p["norm1"]), norm2=row(p["norm2"]), out_norm=row(p["out_norm"]),
        w_in=p["w_in"].astype(BF16), w_o=p["w_o"].astype(BF16),
        q_norm=row(jnp.tile(p["q_norm"], (1, N_HEADS))), k_norm=row(jnp.tile(p["k_norm"], (1, N_KV_HEADS))),
        ssm_d=row(p["ssm_d"]), ssm_w_glu=p["ssm_w_glu"].astype(BF16), ssm_b_glu=row(p["ssm_b_glu"]),
        lru_conv_w=p["lru_conv_w"], lru_conv_b=row(p["lru_conv_b"]),
        lru_wg=jnp.concatenate([_block_diag(p["lru_w_a"]), _block_diag(p["lru_w_i"])], axis=-1).astype(BF16),
        lru_bg=row(jnp.concatenate([p["lru_b_a"], p["lru_b_i"]], axis=-1)),
        lru_lambda=row(p["lru_lambda"]),
        w_up=p["ffn_w_up"].astype(BF16), ffn_conv_w=p["ffn_conv_w"], ffn_conv_b=row(p["ffn_conv_b"]),
        w_down=p["ffn_w_down"].astype(BF16),
    )


def _prompt_layer(x, mod, w, seg, sinks, ssm_tabs, cos, sin, *, layer, bsz, seq_len, nsteps):
    nchunk = seq_len // SSM_CHUNK
    q, k, v, u, xr, yg, uf = _pre_call(x, mod, w, seg, cos, sin, layer=layer, per_token=False, seq_len=seq_len)
    oa = _attn_prompt_call(sinks, q, k, v, layer=layer, bsz=bsz, seq_len=seq_len)
    r_all, np_all, mtp_all, _, _, apw = ssm_tabs
    yf, hend = _ssm_call(uf, r_all, np_all, mtp_all, apw, layer=layer, nchunk=nchunk, nsteps=nsteps,
                         first_power=1)
    ol, hl = _lru_prompt_call(xr, yg, w, layer=layer, bsz=bsz, seq_len=seq_len)
    halo0 = jnp.zeros((bsz, SUBLANES, 2 * D_FF), F32)
    x_new, halo = _post_call(x, oa, yf, u, ol, mod, w, halo0, layer=layer, per_token=False, seq_len=seq_len,
                             row_shift=1)
    keep = min(WINDOW, seq_len)
    last = lambda a, nrows: a.reshape(bsz, seq_len, a.shape[-1])[:, seq_len - nrows:]
    hend_b = jnp.transpose(hend, (1, 0, 2))
    states = (last(k, keep).reshape(bsz, keep, N_KV_HEADS, HEAD_DIM),
              last(v, keep).reshape(bsz, keep, N_KV_HEADS, HEAD_DIM),
              hend_b[..., :SSM_STATE], hend_b[..., SSM_STATE:],
              hl[:, 0, :],
              last(xr, LRU_CONV - 1),
              halo[:, SUBLANES - (FFN_CONV - 1):, :])
    return x_new, states


def _sample_layer(x, mod, w, seg, sinks, ssm_tabs, cos, sin, st, *, layer, n, dseq):
    g = N_SSM_GROUPS
    ck, cv, s_re, s_im, lru_h, lru_conv, ffn_conv = st
    nt = n * dseq
    q, k, v, u, xr, yg = _pre_call(x, mod, w, seg, cos, sin, layer=layer, per_token=True, seq_len=nt, nseq=n)
    q_sm = jnp.transpose(q.reshape(dseq, n, N_KV_HEADS, KV_GROUP, HEAD_DIM), (1, 2, 0, 3, 4))
    q_sm = q_sm.reshape(n, N_KV_HEADS, dseq * KV_GROUP, HEAD_DIM)
    kn = jnp.transpose(k.reshape(dseq, n, D_KV), (1, 0, 2))
    vn = jnp.transpose(v.reshape(dseq, n, D_KV), (1, 0, 2))
    wbuf = ck.shape[1]
    o_sm, sk, sv = _attn_sample_call(sinks, q_sm, kn, vn, ck.reshape(n, wbuf, D_KV), cv.reshape(n, wbuf, D_KV),
                                     layer=layer, dseq=dseq)
    oa = jnp.transpose(o_sm.reshape(n, N_KV_HEADS, dseq, KV_GROUP, HEAD_DIM), (2, 0, 1, 3, 4)).reshape(nt, D_ATTN)
    r_all, _, _, ns_all, mts_all, apw = ssm_tabs
    wpad = LANES - dseq * SSM_GROUP
    uf = jnp.transpose(u.reshape(dseq, n, g, SSM_GROUP), (2, 1, 0, 3)).reshape(g, n, dseq * SSM_GROUP)
    uf = jnp.pad(uf, ((0, 0), (0, 0), (0, wpad))).astype(BF16)
    h_re = jnp.transpose(s_re, (1, 0, 2))
    h_im = jnp.transpose(s_im, (1, 0, 2))
    h0 = jnp.concatenate([h_re, h_im], axis=-1)
    h0s = jnp.concatenate([h_im, h_re], axis=-1)
    yf, hend = _ssm_call(uf, r_all, ns_all, mts_all, apw, h0, h0s, layer=layer, nchunk=1, nsteps=0, first_power=0)
    yf = yf[:, :, :dseq * SSM_GROUP]
    ys = jnp.transpose(yf.reshape(g, n, dseq, SSM_GROUP), (2, 1, 0, 3)).reshape(nt, D_SSM)
    ol, hl = _lru_sample_call(xr, yg, jnp.transpose(lru_conv, (1, 0, 2)), lru_h, w, layer=layer, dseq=dseq)
    nconv = FFN_CONV - 1
    halo0 = jnp.transpose(ffn_conv, (1, 0, 2)).reshape(1, nconv * n, 2 * D_FF)
    x_new, halo = _post_call(x, oa, ys, u, ol, mod, w, halo0, layer=layer, per_token=True, seq_len=nt,
                             row_shift=n, mod_seqs=n)
    hend_b = jnp.transpose(hend, (1, 0, 2))
    xr_tm = xr.reshape(dseq, n, D_LRU)
    lru_conv_all = jnp.concatenate([jnp.transpose(lru_conv, (1, 0, 2)), xr_tm], axis=0)
    states = (sk.reshape(n, wbuf, N_KV_HEADS, HEAD_DIM), sv.reshape(n, wbuf, N_KV_HEADS, HEAD_DIM),
              hend_b[..., :SSM_STATE], hend_b[..., SSM_STATE:],
              hl,
              jnp.transpose(lru_conv_all[dseq:], (1, 0, 2)),
              jnp.transpose(halo.reshape(nconv, n, 2 * D_FF), (1, 0, 2)))
    return x_new, states


def kernel(x_prompt, x_sample, cache_k, cache_v, state_ssm_re, state_ssm_im, state_lru_h, state_lru_conv,
           state_ffn_conv, c_prompt, c_sample, w_ada, b_ada, norm1, w_in, q_norm, k_norm, sinks, ssm_a_re,
           ssm_a_im, ssm_b_re, ssm_b_im, ssm_c_re, ssm_c_im, ssm_d, ssm_log_dt, ssm_w_glu, ssm_b_glu,
           lru_conv_w, lru_conv_b, lru_w_a, lru_b_a, lru_w_i, lru_b_i, lru_lambda, out_norm, w_o, norm2,
           ffn_w_up, ffn_conv_w, ffn_conv_b, ffn_w_down):
    bsz, seq_len = x_prompt.shape[:2]
    n, dseq = x_sample.shape[:2]
    depth = w_in.shape[0]
    params = dict(norm1=norm1, w_in=w_in, q_norm=q_norm, k_norm=k_norm, ssm_d=ssm_d,
                  ssm_w_glu=ssm_w_glu, ssm_b_glu=ssm_b_glu, lru_conv_w=lru_conv_w, lru_conv_b=lru_conv_b,
                  lru_w_a=lru_w_a, lru_b_a=lru_b_a, lru_w_i=lru_w_i, lru_b_i=lru_b_i, lru_lambda=lru_lambda,
                  out_norm=out_norm, w_o=w_o, norm2=norm2, ffn_w_up=ffn_w_up, ffn_conv_w=ffn_conv_w,
                  ffn_conv_b=ffn_conv_b, ffn_w_down=ffn_w_down)
    weights = _prepare_weights(params)
    seg = (jnp.arange(D_ATTN)[:, None] // HEAD_DIM == jnp.arange(D_ATTN)[None, :] // HEAD_DIM).astype(BF16)

    rows = n + bsz
    pad = (-rows) % SUBLANES
    c_all = jnp.concatenate([c_sample, c_prompt, jnp.zeros((pad, D_MODEL), F32)], axis=0)
    mod_all = _ada_call(c_all, w_ada, b_ada)
    mod_p = jnp.broadcast_to(mod_all[:, :, n:n + bsz, None, :], (depth, N_MOD, bsz, SUBLANES, D_MODEL))

    cos_p, sin_p = _rope_call(jnp.arange(seq_len, dtype=F32))
    cos_s, sin_s = _rope_call(jnp.repeat(PAST_LEN + jnp.arange(dseq, dtype=F32), n))

    nchunk = seq_len // SSM_CHUNK
    nsteps = max(nchunk - 1, 0).bit_length()
    powers = (dseq,) + tuple(SSM_CHUNK * (1 << i) for i in range(nsteps))
    assert dseq * SSM_GROUP <= LANES
    ssm_tabs = _ssm_prep_call(ssm_a_re, ssm_a_im, ssm_log_dt, ssm_b_re, ssm_b_im, ssm_c_re, ssm_c_im,
                              lc=SSM_CHUNK, dseq=dseq, powers=powers)

    xp = x_prompt.reshape(bsz * seq_len, D_MODEL)
    xs = jnp.transpose(x_sample, (1, 0, 2)).reshape(dseq * n, D_MODEL)
    new_p, new_s = [], []
    for i in range(depth):
        xp, st_p = _prompt_layer(xp, mod_p, weights, seg, sinks, ssm_tabs, cos_p, sin_p, layer=i, bsz=bsz,
                                 seq_len=seq_len, nsteps=nsteps)
        st_in = (cache_k[i], cache_v[i], state_ssm_re[i], state_ssm_im[i], state_lru_h[i], state_lru_conv[i],
                 state_ffn_conv[i])
        xs, st_s = _sample_layer(xs, mod_all, weights, seg, sinks, ssm_tabs, cos_s, sin_s, st_in, layer=i, n=n,
                                 dseq=dseq)
        new_p.append(st_p)
        new_s.append(st_s)
    pk, pv, p_re, p_im, p_lh, p_lc, p_fc = [jnp.stack(s) for s in zip(*new_p)]
    sk, sv, s_re, s_im, s_lh, s_lc, s_fc = [jnp.stack(s) for s in zip(*new_s)]
    y_p = xp.reshape(bsz, seq_len, D_MODEL)
    y_s = jnp.transpose(xs.reshape(dseq, n, D_MODEL), (1, 0, 2))
    return (y_p, y_s, pk, pv, p_re, p_im, p_lh, p_lc, p_fc, sk, sv, s_re, s_im, s_lh, s_lc, s_fc)
```

```python
import functools

import jax
import jax.numpy as jnp
from jax import lax
from jax.experimental import pallas as pl
from jax.experimental.pallas import tpu as pltpu

F32 = jnp.float32
BF16 = jnp.bfloat16

D_MODEL = 1024
HEAD_DIM = 64
N_HEADS = 8
N_KV_HEADS = 2
KV_GROUP = N_HEADS // N_KV_HEADS
D_ATTN = N_HEADS * HEAD_DIM
D_KV = N_KV_HEADS * HEAD_DIM
WINDOW = 128
ROPE_THETA = 10000.0
PAST_LEN = 8192
D_SSM = 256
SSM_GROUP = 16
N_SSM_GROUPS = 16
SSM_STATE = 64
D_LRU = 256
N_LRU_BLOCKS = 4
LRU_BLOCK = 64
LRU_CONV = 4
LRU_C = 8.0
D_FF = 2816
FFN_CONV = 3
D_IN = D_ATTN + 2 * D_KV + D_SSM + 2 * D_LRU
N_MOD = 6
EPS = 1e-6

SUBLANES = 8
LANES = 128
VMEM_LIMIT_BYTES = 56 * 1024 * 1024

TOKEN_TILE = 1024
ATTN_BLOCK = WINDOW
ATTN_SUBBLOCKS = 4
SSM_CHUNK = 32
LRU_TILE = 1024
FF_CHUNK = 256
N_FF_CHUNKS = D_FF // FF_CHUNK
PRE_PART_ROWS = 256
POST_TILE = 512
POST_PART_ROWS = 256
SAMPLE_ATTN_SEQS = 16
ADA_PIECES = 2
SSM_HALVES = D_SSM // LANES
GROUPS_PER_HALF = LANES // SSM_GROUP


def _cparams(*sem):
    return pltpu.CompilerParams(dimension_semantics=sem, vmem_limit_bytes=VMEM_LIMIT_BYTES)


def _dot(a, b):
    return jnp.dot(a, b, preferred_element_type=F32)


def _split_bf16(x):
    hi = x.astype(BF16)
    lo = (x - hi.astype(F32)).astype(BF16)
    return hi, lo


def _layer_spec(shape, layer, single=False):
    nd = len(shape)
    idx = lambda *_: (layer,) + (0,) * nd
    if single:
        return pl.BlockSpec((None,) + tuple(shape), idx, pipeline_mode=pl.Buffered(1))
    return pl.BlockSpec((None,) + tuple(shape), idx)


def _ada_kernel(c_ref, w_ref, b_ref, o_ref):
    c = c_ref[...]
    s = jax.nn.silu(c).astype(BF16)
    res = _dot(s, w_ref[...].astype(BF16)) + b_ref[...]
    for j in range(ADA_PIECES):
        o_ref[j] = res[:, j * D_MODEL:(j + 1) * D_MODEL]


def _ada_call(c_all, w_ada, b_ada):
    depth = w_ada.shape[0]
    rows = c_all.shape[0]
    cols = ADA_PIECES * D_MODEL
    return pl.pallas_call(
        _ada_kernel,
        grid=(depth, N_MOD // ADA_PIECES),
        in_specs=[
            pl.BlockSpec((rows, D_MODEL), lambda l, j: (0, 0)),
            pl.BlockSpec((None, D_MODEL, cols), lambda l, j: (l, 0, j)),
            pl.BlockSpec((None, 1, cols), lambda l, j: (l, 0, j)),
        ],
        out_specs=pl.BlockSpec((None, ADA_PIECES, rows, D_MODEL), lambda l, j: (l, j, 0, 0)),
        out_shape=jax.ShapeDtypeStruct((depth, N_MOD, rows, D_MODEL), F32),
        compiler_params=_cparams("arbitrary", "arbitrary"),
        name="ada",
    )(c_all, w_ada, b_ada.reshape(depth, 1, N_MOD * D_MODEL))


def _rope_kernel(pos_ref, cos_ref, sin_ref):
    pos = pos_ref[...]
    lane = lax.broadcasted_iota(jnp.int32, pos.shape, 1)
    half = HEAD_DIM // 2
    j = (lane & (half - 1)).astype(F32)
    inv = ROPE_THETA ** (-j / half)
    ang = pos * inv
    cos_ref[...] = jnp.cos(ang)
    s = jnp.sin(ang)
    sin_ref[...] = jnp.where((lane & (HEAD_DIM - 1)) < half, -s, s)


def _rope_call(pos_rows):
    t = pos_rows.shape[0]
    pos_b = jnp.broadcast_to(pos_rows[:, None], (t, LANES))
    return pl.pallas_call(
        _rope_kernel,
        out_shape=(jax.ShapeDtypeStruct((t, LANES), F32), jax.ShapeDtypeStruct((t, LANES), F32)),
        name="rope",
    )(pos_b)


def _mod_rows(ref, per_token, tm):
    if not per_token:
        return ref[0:1, :]
    m = ref[...]
    reps = tm // m.shape[0]
    return jnp.concatenate([m] * reps, axis=0) if reps > 1 else m


def _mod_spec(layer, piece, per_token, seq_of_step):
    if per_token:
        return lambda n: pl.BlockSpec((None, None, n, D_MODEL), lambda *g: (layer, piece, 0, 0))
    return lambda n: pl.BlockSpec((None, None, None, SUBLANES, D_MODEL),
                                  lambda *g: (layer, piece, seq_of_step(*g), 0, 0))


def _rms(x, gain):
    return x * lax.rsqrt(jnp.mean(x * x, axis=-1, keepdims=True) + EPS) * gain


def _head_rms(t, seg, gain):
    ss = _dot((t * t).astype(BF16), seg)
    return t * lax.rsqrt(ss * (1.0 / HEAD_DIM) + EPS) * gain


def _rope(t, cos, sin):
    width = t.shape[1]
    reps = width // LANES
    if reps > 1:
        cos = jnp.concatenate([cos] * reps, axis=1)
        sin = jnp.concatenate([sin] * reps, axis=1)
    lane = lax.broadcasted_iota(jnp.int32, t.shape, 1)
    half = HEAD_DIM // 2
    up = pltpu.roll(t, width - half, axis=1)
    dn = pltpu.roll(t, half, axis=1)
    rot = jnp.where((lane & (HEAD_DIM - 1)) < half, up, dn)
    return t * cos + rot * sin


def _unit_transpose8(vs):
    lane = lax.broadcasted_iota(jnp.int32, vs[0].shape, 1)
    unit = lane >> 4
    for b in range(3):
        d = 1 << b
        bit = (unit >> b) & 1
        new = list(vs)
        for i in range(8):
            if (i >> b) & 1 == 0:
                lo, hi = vs[i], vs[i + d]
                new[i] = jnp.where(bit == 0, lo, pltpu.roll(hi, d * SSM_GROUP, axis=1))
                new[i + d] = jnp.where(bit == 1, hi, pltpu.roll(lo, LANES - d * SSM_GROUP, axis=1))
        vs = new
    return vs


def _to_group_major(u_ref, nk, lc):
    outs = [[None] * (lc // 8) for _ in range(N_SSM_GROUPS)]
    for h in range(SSM_HALVES):
        for tb in range(lc // 8):
            vs = [u_ref[h, pl.ds(tb * 8 + tp, nk, stride=lc), :] for tp in range(8)]
            ws = _unit_transpose8(vs)
            for gp in range(GROUPS_PER_HALF):
                outs[h * GROUPS_PER_HALF + gp][tb] = ws[gp]
    return [jnp.concatenate(o, axis=1) for o in outs]


def _from_group_major(yf_ref, ys_ref, nk, lc):
    for h in range(SSM_HALVES):
        for tb in range(lc // 8):
            ws = [yf_ref[h * GROUPS_PER_HALF + gp, :, tb * LANES:(tb + 1) * LANES] for gp in range(GROUPS_PER_HALF)]
            vs = _unit_transpose8(ws)
            for tp in range(8):
                ys_ref[h, pl.ds(tb * 8 + tp, nk, stride=lc), :] = vs[tp]


def _pre_kernel(x_ref, sh_ref, sc_ref, n1_ref, w_ref, qn_ref, kn_ref, seg_ref, cos_ref, sin_ref,
                q_ref, k_ref, v_ref, u_ref, xr_ref, yg_ref, *maybe_uf_ref, per_token):
    tm = x_ref.shape[0]
    parts = tm // PRE_PART_ROWS if tm % PRE_PART_ROWS == 0 else 1
    tp = tm // parts
    c1 = D_ATTN
    c2 = c1 + D_KV
    c3 = c2 + D_KV
    c4 = c3 + D_SSM
    c5 = c4 + D_LRU
    sc = _mod_rows(sc_ref, per_token, tm)
    sh = _mod_rows(sh_ref, per_token, tm)
    seg = seg_ref[...]

    def project(p):
        rows = slice(p * tp, (p + 1) * tp)
        scp, shp = (sc[rows], sh[rows]) if per_token else (sc, sh)
        hb = (_rms(x_ref[rows, :], n1_ref[...]) * (1.0 + scp) + shp).astype(BF16)
        return _dot(hb, w_ref[...])

    def finish(p, proj):
        rows = slice(p * tp, (p + 1) * tp)
        cos = cos_ref[rows, :]
        sin = sin_ref[rows, :]
        qn = _head_rms(proj[:, :c1], seg, qn_ref[...])
        kn = _head_rms(proj[:, c1:c2], seg[:D_KV, :D_KV], kn_ref[...])
        v_ref[rows, :] = proj[:, c2:c3]
        u_ref[rows, :] = proj[:, c3:c4]
        xr_ref[rows, :] = proj[:, c4:c5]
        yg_ref[rows, :] = proj[:, c5:]
        if maybe_uf_ref:
            u_scr = maybe_uf_ref[1]
            for hh in range(SSM_HALVES):
                u_scr[hh, rows, :] = proj[:, c3 + hh * LANES:c3 + (hh + 1) * LANES]
        q_ref[rows, :] = (_rope(qn, cos, sin) * (HEAD_DIM ** -0.5)).astype(BF16)
        k_ref[rows, :] = _rope(kn, cos, sin)

    nxt = project(0)
    for p in range(parts):
        cur = nxt
        if p + 1 < parts:
            nxt = project(p + 1)
        finish(p, cur)
    if maybe_uf_ref:
        uf_ref, u_scr = maybe_uf_ref
        groups = _to_group_major(u_scr, uf_ref.shape[1], SSM_CHUNK)
        for g in range(N_SSM_GROUPS):
            uf_ref[g] = groups[g].astype(BF16)


def _pre_call(x, mod, w, seg, cos, sin, *, layer, per_token, seq_len, nseq=None):
    nt = x.shape[0]
    tm = min(TOKEN_TILE, nt)
    tiles_per_seq = seq_len // tm
    if per_token:
        tab_spec = pl.BlockSpec((tm, LANES), lambda i: (i, 0))
    else:
        tab_spec = pl.BlockSpec((tm, LANES), lambda i: (i % tiles_per_seq, 0))
    mspec = lambda piece: _mod_spec(layer, piece, per_token, lambda i: i // tiles_per_seq)(nseq)
    row = lambda wd: pl.BlockSpec((tm, wd), lambda i: (i, 0))
    widths = (D_ATTN, D_KV, D_KV, D_SSM, D_LRU, D_LRU)
    dtypes = (BF16, F32, F32, F32, F32, F32)
    out_specs = [row(wd) for wd in widths]
    out_shape = [jax.ShapeDtypeStruct((nt, wd), d) for wd, d in zip(widths, dtypes)]
    scratch = []
    if not per_token:
        nk = tm // SSM_CHUNK
        wf = SSM_CHUNK * SSM_GROUP
        out_specs.append(pl.BlockSpec((N_SSM_GROUPS, nk, wf), lambda i: (0, i, 0)))
        out_shape.append(jax.ShapeDtypeStruct((N_SSM_GROUPS, nt // SSM_CHUNK, wf), BF16))
        scratch.append(pltpu.VMEM((SSM_HALVES, tm, LANES), F32))
    return pl.pallas_call(
        functools.partial(_pre_kernel, per_token=per_token),
        grid=(nt // tm,),
        in_specs=[
            row(D_MODEL), mspec(0), mspec(1),
            _layer_spec((1, D_MODEL), layer),
            _layer_spec((D_MODEL, D_IN), layer, True),
            _layer_spec((1, D_ATTN), layer),
            _layer_spec((1, D_KV), layer),
            pl.BlockSpec((D_ATTN, D_ATTN), lambda i: (0, 0)),
            tab_spec, tab_spec,
        ],
        out_specs=out_specs,
        out_shape=out_shape,
        scratch_shapes=scratch,
        compiler_params=_cparams("arbitrary"),
        name="pre",
    )(x, mod, mod, w["norm1"], w["w_in"], w["q_norm"], w["k_norm"], seg, cos, sin)


def _attn_prompt_kernel(sink_ref, q_ref, kc_ref, kp_ref, vc_ref, vp_ref, o_ref, *, layer, nsub):
    assert (KV_GROUP, N_KV_HEADS, 2 * HEAD_DIM) == (4, 2, LANES)
    i = pl.program_id(1)
    bq = ATTN_BLOCK
    kcat = jnp.concatenate([kp_ref[...], kc_ref[...]], axis=0).astype(BF16)
    vcat = jnp.concatenate([vp_ref[...], vc_ref[...]], axis=0).astype(BF16)
    low = lax.broadcasted_iota(jnp.int32, kcat.shape, 1) < HEAD_DIM

    def half_placed(x):
        zero = jnp.zeros_like(x)
        h0_lo = jnp.where(low, x, zero)
        h1_hi = jnp.where(low, zero, x)
        return [[h0_lo, pltpu.roll(h0_lo, HEAD_DIM, axis=1)], [pltpu.roll(h1_hi, HEAD_DIM, axis=1), h1_hi]]

    kz = half_placed(kcat)
    vz = half_placed(vcat)
    qi = lax.broadcasted_iota(jnp.int32, (bq, 2 * bq), 0)
    si = lax.broadcasted_iota(jnp.int32, (bq, 2 * bq), 1)
    diff = qi + bq - si
    in_window = (diff >= 0) & (diff < WINDOW)
    first_pair = lax.broadcasted_iota(jnp.int32, (2 * bq, 1), 0) < bq
    heads = [(h, e) for h in range(N_KV_HEADS) for e in range(2)]

    def score_block(j):
        band = slice(j * bq, (j + 2) * bq)
        qrows = slice(j * bq, (j + 1) * bq)
        scores = []
        for h, e in heads:
            qh = jnp.concatenate([q_ref[qrows, (2 * h + p) * LANES:(2 * h + p + 1) * LANES] for p in range(2)],
                                 axis=0)
            s = lax.dot_general(qh, kz[h][e][band], (((1,), (1,)), ((), ())), preferred_element_type=F32)
            scores.append(s)
        return scores

    nxt = score_block(0)
    for j in range(nsub):
        scores = nxt
        if j + 1 < nsub:
            nxt = score_block(j + 1)
        if j == 0:
            allowed = in_window & ((si >= bq) | (i > 0))
        else:
            allowed = in_window
        allowed = jnp.concatenate([allowed, allowed], axis=0)
        band = slice(j * bq, (j + 2) * bq)
        qrows = slice(j * bq, (j + 1) * bq)
        probs = []
        for (h, e), s in zip(heads, scores):
            s = jnp.where(allowed, s, -jnp.inf)
            sink = jnp.where(first_pair, sink_ref[layer, KV_GROUP * h + e], sink_ref[layer, KV_GROUP * h + 2 + e])
            m = jnp.maximum(jnp.max(s, axis=-1, keepdims=True), sink)
            p = jnp.exp(s - m)
            denom = jnp.sum(p, axis=-1, keepdims=True) + jnp.exp(sink - m)
            probs.append((p.astype(BF16), denom))
        outs = [_dot(p, vz[h][e][band]) / denom for (h, e), (p, denom) in zip(heads, probs)]
        for h in range(N_KV_HEADS):
            o_pair = outs[2 * h] + outs[2 * h + 1]
            o_ref[qrows, (2 * h) * LANES:(2 * h + 1) * LANES] = o_pair[:bq]
            o_ref[qrows, (2 * h + 1) * LANES:(2 * h + 2) * LANES] = o_pair[bq:]


def _attn_prompt_call(sinks, q, k, v, *, layer, bsz, seq_len):
    nsub = min(ATTN_SUBBLOCKS, seq_len // ATTN_BLOCK)
    bq = nsub * ATTN_BLOCK
    nb = seq_len // bq
    nb_small = seq_len // ATTN_BLOCK
    cur = lambda w: pl.BlockSpec((bq, w), lambda b, i: (b * nb + i, 0))
    prev = lambda w: pl.BlockSpec((ATTN_BLOCK, w), lambda b, i: (b * nb_small + jnp.maximum(i * nsub - 1, 0), 0))
    return pl.pallas_call(
        functools.partial(_attn_prompt_kernel, layer=layer, nsub=nsub),
        grid=(bsz, nb),
        in_specs=[pl.BlockSpec(memory_space=pltpu.SMEM), cur(D_ATTN), cur(D_KV), prev(D_KV), cur(D_KV),
                  prev(D_KV)],
        out_specs=cur(D_ATTN),
        out_shape=jax.ShapeDtypeStruct((bsz * seq_len, D_ATTN), F32),
        compiler_params=_cparams("arbitrary", "arbitrary"),
        name="attn_prompt",
    )(sinks, q, k, k, v, v)


def _attn_sample_kernel(sink_ref, q_ref, kn_ref, vn_ref, ck_ref, cv_ref, o_ref, sk_ref, sv_ref, *, layer, dseq):
    kk = jnp.concatenate([ck_ref[...], kn_ref[...]], axis=1)
    vv = jnp.concatenate([cv_ref[...], vn_ref[...]], axis=1)
    wbuf = ck_ref.shape[1]
    sk_ref[...] = kk[:, dseq:, :]
    sv_ref[...] = vv[:, dseq:, :]
    nq = dseq * KV_GROUP
    nk = wbuf + dseq
    qrow = lax.broadcasted_iota(jnp.int32, (nq, nk), 0)
    j = lax.broadcasted_iota(jnp.int32, (nq, nk), 1)
    diff = (qrow >> 2) + wbuf - j
    allowed = ((diff >= 0) & (diff < WINDOW))[None]
    g_of_row = lax.broadcasted_iota(jnp.int32, (nq, 1), 0) & (KV_GROUP - 1)
    kkb = kk.astype(BF16)
    vvb = vv.astype(BF16)
    for h in range(N_KV_HEADS):
        kh = kkb[:, :, h * HEAD_DIM:(h + 1) * HEAD_DIM]
        vh = vvb[:, :, h * HEAD_DIM:(h + 1) * HEAD_DIM]
        qh = q_ref[:, h]
        s = jnp.einsum("sqd,skd->sqk", qh, kh, preferred_element_type=F32)
        s = jnp.where(allowed, s, -jnp.inf)
        sink = jnp.full((nq, 1), sink_ref[layer, h * KV_GROUP], F32)
        for g in range(1, KV_GROUP):
            sink = jnp.where(g_of_row == g, sink_ref[layer, h * KV_GROUP + g], sink)
        sink = sink[None]
        m = jnp.maximum(jnp.max(s, axis=-1, keepdims=True), sink)
        p = jnp.exp(s - m)
        denom = jnp.sum(p, axis=-1, keepdims=True) + jnp.exp(sink - m)
        o = jnp.einsum("sqk,skd->sqd", p.astype(BF16), vh, preferred_element_type=F32) / denom
        o_ref[:, h] = o


def _attn_sample_call(sinks, q, kn, vn, ck, cv, *, layer, dseq):
    n, wbuf = ck.shape[0], ck.shape[1]
    sb = min(SAMPLE_ATTN_SEQS, n)
    nq = dseq * KV_GROUP
    qspec = pl.BlockSpec((sb, N_KV_HEADS, nq, HEAD_DIM), lambda i: (i, 0, 0, 0))
    nspec = pl.BlockSpec((sb, dseq, D_KV), lambda i: (i, 0, 0))
    cspec = pl.BlockSpec((sb, wbuf, D_KV), lambda i: (i, 0, 0))
    return pl.pallas_call(
        functools.partial(_attn_sample_kernel, layer=layer, dseq=dseq),
        grid=(n // sb,),
        in_specs=[pl.BlockSpec(memory_space=pltpu.SMEM), qspec, nspec, nspec, cspec, cspec],
        out_specs=[qspec, cspec, cspec],
        out_shape=[jax.ShapeDtypeStruct((n, N_KV_HEADS, nq, HEAD_DIM), F32),
                   jax.ShapeDtypeStruct((n, wbuf, D_KV), F32),
                   jax.ShapeDtypeStruct((n, wbuf, D_KV), F32)],
        compiler_params=_cparams("arbitrary"),
        name="attn_sample",
    )(sinks, q, kn, vn, ck, cv)


def _bdot3(a, b):
    dn = (((2,), (2,)), ((0,), (0,)))
    ah, al = _split_bf16(a)
    bh, bl = _split_bf16(b)
    d = lambda x, y: lax.dot_general(x, y, dn, preferred_element_type=F32)
    return d(ah, bh) + d(ah, bl) + d(al, bh)


def _ssm_prep_kernel(are_ref, aim_ref, ldt_ref, bre_ref, bim_ref, cre_ref, cim_ref,
                     r_ref, np_ref, mtp_ref, ns_ref, mts_ref, apw_ref, *, lc, dseq, powers):
    c = SSM_GROUP
    a_re = are_ref[...]
    a_im = aim_ref[...]
    dt = jnp.exp(ldt_ref[...])
    zr = a_re * dt
    zi = a_im * dt

    mag = jnp.exp(zr)
    abr, abi = mag * jnp.cos(zi), mag * jnp.sin(zi)
    pows = {0: (jnp.ones_like(abr), jnp.zeros_like(abi)), 1: (abr, abi)}

    def a_pow(j):
        if j not in pows:
            (pr, pi), (qr, qi) = (a_pow(j - 1), pows[1]) if j <= lc else (a_pow(j // 2),) * 2
            assert j <= lc or j % 2 == 0
            pows[j] = (pr * qr - pi * qi, pr * qi + pi * qr)
        return pows[j]

    xr = abr - 1.0
    den = a_re * a_re + a_im * a_im
    coef_r = (xr * a_re + abi * a_im) / den
    coef_i = (abi * a_re - xr * a_im) / den
    btr = bre_ref[...]
    bti = bim_ref[...]
    bbr = coef_r * btr - coef_i * bti
    bbi = coef_r * bti + coef_i * btr
    cre = cre_ref[...]
    cim = cim_ref[...]
    ns_ref[...] = jnp.zeros_like(ns_ref)
    mts_ref[...] = jnp.zeros_like(mts_ref)
    cars, cais = [], []
    for j in range(lc + 1):
        er, ei = a_pow(j)
        car = cre * er - cim * ei
        cai = cre * ei + cim * er
        if j < lc:
            cars.append(car)
            cais.append(cai)
            nbr = er * bbr - ei * bbi
            nbi = er * bbi + ei * bbr
            ncat = jnp.concatenate([nbr, nbi, nbi, nbr], axis=-1).astype(BF16)
            s = lc - 1 - j
            np_ref[:, s * c:(s + 1) * c, :] = ncat
            if j < dseq:
                s = dseq - 1 - j
                ns_ref[:, s * c:(s + 1) * c, :] = ncat
        if j >= 1:
            mcat = jnp.concatenate([car, -cai], axis=-1).astype(BF16)
            mtp_ref[:, (j - 1) * c:j * c, :] = mcat
            if j <= dseq:
                mts_ref[:, (j - 1) * c:j * c, :] = mcat
    ca_r = jnp.concatenate(cars, axis=1)
    ca_i = jnp.concatenate(cais, axis=1)
    r_ref[...] = _bdot3(bbr, ca_r) - _bdot3(bbi, ca_i)
    for idx, pw in enumerate(powers):
        er, ei = a_pow(pw)
        apw_ref[idx, 0] = jnp.concatenate([er, er], axis=-1)
        apw_ref[idx, 1] = jnp.concatenate([-ei, ei], axis=-1)
        apw_ref[idx, 2] = jnp.concatenate([ei, -ei], axis=-1)


def _ssm_prep_call(a_re, a_im, log_dt, b_re, b_im, c_re, c_im, *, lc, dseq, powers):
    depth = a_re.shape[0]
    g, p, c = N_SSM_GROUPS, SSM_STATE, SSM_GROUP
    npw = len(powers)
    a_spec = pl.BlockSpec((None, g, 1, p), lambda l: (l, 0, 0, 0))
    m_spec = pl.BlockSpec((None, g, c, p), lambda l: (l, 0, 0, 0))
    out4 = lambda a, b: pl.BlockSpec((None, g, a, b), lambda l: (l, 0, 0, 0))
    shape4 = lambda a, b, d: jax.ShapeDtypeStruct((depth, g, a, b), d)
    ws = LANES
    return pl.pallas_call(
        functools.partial(_ssm_prep_kernel, lc=lc, dseq=dseq, powers=powers),
        grid=(depth,),
        in_specs=[a_spec, a_spec, a_spec, m_spec, m_spec, m_spec, m_spec],
        out_specs=[out4(c, lc * c), out4(lc * c, 4 * p), out4(lc * c, 2 * p), out4(ws, 4 * p), out4(ws, 2 * p),
                   pl.BlockSpec((None, npw, 3, g, 1, 2 * p), lambda l: (l, 0, 0, 0, 0, 0))],
        out_shape=[shape4(c, lc * c, F32), shape4(lc * c, 4 * p, BF16), shape4(lc * c, 2 * p, BF16),
                   shape4(ws, 4 * p, BF16), shape4(ws, 2 * p, BF16),
                   jax.ShapeDtypeStruct((depth, npw, 3, g, 1, 2 * p), F32)],
        compiler_params=_cparams("arbitrary"),
        name="ssm_prep",
    )(a_re.reshape(depth, g, 1, p), a_im.reshape(depth, g, 1, p),
      jnp.broadcast_to(log_dt[:, :, None, None], (depth, g, 1, p)),
      jnp.swapaxes(b_re, -1, -2), jnp.swapaxes(b_im, -1, -2), c_re, c_im)


def _ssm_kernel(*refs, nsteps, first_power, nchunk, nseq, has_h0):
    if has_h0:
        u_ref, r_ref, n_ref, mt_ref, apw_ref, h0_ref, h0s_ref, y_ref, hend_ref, toe_scr = refs
    else:
        u_ref, r_ref, n_ref, mt_ref, apw_ref, y_ref, hend_ref, toe_scr = refs
    w = u_ref.shape[1]
    c = SSM_GROUP
    r = r_ref[...]
    lane = lax.broadcasted_iota(jnp.int32, r.shape, 1)
    toe_scr[0:c, :] = r.astype(BF16)
    for s in range(1, w // c):
        toe_scr[s * c:(s + 1) * c, :] = jnp.where(lane >= s * c, pltpu.roll(r, s * c, axis=1), 0.0).astype(BF16)
    uf = u_ref[...]
    y = _dot(uf, toe_scr[...])
    st = _dot(uf, n_ref[...])
    ws = 2 * SSM_STATE
    h = st[:, :ws]
    hs = st[:, ws:]
    kidx = lax.broadcasted_iota(jnp.int32, h.shape, 0) & (nchunk - 1)
    shift = lambda x, d: jnp.where(kidx >= d, pltpu.roll(x, d, axis=0), 0.0)
    for i in range(nsteps):
        d = 1 << i
        pw = first_power + i
        a1, a2, a3 = apw_ref[pw, 0], apw_ref[pw, 1], apw_ref[pw, 2]
        hd = shift(h, d)
        hsd = shift(hs, d)
        h, hs = h + a1 * hd + a2 * hsd, hs + a1 * hsd + a3 * hd
    if has_h0:
        h0 = h0_ref[...]
        hin = h0
        h = h + apw_ref[first_power, 0] * h0 + apw_ref[first_power, 1] * h0s_ref[...]
    else:
        hin = shift(h, 1)
    y_ref[...] = y + lax.dot_general(hin.astype(BF16), mt_ref[...], (((1,), (1,)), ((), ())),
                                     preferred_element_type=F32)
    if nchunk == 1:
        hend_ref[...] = h
    else:
        hend_ref[...] = jnp.concatenate([h[(b + 1) * nchunk - 1:(b + 1) * nchunk, :] for b in range(nseq)], axis=0)


def _ssm_call(uf, r, nmat, mt, apw, h0=None, h0s=None, *, layer, nchunk, nsteps, first_power):
    g, rows, w = uf.shape
    assert nchunk & (nchunk - 1) == 0, "chunks per sequence must be a power of two"
    nseq = rows // nchunk
    npw = apw.shape[1]
    gspec = lambda a, b: pl.BlockSpec((None, a, b), lambda i: (i, 0, 0))
    lgspec = lambda a, b: pl.BlockSpec((None, None, a, b), lambda i: (layer, i, 0, 0))
    in_specs = [gspec(rows, w), lgspec(SSM_GROUP, w), lgspec(w, 4 * SSM_STATE), lgspec(w, 2 * SSM_STATE),
                pl.BlockSpec((None, npw, 3, None, 1, 2 * SSM_STATE), lambda i: (layer, 0, 0, i, 0, 0))]
    args = [uf, r, nmat, mt, apw]
    if h0 is not None:
        in_specs += [gspec(rows, 2 * SSM_STATE)] * 2
        args += [h0, h0s]
    return pl.pallas_call(
        functools.partial(_ssm_kernel, nsteps=nsteps, first_power=first_power, nchunk=nchunk, nseq=nseq,
                          has_h0=h0 is not None),
        grid=(g,),
        in_specs=in_specs,
        out_specs=[gspec(rows, w), gspec(nseq, 2 * SSM_STATE)],
        out_shape=[jax.ShapeDtypeStruct((g, rows, w), F32),
                   jax.ShapeDtypeStruct((g, nseq, 2 * SSM_STATE), F32)],
        scratch_shapes=[pltpu.VMEM((w, w), BF16)],
        compiler_params=_cparams("arbitrary"),
        name="ssm",
    )(*args)


def _softplus(z):
    return jnp.maximum(z, 0.0) + jnp.log1p(jnp.exp(-jnp.abs(z)))


def _lru_gates(xc, wg_ref, bg_ref, lam_ref):
    gl = _dot(xc.astype(BF16), wg_ref[...]) + bg_ref[...]
    r = jax.nn.sigmoid(gl[:, :D_LRU])
    gi = jax.nn.sigmoid(gl[:, D_LRU:])
    log_a = -LRU_C * r * _softplus(-lam_ref[...])
    a = jnp.exp(log_a)
    mult = jnp.sqrt(1.0 - a * a)
    return a, mult, gi


def _lru_prompt_kernel(xr_ref, yg_ref, cw_ref, cb_ref, wg_ref, bg_ref, lam_ref, o_ref, hl_ref,
                       xp_scr, hc_scr):
    t = pl.program_id(1)
    tl = xr_ref.shape[0]
    halo = SUBLANES

    @pl.when(t == 0)
    def _():
        xp_scr[0:halo, :] = jnp.zeros((halo, D_LRU), F32)
        hc_scr[...] = jnp.zeros((1, D_LRU), F32)

    xp_scr[halo:halo + tl, :] = xr_ref[...]
    xc = cb_ref[...]
    for j in range(LRU_CONV):
        off = halo - (LRU_CONV - 1) + j
        xc = xc + cw_ref[j:j + 1, :] * xp_scr[off:off + tl, :]
    a, mult, gi = _lru_gates(xc, wg_ref, bg_ref, lam_ref)
    row = lax.broadcasted_iota(jnp.int32, (tl, D_LRU), 0)
    mult = jnp.where((row == 0) & (t == 0), 1.0, mult)
    b = mult * gi * xc
    nblk = tl // SUBLANES
    a = a.reshape(nblk, SUBLANES, D_LRU)
    b = b.reshape(nblk, SUBLANES, D_LRU)
    sub = lax.broadcasted_iota(jnp.int32, a.shape, 1)
    d = 1
    while d < SUBLANES:
        keep = sub >= d
        a_sh = jnp.where(keep, pltpu.roll(a, d, axis=1), 1.0)
        b_sh = jnp.where(keep, pltpu.roll(b, d, axis=1), 0.0)
        b = a * b_sh + b
        a = a * a_sh
        d *= 2
    carry = hc_scr[...]
    blocks = []
    for blk in range(nblk):
        hb = a[blk] * carry + b[blk]
        blocks.append(hb)
        carry = hb[SUBLANES - 1:SUBLANES, :]
    h = jnp.concatenate(blocks, axis=0)
    o_ref[...] = h * jax.nn.gelu(yg_ref[...])
    last = carry
    hc_scr[...] = last
    hl_ref[...] = jnp.broadcast_to(last, (SUBLANES, D_LRU))
    xp_scr[0:halo, :] = xp_scr[tl:tl + halo, :]


def _lru_weight_specs(layer):
    return [_layer_spec((LRU_CONV, D_LRU), layer), _layer_spec((1, D_LRU), layer),
            _layer_spec((D_LRU, 2 * D_LRU), layer), _layer_spec((1, 2 * D_LRU), layer),
            _layer_spec((1, D_LRU), layer)]


def _lru_weight_args(w):
    return (w["lru_conv_w"], w["lru_conv_b"], w["lru_wg"], w["lru_bg"], w["lru_lambda"])


def _lru_prompt_call(xr, yg, w, *, layer, bsz, seq_len):
    tl = min(LRU_TILE, seq_len)
    nt = seq_len // tl
    row = pl.BlockSpec((tl, D_LRU), lambda b, t: (b * nt + t, 0))
    return pl.pallas_call(
        _lru_prompt_kernel,
        grid=(bsz, nt),
        in_specs=[row, row] + _lru_weight_specs(layer),
        out_specs=[row, pl.BlockSpec((None, SUBLANES, D_LRU), lambda b, t: (b, 0, 0))],
        out_shape=[jax.ShapeDtypeStruct((bsz * seq_len, D_LRU), F32),
                   jax.ShapeDtypeStruct((bsz, SUBLANES, D_LRU), F32)],
        scratch_shapes=[pltpu.VMEM((tl + 2 * SUBLANES, D_LRU), F32), pltpu.VMEM((1, D_LRU), F32)],
        compiler_params=_cparams("arbitrary", "arbitrary"),
        name="lru_prompt",
    )(xr, yg, *_lru_weight_args(w))


def _lru_sample_kernel(xr_ref, yg_ref, buf_ref, h0_ref, cw_ref, cb_ref, wg_ref, bg_ref, lam_ref,
                       o_ref, hl_ref, *, dseq):
    n = h0_ref.shape[0]
    xp = [buf_ref[j] for j in range(LRU_CONV - 1)] + [xr_ref[pl.ds(t * n, n), :] for t in range(dseq)]
    xcs = []
    for t in range(dseq):
        xc = cb_ref[...]
        for j in range(LRU_CONV):
            xc = xc + cw_ref[j:j + 1, :] * xp[t + j]
        xcs.append(xc)
    xc = jnp.concatenate(xcs, axis=0)
    a, mult, gi = _lru_gates(xc, wg_ref, bg_ref, lam_ref)
    b = mult * gi * xc
    h = h0_ref[...]
    for t in range(dseq):
        h = a[t * n:(t + 1) * n] * h + b[t * n:(t + 1) * n]
        o_ref[pl.ds(t * n, n), :] = h * jax.nn.gelu(yg_ref[pl.ds(t * n, n), :])
    hl_ref[...] = h


def _lru_sample_call(xr, yg, buf_tm, h0, w, *, layer, dseq):
    n = h0.shape[0]
    full = lambda a: pl.BlockSpec(a.shape, lambda i: (0,) * a.ndim)
    return pl.pallas_call(
        functools.partial(_lru_sample_kernel, dseq=dseq),
        grid=(1,),
        in_specs=[full(xr), full(yg), full(buf_tm), full(h0)] + _lru_weight_specs(layer),
        out_specs=[pl.BlockSpec((dseq * n, D_LRU), lambda i: (0, 0)), pl.BlockSpec((n, D_LRU), lambda i: (0, 0))],
        out_shape=[jax.ShapeDtypeStruct((dseq * n, D_LRU), F32), jax.ShapeDtypeStruct((n, D_LRU), F32)],
        compiler_params=_cparams("arbitrary"),
        name="lru_sample",
    )(xr, yg, buf_tm, h0, *_lru_weight_args(w))


def _post_kernel(x_ref, oa_ref, ys_ref, u_ref, ol_ref, g1_ref, sh2_ref, sc2_ref, g2_ref,
                 d_ref, wglu_ref, bglu_ref, on_ref, wo_ref, n2_ref, wup_ref, cw_ref, cb_ref, wdn_ref,
                 halo_in_ref, xo_ref, halo_out_ref, h2_scr, acc_scr, halo_scr, ys_scr,
                 *, per_token, row_shift, group_major):
    t = pl.program_id(1)
    tm = x_ref.shape[0]
    hrows = halo_in_ref.shape[0]

    @pl.when(t == 0)
    def _():
        halo_scr[...] = halo_in_ref[...]

    parts = 1 if per_token else max(1, tm // POST_PART_ROWS)
    tp = tm // parts
    if group_major:
        _from_group_major(ys_ref, ys_scr, ys_ref.shape[1], SSM_CHUNK)
    on = on_ref[...]
    c1 = D_ATTN
    c2 = c1 + D_SSM
    rowh = lax.broadcasted_iota(jnp.int32, (hrows, FF_CHUNK), 0)

    def ff_cols(c, gv):
        return slice(gv * D_FF + c * FF_CHUNK, gv * D_FF + (c + 1) * FF_CHUNK)

    for p in range(parts):
        rows = slice(p * tp, (p + 1) * tp)
        if group_major:
            ys = jnp.concatenate([ys_scr[h, rows, :] for h in range(SSM_HALVES)], axis=1)
        else:
            ys = ys_ref[rows, :]
        ys = ys + d_ref[...] * u_ref[rows, :]
        gs = jax.nn.gelu(ys)
        o_ssm = gs * jax.nn.sigmoid(_dot(gs.astype(BF16), wglu_ref[...]) + bglu_ref[...])
        o = jnp.concatenate([_rms(oa_ref[rows, :], on[:, :c1]), _rms(o_ssm, on[:, c1:c2]),
                             _rms(ol_ref[rows, :], on[:, c2:])], axis=-1)
        x1 = x_ref[rows, :] + _mod_rows(g1_ref, per_token, tm) * _dot(o.astype(BF16), wo_ref[...])
        h2 = _rms(x1, n2_ref[...]) * (1.0 + _mod_rows(sc2_ref, per_token, tm)) + _mod_rows(sh2_ref, per_token, tm)
        h2_scr[rows, :] = h2.astype(BF16)

        def up_proj(c):
            return [_dot(h2_scr[rows, :], wup_ref[:, ff_cols(c, gv)]) for gv in range(2)]

        ups = up_proj(0)
        for c in range(N_FF_CHUNKS):
            cur = ups
            if c + 1 < N_FF_CHUNKS:
                ups = up_proj(c + 1)
            halves = []
            for gv in range(2):
                cols = ff_cols(c, gv)
                up = cur[gv]
                halo = halo_scr[:, cols]
                y = cb_ref[:, cols] + cw_ref[FFN_CONV - 1:FFN_CONV, cols] * up
                for back in range(1, FFN_CONV):
                    sh = back * row_shift
                    r = pltpu.roll(up, sh, axis=0)
                    hr = pltpu.roll(halo, sh, axis=0) if sh % hrows else halo
                    head = jnp.where(rowh < sh, hr, r[:hrows])
                    shifted = jnp.concatenate([head, r[hrows:]], axis=0)
                    j = FFN_CONV - 1 - back
                    y = y + cw_ref[j:j + 1, cols] * shifted
                halo_scr[:, cols] = up[tp - hrows:, :]
                halves.append(y)
            act = (jax.nn.gelu(halves[0]) * halves[1]).astype(BF16)
            contrib = _dot(act, wdn_ref[c * FF_CHUNK:(c + 1) * FF_CHUNK, :])
            if c == 0:
                acc_scr[rows, :] = contrib
            else:
                acc_scr[rows, :] += contrib
        xo_ref[rows, :] = x1 + _mod_rows(g2_ref, per_token, tm) * acc_scr[rows, :]
    halo_out_ref[...] = halo_scr[...]


def _post_call(x, oa, ys, u, ol, mod, w, halo_in, *, layer, per_token, seq_len, row_shift, mod_seqs=None):
    nt = x.shape[0]
    tm = nt if per_token else min(POST_TILE, nt)
    tiles_per_seq = seq_len // tm
    nseq = nt // seq_len
    hrows = halo_in.shape[1]
    row = lambda wd: pl.BlockSpec((tm, wd), lambda s, t: (s * tiles_per_seq + t, 0))
    mspec = lambda piece: _mod_spec(layer, piece, per_token, lambda s, t: s)(mod_seqs)
    halo_spec = pl.BlockSpec((None, hrows, 2 * D_FF), lambda s, t: (s, 0, 0))
    group_major = ys.ndim == 3
    if group_major:
        ys_spec = pl.BlockSpec((N_SSM_GROUPS, tm // SSM_CHUNK, SSM_CHUNK * SSM_GROUP),
                               lambda s, t: (0, s * tiles_per_seq + t, 0))
    else:
        ys_spec = row(D_SSM)
    return pl.pallas_call(
        functools.partial(_post_kernel, per_token=per_token, row_shift=row_shift, group_major=group_major),
        grid=(nseq, tiles_per_seq),
        in_specs=[
            row(D_MODEL), row(D_ATTN), ys_spec, row(D_SSM), row(D_LRU),
            mspec(2), mspec(3), mspec(4), mspec(5),
            _layer_spec((1, D_SSM), layer), _layer_spec((D_SSM, D_SSM), layer), _layer_spec((1, D_SSM), layer),
            _layer_spec((1, D_MODEL), layer), _layer_spec((D_MODEL, D_MODEL), layer, True),
            _layer_spec((1, D_MODEL), layer),
            _layer_spec((D_MODEL, 2 * D_FF), layer, True),
            _layer_spec((FFN_CONV, 2 * D_FF), layer),
            _layer_spec((1, 2 * D_FF), layer),
            _layer_spec((D_FF, D_MODEL), layer, True),
            halo_spec,
        ],
        out_specs=[row(D_MODEL), halo_spec],
        out_shape=[jax.ShapeDtypeStruct((nt, D_MODEL), F32),
                   jax.ShapeDtypeStruct((nseq, hrows, 2 * D_FF), F32)],
        scratch_shapes=[pltpu.VMEM((tm, D_MODEL), BF16), pltpu.VMEM((tm, D_MODEL), F32),
                        pltpu.VMEM((hrows, 2 * D_FF), F32),
                        pltpu.VMEM((SSM_HALVES, tm, LANES), F32)],
        compiler_params=_cparams("arbitrary", "arbitrary"),
        name="post",
    )(x, oa, ys, u, ol, mod, mod, mod, mod, w["ssm_d"], w["ssm_w_glu"], w["ssm_b_glu"], w["out_norm"],
      w["w_o"], w["norm2"], w["w_up"], w["ffn_conv_w"], w["ffn_conv_b"], w["w_down"], halo_in)


def _block_diag(w):
    depth, nb, bs, _ = w.shape
    eye = jnp.eye(nb, dtype=w.dtype)
    return jnp.einsum("lhij,hk->lhikj", w, eye).reshape(depth, nb * bs, nb * bs)


def _prepare_weights(p):
    depth = p["w_in"].shape[0]
    row = lambda a: a.reshape(depth, 1, -1)
    return dict(
        norm1=row(p["norm1"]), norm2=row(p["norm2"]), out_norm=row(p["out_norm"]),
        w_in=p["w_in"].astype(BF16), w_o=p["w_o"].astype(BF16),
        q_norm=row(jnp.tile(p["q_norm"], (1, N_HEADS))), k_norm=row(jnp.tile(p["k_norm"], (1, N_KV_HEADS))),
        ssm_d=row(p["ssm_d"]), ssm_w_glu=p["ssm_w_glu"].astype(BF16), ssm_b_glu=row(p["ssm_b_glu"]),
        lru_conv_w=p["lru_conv_w"], lru_conv_b=row(p["lru_conv_b"]),
        lru_wg=jnp.concatenate([_block_diag(p["lru_w_a"]), _block_diag(p["lru_w_i"])], axis=-1).astype(BF16),
        lru_bg=row(jnp.concatenate([p["lru_b_a"], p["lru_b_i"]], axis=-1)),
        lru_lambda=row(p["lru_lambda"]),
        w_up=p["ffn_w_up"].astype(BF16), ffn_conv_w=p["ffn_conv_w"], ffn_conv_b=row(p["ffn_conv_b"]),
        w_down=p["ffn_w_down"].astype(BF16),
    )


def _prompt_layer(x, mod, w, seg, sinks, ssm_tabs, cos, sin, *, layer, bsz, seq_len, nsteps):
    nchunk = seq_len // SSM_CHUNK
    q, k, v, u, xr, yg, uf = _pre_call(x, mod, w, seg, cos, sin, layer=layer, per_token=False, seq_len=seq_len)
    oa = _attn_prompt_call(sinks, q, k, v, layer=layer, bsz=bsz, seq_len=seq_len)
    r_all, np_all, mtp_all, _, _, apw = ssm_tabs
    yf, hend = _ssm_call(uf, r_all, np_all, mtp_all, apw, layer=layer, nchunk=nchunk, nsteps=nsteps,
                         first_power=1)
    ol, hl = _lru_prompt_call(xr, yg, w, layer=layer, bsz=bsz, seq_len=seq_len)
    halo0 = jnp.zeros((bsz, SUBLANES, 2 * D_FF), F32)
    x_new, halo = _post_call(x, oa, yf, u, ol, mod, w, halo0, layer=layer, per_token=False, seq_len=seq_len,
                             row_shift=1)
    keep = min(WINDOW, seq_len)
    last = lambda a, nrows: a.reshape(bsz, seq_len, a.shape[-1])[:, seq_len - nrows:]
    hend_b = jnp.transpose(hend, (1, 0, 2))
    states = (last(k, keep).reshape(bsz, keep, N_KV_HEADS, HEAD_DIM),
              last(v, keep).reshape(bsz, keep, N_KV_HEADS, HEAD_DIM),
              hend_b[..., :SSM_STATE], hend_b[..., SSM_STATE:],
              hl[:, 0, :],
              last(xr, LRU_CONV - 1),
              halo[:, SUBLANES - (FFN_CONV - 1):, :])
    return x_new, states


def _sample_layer(x, mod, w, seg, sinks, ssm_tabs, cos, sin, st, *, layer, n, dseq):
    g = N_SSM_GROUPS
    ck, cv, s_re, s_im, lru_h, lru_conv, ffn_conv = st
    nt = n * dseq
    q, k, v, u, xr, yg = _pre_call(x, mod, w, seg, cos, sin, layer=layer, per_token=True, seq_len=nt, nseq=n)
    q_sm = jnp.transpose(q.reshape(dseq, n, N_KV_HEADS, KV_GROUP, HEAD_DIM), (1, 2, 0, 3, 4))
    q_sm = q_sm.reshape(n, N_KV_HEADS, dseq * KV_GROUP, HEAD_DIM)
    kn = jnp.transpose(k.reshape(dseq, n, D_KV), (1, 0, 2))
    vn = jnp.transpose(v.reshape(dseq, n, D_KV), (1, 0, 2))
    wbuf = ck.shape[1]
    o_sm, sk, sv = _attn_sample_call(sinks, q_sm, kn, vn, ck.reshape(n, wbuf, D_KV), cv.reshape(n, wbuf, D_KV),
                                     layer=layer, dseq=dseq)
    oa = jnp.transpose(o_sm.reshape(n, N_KV_HEADS, dseq, KV_GROUP, HEAD_DIM), (2, 0, 1, 3, 4)).reshape(nt, D_ATTN)
    r_all, _, _, ns_all, mts_all, apw = ssm_tabs
    wpad = LANES - dseq * SSM_GROUP
    uf = jnp.transpose(u.reshape(dseq, n, g, SSM_GROUP), (2, 1, 0, 3)).reshape(g, n, dseq * SSM_GROUP)
    uf = jnp.pad(uf, ((0, 0), (0, 0), (0, wpad))).astype(BF16)
    h_re = jnp.transpose(s_re, (1, 0, 2))
    h_im = jnp.transpose(s_im, (1, 0, 2))
    h0 = jnp.concatenate([h_re, h_im], axis=-1)
    h0s = jnp.concatenate([h_im, h_re], axis=-1)
    yf, hend = _ssm_call(uf, r_all, ns_all, mts_all, apw, h0, h0s, layer=layer, nchunk=1, nsteps=0, first_power=0)
    yf = yf[:, :, :dseq * SSM_GROUP]
    ys = jnp.transpose(yf.reshape(g, n, dseq, SSM_GROUP), (2, 1, 0, 3)).reshape(nt, D_SSM)
    ol, hl = _lru_sample_call(xr, yg, jnp.transpose(lru_conv, (1, 0, 2)), lru_h, w, layer=layer, dseq=dseq)
    nconv = FFN_CONV - 1
    halo0 = jnp.transpose(ffn_conv, (1, 0, 2)).reshape(1, nconv * n, 2 * D_FF)
    x_new, halo = _post_call(x, oa, ys, u, ol, mod, w, halo0, layer=layer, per_token=True, seq_len=nt,
                             row_shift=n, mod_seqs=n)
    hend_b = jnp.transpose(hend, (1, 0, 2))
    xr_tm = xr.reshape(dseq, n, D_LRU)
    lru_conv_all = jnp.concatenate([jnp.transpose(lru_conv, (1, 0, 2)), xr_tm], axis=0)
    states = (sk.reshape(n, wbuf, N_KV_HEADS, HEAD_DIM), sv.reshape(n, wbuf, N_KV_HEADS, HEAD_DIM),
              hend_b[..., :SSM_STATE], hend_b[..., SSM_STATE:],
              hl,
              jnp.transpose(lru_conv_all[dseq:], (1, 0, 2)),
              jnp.transpose(halo.reshape(nconv, n, 2 * D_FF), (1, 0, 2)))
    return x_new, states


def kernel(x_prompt, x_sample, cache_k, cache_v, state_ssm_re, state_ssm_im, state_lru_h, state_lru_conv,
           state_ffn_conv, c_prompt, c_sample, w_ada, b_ada, norm1, w_in, q_norm, k_norm, sinks, ssm_a_re,
           ssm_a_im, ssm_b_re, ssm_b_im, ssm_c_re, ssm_c_im, ssm_d, ssm_log_dt, ssm_w_glu, ssm_b_glu,
           lru_conv_w, lru_conv_b, lru_w_a, lru_b_a, lru_w_i, lru_b_i, lru_lambda, out_norm, w_o, norm2,
           ffn_w_up, ffn_conv_w, ffn_conv_b, ffn_w_down):
    bsz, seq_len = x_prompt.shape[:2]
    n, dseq = x_sample.shape[:2]
    depth = w_in.shape[0]
    params = dict(norm1=norm1, w_in=w_in, q_norm=q_norm, k_norm=k_norm, ssm_d=ssm_d,
                  ssm_w_glu=ssm_w_glu, ssm_b_glu=ssm_b_glu, lru_conv_w=lru_conv_w, lru_conv_b=lru_conv_b,
                  lru_w_a=lru_w_a, lru_b_a=lru_b_a, lru_w_i=lru_w_i, lru_b_i=lru_b_i, lru_lambda=lru_lambda,
                  out_norm=out_norm, w_o=w_o, norm2=norm2, ffn_w_up=ffn_w_up, ffn_conv_w=ffn_conv_w,
                  ffn_conv_b=ffn_conv_b, ffn_w_down=ffn_w_down)
    weights = _prepare_weights(params)
    seg = (jnp.arange(D_ATTN)[:, None] // HEAD_DIM == jnp.arange(D_ATTN)[None, :] // HEAD_DIM).astype(BF16)

    rows = n + bsz
    pad = (-rows) % SUBLANES
    c_all = jnp.concatenate([c_sample, c_prompt, jnp.zeros((pad, D_MODEL), F32)], axis=0)
    mod_all = _ada_call(c_all, w_ada, b_ada)
    mod_p = jnp.broadcast_to(mod_all[:, :, n:n + bsz, None, :], (depth, N_MOD, bsz, SUBLANES, D_MODEL))

    cos_p, sin_p = _rope_call(jnp.arange(seq_len, dtype=F32))
    cos_s, sin_s = _rope_call(jnp.repeat(PAST_LEN + jnp.arange(dseq, dtype=F32), n))

    nchunk = seq_len // SSM_CHUNK
    nsteps = max(nchunk - 1, 0).bit_length()
    powers = (dseq,) + tuple(SSM_CHUNK * (1 << i) for i in range(nsteps))
    assert dseq * SSM_GROUP <= LANES
    ssm_tabs = _ssm_prep_call(ssm_a_re, ssm_a_im, ssm_log_dt, ssm_b_re, ssm_b_im, ssm_c_re, ssm_c_im,
                              lc=SSM_CHUNK, dseq=dseq, powers=powers)

    xp = x_prompt.reshape(bsz * seq_len, D_MODEL)
    xs = jnp.transpose(x_sample, (1, 0, 2)).reshape(dseq * n, D_MODEL)
    new_p, new_s = [], []
    for i in range(depth):
        xp, st_p = _prompt_layer(xp, mod_p, weights, seg, sinks, ssm_tabs, cos_p, sin_p, layer=i, bsz=bsz,
                                 seq_len=seq_len, nsteps=nsteps)
        st_in = (cache_k[i], cache_v[i], state_ssm_re[i], state_ssm_im[i], state_lru_h[i], state_lru_conv[i],
                 state_ffn_conv[i])
        xs, st_s = _sample_layer(xs, mod_all, weights, seg, sinks, ssm_tabs, cos_s, sin_s, st_in, layer=i, n=n,
                                 dseq=dseq)
        new_p.append(st_p)
        new_s.append(st_s)
    pk, pv, p_re, p_im, p_lh, p_lc, p_fc = [jnp.stack(s) for s in zip(*new_p)]
    sk, sv, s_re, s_im, s_lh, s_lc, s_fc = [jnp.stack(s) for s in zip(*new_s)]
    y_p = xp.reshape(bsz, seq_len, D_MODEL)
    y_s = jnp.transpose(xs.reshape(dseq, n, D_MODEL), (1, 0, 2))
    return (y_p, y_s, pk, pv, p_re, p_im, p_lh, p_lc, p_fc, sk, sv, s_re, s_im, s_lh, s_lc, s_fc)
```
